```python
import math
import jax, jax.numpy as jnp
from jax import lax
import numpy as np

D_MODEL = 1024
BATCH = 8
SEQ = 2048
DEPTH = 2

CHUNK = 64
MEM_LEN = 256
Q_BLOCK = 128
RMS_EPS = 1e-6

DA_HEADS = 8
DA_QK_DIM = 64
DA_V_DIM = 2 * DA_QK_DIM
DA_QK_WIDTH = DA_HEADS * 2 * DA_QK_DIM
DA_WIDTH = DA_HEADS * DA_V_DIM

POOL_WINDOWS = (2, 4, 8, 16)
POOL_GROUPS = 4
POOL_GROUP_DIM = 128
POOL_WIDTH = POOL_GROUPS * POOL_GROUP_DIM

MEM_HEADS = 4
MEM_HEAD_DIM = 128
MEM_WIDTH = MEM_HEADS * MEM_HEAD_DIM

N_BRANCHES = 3
IN_WIDTH = 2 * DA_QK_WIDTH + DA_WIDTH + POOL_WIDTH + MEM_WIDTH

D_FF_DENSE = 2816
N_EXPERTS = 8
TOP_K = 2
D_FF_EXPERT = 3584
N_DENSE = (DEPTH + 1) // 2
N_MOE = DEPTH // 2

kernel_name = "hybrid_diffattn_pool_memxattn_moe"


def rmsnorm(x, g):
    xf = x.astype(jnp.float32)
    xf = xf * lax.rsqrt(jnp.mean(xf * xf, axis=-1, keepdims=True) + RMS_EPS)
    return xf.astype(x.dtype) * g


def alibi_slopes(n_heads):
    return np.array([2.0 ** (-8.0 * (h + 1) / n_heads) for h in range(n_heads)], dtype=np.float32)


def lambda_init(layer):
    return 0.8 - 0.6 * math.exp(-0.3 * layer)


def diff_attention(q, k, v, lam):
    S = q.shape[1]
    scale = DA_QK_DIM ** -0.5
    slopes = jnp.asarray(alibi_slopes(DA_HEADS))[:, None, None, None]
    outs = []
    for i in range(S // Q_BLOCK):
        q0 = i * Q_BLOCK
        kv_len = q0 + Q_BLOCK
        qb = q[:, q0:kv_len]
        kb = k[:, :kv_len]
        vb = v[:, :kv_len]
        s = jnp.einsum('bqhmd,bkhmd->bhmqk', qb, kb,
                       preferred_element_type=jnp.float32) * scale
        q_pos = jnp.arange(q0, kv_len)
        k_pos = jnp.arange(kv_len)
        allowed = (k_pos // CHUNK)[None, :] <= (q_pos // CHUNK)[:, None]
        dist = jnp.abs(q_pos[:, None] - k_pos[None, :]).astype(jnp.float32)
        s = jnp.where(allowed, s - slopes * dist, -1e30)
        p = jax.nn.softmax(s, axis=-1)
        attn = p[:, :, 0] - lam.astype(jnp.float32) * p[:, :, 1]
        outs.append(jnp.einsum('bhqk,bkhd->bqhd', attn.astype(vb.dtype), vb))
    return jnp.concatenate(outs, axis=1)


def pool_mixer(p, w_grp, scale):
    B, S, _ = p.shape
    pg = p.reshape(B, S, POOL_GROUPS, POOL_GROUP_DIM).astype(jnp.float32)
    cum = lax.cumsum(pg, axis=1)
    cum_pad = jnp.concatenate([jnp.zeros_like(cum[:, :1]), cum], axis=1)
    t = jnp.arange(S)
    pooled = []
    for g, w in enumerate(POOL_WINDOWS):
        lo = jnp.maximum(t + 1 - w, 0)
        cnt = (t + 1 - lo).astype(jnp.float32)[None, :, None]
        pooled.append((cum[:, :, g] - cum_pad[:, lo, g]) / cnt)
    pooled = jnp.stack(pooled, axis=2)
    d = (pooled - pg).astype(p.dtype)
    y = jnp.einsum('bsgc,gcd->bsgd', d, w_grp)
    return y.reshape(B, S, POOL_WIDTH) * scale


def mem_attention(q, mem_h, w_mem_kv, g_q, g_k):
    B, S, _ = q.shape
    M = mem_h.shape[1]
    kv = mem_h @ w_mem_kv
    k = rmsnorm(kv[..., :MEM_WIDTH].reshape(B, M, MEM_HEADS, MEM_HEAD_DIM), g_k)
    v = kv[..., MEM_WIDTH:].reshape(B, M, MEM_HEADS, MEM_HEAD_DIM)
    q = rmsnorm(q.reshape(B, S, MEM_HEADS, MEM_HEAD_DIM), g_q)
    s = jnp.einsum('bqhd,bkhd->bhqk', q, k,
                   preferred_element_type=jnp.float32) * (MEM_HEAD_DIM ** -0.5)
    p = jax.nn.softmax(s, axis=-1).astype(v.dtype)
    return jnp.einsum('bhqk,bkhd->bqhd', p, v).reshape(B, S, MEM_WIDTH)


def swiglu(h, w1, w3, w2):
    return (jax.nn.silu(h @ w1) * (h @ w3)) @ w2


def moe_swiglu(h, w_router, w1, w3, w2):
    B, S, D = h.shape
    ht = h.reshape(B * S, D)
    logits = (ht @ w_router).astype(jnp.float32)
    vals, idx = lax.top_k(logits, TOP_K)
    gates = jax.nn.softmax(vals, axis=-1).astype(h.dtype)
    comb = jnp.einsum('tk,tke->te', gates, jax.nn.one_hot(idx, N_EXPERTS, dtype=h.dtype))
    out = jnp.zeros_like(ht)
    for e in range(N_EXPERTS):
        out = out + comb[:, e:e + 1] * swiglu(ht, w1[e], w3[e], w2[e])
    return out.reshape(B, S, D)


def setup_inputs(seed: int = 0) -> dict:
    key = jax.random.key(seed)
    ks = iter(jax.random.split(key, 40))

    def nrm(shape, scale):
        return jax.random.normal(next(ks), shape, jnp.float32) * scale

    def gain(shape):
        return 1.0 + nrm(shape, 0.1)

    D = D_MODEL
    return {
        "x": nrm((BATCH, SEQ, D), 1.0),
        "mem": nrm((BATCH, MEM_LEN, D), 1.0),
        "norm_mix": gain((DEPTH, D)),
        "norm_mem": gain((DEPTH, D)),
        "norm_ffn": gain((DEPTH, D)),
        "w_in": nrm((DEPTH, D, IN_WIDTH), D ** -0.5),
        "w_gate": nrm((DEPTH, D, N_BRANCHES * D), D ** -0.5),
        "b_gate": nrm((DEPTH, N_BRANCHES * D), 0.1),
        "da_q_norm": gain((DEPTH, 2, DA_QK_DIM)),
        "da_k_norm": gain((DEPTH, 2, DA_QK_DIM)),
        "da_lam_q1": nrm((DEPTH, DA_QK_DIM), 0.1),
        "da_lam_k1": nrm((DEPTH, DA_QK_DIM), 0.1),
        "da_lam_q2": nrm((DEPTH, DA_QK_DIM), 0.1),
        "da_lam_k2": nrm((DEPTH, DA_QK_DIM), 0.1),
        "da_out_norm": gain((DEPTH, DA_V_DIM)),
        "pool_w": nrm((DEPTH, POOL_GROUPS, POOL_GROUP_DIM, POOL_GROUP_DIM), POOL_GROUP_DIM ** -0.5),
        "pool_scale": gain((DEPTH, POOL_WIDTH)),
        "mem_q_norm": gain((DEPTH, MEM_HEAD_DIM)),
        "mem_k_norm": gain((DEPTH, MEM_HEAD_DIM)),
        "w_mem_kv": nrm((DEPTH, D, 2 * MEM_WIDTH), D ** -0.5),
        "w_br_da": nrm((DEPTH, DA_WIDTH, D), DA_WIDTH ** -0.5),
        "w_br_pool": nrm((DEPTH, POOL_WIDTH, D), POOL_WIDTH ** -0.5),
        "w_br_mem": nrm((DEPTH, MEM_WIDTH, D), MEM_WIDTH ** -0.5),
        "w_out": nrm((DEPTH, D, D), D ** -0.5),
        "ffn_w1": nrm((N_DENSE, D, D_FF_DENSE), D ** -0.5),
        "ffn_w3": nrm((N_DENSE, D, D_FF_DENSE), D ** -0.5),
        "ffn_w2": nrm((N_DENSE, D_FF_DENSE, D), D_FF_DENSE ** -0.5),
        "moe_router": nrm((N_MOE, D, N_EXPERTS), D ** -0.5),
        "moe_w1": nrm((N_MOE, N_EXPERTS, D, D_FF_EXPERT), D ** -0.5),
        "moe_w3": nrm((N_MOE, N_EXPERTS, D, D_FF_EXPERT), D ** -0.5),
        "moe_w2": nrm((N_MOE, N_EXPERTS, D_FF_EXPERT, D), D_FF_EXPERT ** -0.5),
    }


def reference(x, mem, norm_mix, norm_mem, norm_ffn, w_in, w_gate, b_gate,
              da_q_norm, da_k_norm, da_lam_q1, da_lam_k1, da_lam_q2, da_lam_k2, da_out_norm,
              pool_w, pool_scale, mem_q_norm, mem_k_norm, w_mem_kv,
              w_br_da, w_br_pool, w_br_mem, w_out,
              ffn_w1, ffn_w3, ffn_w2, moe_router, moe_w1, moe_w3, moe_w2):
    B, S, D = x.shape
    c_q = DA_QK_WIDTH
    c_k = c_q + DA_QK_WIDTH
    c_v = c_k + DA_WIDTH
    c_p = c_v + POOL_WIDTH
    for l in range(DEPTH):
        h = rmsnorm(x, norm_mix[l])
        proj = h @ w_in[l]
        q = rmsnorm(proj[..., :c_q].reshape(B, S, DA_HEADS, 2, DA_QK_DIM), da_q_norm[l])
        k = rmsnorm(proj[..., c_q:c_k].reshape(B, S, DA_HEADS, 2, DA_QK_DIM), da_k_norm[l])
        v = proj[..., c_k:c_v].reshape(B, S, DA_HEADS, DA_V_DIM)
        p_in = proj[..., c_v:c_p]
        mq = proj[..., c_p:]

        lam_0 = lambda_init(l)
        lam = (jnp.exp(jnp.sum(da_lam_q1[l] * da_lam_k1[l]))
               - jnp.exp(jnp.sum(da_lam_q2[l] * da_lam_k2[l])) + lam_0)
        o_da = diff_attention(q, k, v, lam)
        o_da = (rmsnorm(o_da, da_out_norm[l]) * (1.0 - lam_0)).reshape(B, S, DA_WIDTH)

        o_pool = pool_mixer(p_in, pool_w[l], pool_scale[l])
        o_mem = mem_attention(mq, rmsnorm(mem, norm_mem[l]), w_mem_kv[l], mem_q_norm[l], mem_k_norm[l])

        gates = jax.nn.sigmoid(h @ w_gate[l] + b_gate[l]).reshape(B, S, N_BRANCHES, D)
        merged = (gates[:, :, 0] * (o_da @ w_br_da[l])
                  + gates[:, :, 1] * (o_pool @ w_br_pool[l])
                  + gates[:, :, 2] * (o_mem @ w_br_mem[l]))
        x = x + merged @ w_out[l]

        h2 = rmsnorm(x, norm_ffn[l])
        if l % 2 == 0:
            j = l // 2
            x = x + swiglu(h2, ffn_w1[j], ffn_w3[j], ffn_w2[j])
        else:
            j = l // 2
            x = x + moe_swiglu(h2, moe_router[j], moe_w1[j], moe_w3[j], moe_w2[j])
    return x
```

```python
import functools
import math

import jax
import jax.numpy as jnp
from jax import lax
from jax.experimental import pallas as pl
from jax.experimental.pallas import tpu as pltpu

F32 = jnp.float32
BF16 = jnp.bfloat16

D_MODEL = 1024
CHUNK = 64
RMS_EPS = 1e-6
DA_HEADS = 8
DA_QK_DIM = 64
DA_V_DIM = 128
POOL_WINDOWS = (2, 4, 8, 16)
POOL_GROUP_DIM = 128
POOL_WIDTH = 512
MEM_HEADS = 4
MEM_HEAD_DIM = 128
MEM_WIDTH = 512
N_EXPERTS = 8
LOG2E = 1.4426950408889634
MASK_VALUE = -1e30

V7X_VMEM_BYTES = 64 * 1024 * 1024
VMEM_LIMIT_CAP = 56 * 1024 * 1024
LANES = 128

ROW_TILE = 512
ATTN_TILE = 256
POOL_HALO = 16
FFN_ROW_TILE = 1024
FFN_COL_TILE = 512


def _vmem_limit(estimate_bytes):
    return int(min(VMEM_LIMIT_CAP, max(32 * 1024 * 1024, 2 * estimate_bytes)))


def _rms(x):
    return x * lax.rsqrt(jnp.mean(x * x, axis=-1, keepdims=True) + RMS_EPS)


def _dot(a, b):
    return jnp.dot(a, b, preferred_element_type=F32)


def _dot_nt(a, b):
    return lax.dot_general(a, b, (((1,), (1,)), ((), ())), preferred_element_type=F32)


def _in_proj_kernel(x_ref, g_ref, w_ref, o_ref, *, col_tile):
    hb = (_rms(x_ref[...]) * g_ref[...]).astype(BF16)
    n = w_ref.shape[1]
    for c0 in range(0, n, col_tile):
        o_ref[:, c0:c0 + col_tile] = _dot(hb, w_ref[:, c0:c0 + col_tile]).astype(BF16)


def _in_proj(x, g, w_bf16):
    t, d = x.shape
    n = w_bf16.shape[1]
    tm = ROW_TILE
    est = 2 * (tm * d * 4 + d * n * 2 + tm * n * 2) + tm * 1024 * 4
    return pl.pallas_call(
        functools.partial(_in_proj_kernel, col_tile=1024),
        grid=(t // tm,),
        in_specs=[
            pl.BlockSpec((tm, d), lambda i: (i, 0)),
            pl.BlockSpec((1, d), lambda i: (0, 0)),
            pl.BlockSpec((d, n), lambda i: (0, 0)),
        ],
        out_specs=pl.BlockSpec((tm, n), lambda i: (i, 0)),
        out_shape=jax.ShapeDtypeStruct((t, n), BF16),
        compiler_params=pltpu.CompilerParams(
            dimension_semantics=("parallel",), vmem_limit_bytes=_vmem_limit(est)),
        name="in_proj",
    )(x, g, w_bf16)


def _mem_kv_kernel(mem_ref, g_ref, w_ref, gk_ref, k_ref, v_ref):
    mh = (_rms(mem_ref[...]) * g_ref[...]).astype(BF16)
    kv = _dot(mh, w_ref[...])
    for hh in range(MEM_HEADS):
        c0 = hh * MEM_HEAD_DIM
        k_ref[:, c0:c0 + MEM_HEAD_DIM] = (_rms(kv[:, c0:c0 + MEM_HEAD_DIM]) * gk_ref[...]).astype(BF16)
    v_ref[...] = kv[:, MEM_WIDTH:].astype(BF16)


def _mem_kv(mem, g, w_bf16, gk):
    b, m, d = mem.shape
    out = jax.ShapeDtypeStruct((b, m, MEM_WIDTH), BF16)
    return pl.pallas_call(
        _mem_kv_kernel,
        grid=(b,),
        in_specs=[
            pl.BlockSpec((None, m, d), lambda i: (i, 0, 0)),
            pl.BlockSpec((1, d), lambda i: (0, 0)),
            pl.BlockSpec((d, 2 * MEM_WIDTH), lambda i: (0, 0)),
            pl.BlockSpec((1, MEM_HEAD_DIM), lambda i: (0, 0)),
        ],
        out_specs=[pl.BlockSpec((None, m, MEM_WIDTH), lambda i: (i, 0, 0))] * 2,
        out_shape=[out, out],
        compiler_params=pltpu.CompilerParams(dimension_semantics=("parallel",)),
        name="mem_kv",
    )(mem, g, w_bf16, gk)


def _diff_attn_kernel(lam_ref, q_ref, k_ref, v_ref, gq_ref, gk_ref, go_ref, o_ref,
                      qn_scr, kz0_scr, kz1_scr, vext_scr, s0_scr, s1_scr, *, lam_init, seq):
    tq = ATTN_TILE
    head = pl.program_id(1)
    slope = jnp.exp2(-(jnp.zeros((1, LANES), F32) + (head + 1).astype(F32)))[:, :1] * LOG2E

    lam_v = lam_ref[...]
    lam = (jnp.exp(jnp.sum(lam_v[0:1] * lam_v[1:2], axis=-1, keepdims=True))
           - jnp.exp(jnp.sum(lam_v[2:3] * lam_v[3:4], axis=-1, keepdims=True)) + lam_init)

    r_i = lax.broadcasted_iota(jnp.int32, (LANES, LANES), 0) // DA_QK_DIM
    c_i = lax.broadcasted_iota(jnp.int32, (LANES, LANES), 1) // DA_QK_DIM
    group_mean = jnp.where(r_i == c_i, 1.0 / DA_QK_DIM, 0.0).astype(BF16)

    def qk_norm(x_bf16, gain):
        xf = x_bf16.astype(F32)
        sq = xf * xf
        hi = sq.astype(BF16)
        lo = (sq - hi.astype(F32)).astype(BF16)
        ms = _dot(hi, group_mean) + _dot(lo, group_mean)
        return xf * lax.rsqrt(ms + RMS_EPS) * gain

    qn_scr[...] = (qk_norm(q_ref[...], gq_ref[...]) * (DA_QK_DIM ** -0.5 * LOG2E)).astype(BF16)
    kn = qk_norm(k_ref[...], gk_ref[...])
    lane = lax.broadcasted_iota(jnp.int32, (seq, LANES), 1)
    kz0_scr[...] = jnp.where(lane < DA_QK_DIM, kn, 0.0).astype(BF16)
    kz1_scr[...] = jnp.where(lane >= DA_QK_DIM, kn, 0.0).astype(BF16)
    vext_scr[:, :DA_V_DIM] = v_ref[...]
    vext_scr[:, DA_V_DIM:] = jnp.ones((seq, DA_V_DIM), BF16)

    qrel = lax.broadcasted_iota(jnp.int32, (tq, tq), 0)
    krel = lax.broadcasted_iota(jnp.int32, (tq, tq), 1)
    allowed = (krel // CHUNK) <= (qrel // CHUNK)
    diag_bias = slope * (qrel - jnp.abs(qrel - krel)).astype(F32)
    kcol = lax.broadcasted_iota(jnp.int32, (1, tq), 1).astype(F32)

    for i in range(seq // tq):
        q_i = qn_scr[i * tq:(i + 1) * tq, :]
        mx = [jnp.full((tq, LANES), -jnp.inf, F32), jnp.full((tq, LANES), -jnp.inf, F32)]
        for j in range(i + 1):
            ks = slice(j * tq, (j + 1) * tq)
            for m, (kz, s_scr) in enumerate(((kz0_scr, s0_scr), (kz1_scr, s1_scr))):
                s = _dot_nt(q_i, kz[ks, :])
                if j < i:
                    s = s + slope * (kcol + float((j - i) * tq))
                else:
                    s = jnp.where(allowed, s + diag_bias, MASK_VALUE)
                s_scr[:, ks] = s
                for c0 in range(0, tq, LANES):
                    mx[m] = jnp.maximum(mx[m], s[:, c0:c0 + LANES])
        row_max = [jnp.max(mx[0], axis=-1, keepdims=True), jnp.max(mx[1], axis=-1, keepdims=True)]
        acc = [jnp.zeros((tq, 2 * DA_V_DIM), F32), jnp.zeros((tq, 2 * DA_V_DIM), F32)]
        for j in range(i + 1):
            ks = slice(j * tq, (j + 1) * tq)
            for m, s_scr in enumerate((s0_scr, s1_scr)):
                p = jnp.exp2(s_scr[:, ks] - row_max[m]).astype(BF16)
                acc[m] = acc[m] + _dot(p, vext_scr[ks, :])
        o = (acc[0][:, :DA_V_DIM] / acc[0][:, DA_V_DIM:]
             - lam * (acc[1][:, :DA_V_DIM] / acc[1][:, DA_V_DIM:]))
        o = _rms(o) * go_ref[...] * (1.0 - lam_init)
        o_ref[i * tq:(i + 1) * tq, :] = o.astype(BF16)


def _diff_attn(proj3, lam_vecs, gq, gk, go, lam_init):
    b, s, _ = proj3.shape
    hd = 2 * DA_QK_DIM
    kern = functools.partial(_diff_attn_kernel, lam_init=lam_init, seq=s)
    est = (2 * 4 * s * hd * 2 + 3 * s * hd * 2 + s * 2 * hd * 2 + 2 * ATTN_TILE * s * 4
           + 8 * s * hd * 4)
    return pl.pallas_call(
        kern,
        grid=(b, DA_HEADS),
        in_specs=[
            pl.BlockSpec((4, DA_QK_DIM), lambda bi, h: (0, 0)),
            pl.BlockSpec((None, s, hd), lambda bi, h: (bi, 0, h)),
            pl.BlockSpec((None, s, hd), lambda bi, h: (bi, 0, DA_HEADS + h)),
            pl.BlockSpec((None, s, DA_V_DIM), lambda bi, h: (bi, 0, 2 * DA_HEADS + h)),
            pl.BlockSpec((1, hd), lambda bi, h: (0, 0)),
            pl.BlockSpec((1, hd), lambda bi, h: (0, 0)),
            pl.BlockSpec((1, DA_V_DIM), lambda bi, h: (0, 0)),
        ],
        out_specs=pl.BlockSpec((None, s, DA_V_DIM), lambda bi, h: (bi, 0, h)),
        out_shape=jax.ShapeDtypeStruct((b, s, DA_HEADS * DA_V_DIM), BF16),
        scratch_shapes=[
            pltpu.VMEM((s, hd), BF16),
            pltpu.VMEM((s, hd), BF16),
            pltpu.VMEM((s, hd), BF16),
            pltpu.VMEM((s, 2 * DA_V_DIM), BF16),
            pltpu.VMEM((ATTN_TILE, s), F32),
            pltpu.VMEM((ATTN_TILE, s), F32),
        ],
        compiler_params=pltpu.CompilerParams(
            dimension_semantics=("parallel", "parallel"), vmem_limit_bytes=_vmem_limit(est)),
        name="diff_attn",
    )(lam_vecs, proj3, proj3, proj3, gq, gk, go)


def _mix_kernel(x_ref, pool_ref, halo_ref, mq_ref, oda_ref, mk_ref, mv_ref,
                gmix_ref, wg_ref, bg_ref, wpool_ref, pscale_ref, gmq_ref,
                wda_ref, wbp_ref, wbm_ref, wout_ref, o_ref, *, tiles_per_seq):
    tm = x_ref.shape[0]
    d = D_MODEL
    x = x_ref[...]
    hb = (_rms(x) * gmix_ref[...]).astype(BF16)

    tile_in_seq = pl.program_id(0) % tiles_per_seq
    p_cur = pool_ref[...].astype(F32)
    halo = jnp.where(tile_in_seq == 0, 0.0, halo_ref[...].astype(F32))
    ext = jnp.concatenate([halo, p_cur], axis=0)
    t_pos = tile_in_seq * tm + lax.broadcasted_iota(jnp.int32, (tm, 1), 0)
    pool_parts = []
    for g, w in enumerate(POOL_WINDOWS):
        c0 = g * POOL_GROUP_DIM
        s = ext[:, c0:c0 + POOL_GROUP_DIM]
        sh = 1
        while sh < w:
            s = s + pltpu.roll(s, sh, axis=0)
            sh *= 2
        cnt = jnp.minimum(t_pos + 1, w).astype(F32)
        dlt = (s[POOL_HALO:] / cnt - p_cur[:, c0:c0 + POOL_GROUP_DIM]).astype(BF16)
        pool_parts.append(_dot(dlt, wpool_ref[g]))
    o_pool = (jnp.concatenate(pool_parts, axis=1) * pscale_ref[...]).astype(BF16)

    mq = mq_ref[...].astype(F32)
    mem_parts = []
    for hh in range(MEM_HEADS):
        c0 = hh * MEM_HEAD_DIM
        qh = (_rms(mq[:, c0:c0 + MEM_HEAD_DIM]) * gmq_ref[...] * (MEM_HEAD_DIM ** -0.5)).astype(BF16)
        s = _dot_nt(qh, mk_ref[:, c0:c0 + MEM_HEAD_DIM])
        p = jnp.exp(s - jnp.max(s, axis=-1, keepdims=True))
        l = jnp.sum(p, axis=-1, keepdims=True)
        mem_parts.append(_dot(p.astype(BF16), mv_ref[:, c0:c0 + MEM_HEAD_DIM]) / l)
    o_mem = jnp.concatenate(mem_parts, axis=1).astype(BF16)

    def gate(k):
        return jax.nn.sigmoid(_dot(hb, wg_ref[:, k * d:(k + 1) * d]) + bg_ref[:, k * d:(k + 1) * d])

    merged = gate(0) * _dot(oda_ref[...], wda_ref[...])
    merged = merged + gate(1) * _dot(o_pool, wbp_ref[...])
    merged = merged + gate(2) * _dot(o_mem, wbm_ref[...])
    o_ref[...] = x + _dot(merged.astype(BF16), wout_ref[...])


def _mix(x, proj, o_da, mem_k, mem_v, gmix, wg, bg, wpool, pscale, gmq, wda, wbp, wbm, wout, seq):
    t, d = x.shape
    tm = ROW_TILE
    tiles_per_seq = seq // tm
    halo_blocks_per_tile = tm // POOL_HALO
    pool_col = (proj.shape[1] - POOL_WIDTH - MEM_WIDTH) // POOL_WIDTH
    mem_len = mem_k.shape[1]
    const = lambda i: (0, 0)
    weights_bytes = 2 * (wg.size + wda.size + wbp.size + wbm.size + wout.size + wpool.size)
    est = 2 * weights_bytes + 2 * (2 * tm * d * 4 + tm * (d + 2 * POOL_WIDTH) * 2) + 8 * tm * d * 4
    return pl.pallas_call(
        functools.partial(_mix_kernel, tiles_per_seq=tiles_per_seq),
        grid=(t // tm,),
        in_specs=[
            pl.BlockSpec((tm, d), lambda i: (i, 0)),
            pl.BlockSpec((tm, POOL_WIDTH), lambda i: (i, pool_col)),
            pl.BlockSpec((POOL_HALO, POOL_WIDTH),
                         lambda i: (jnp.maximum(i * halo_blocks_per_tile - 1, 0), pool_col)),
            pl.BlockSpec((tm, MEM_WIDTH), lambda i: (i, pool_col + 1)),
            pl.BlockSpec((tm, d), lambda i: (i, 0)),
            pl.BlockSpec((None, mem_len, MEM_WIDTH), lambda i: (i // tiles_per_seq, 0, 0)),
            pl.BlockSpec((None, mem_len, MEM_WIDTH), lambda i: (i // tiles_per_seq, 0, 0)),
            pl.BlockSpec((1, d), const),
            pl.BlockSpec(wg.shape, const),
            pl.BlockSpec((1, bg.shape[1]), const),
            pl.BlockSpec(wpool.shape, lambda i: (0, 0, 0)),
            pl.BlockSpec((1, POOL_WIDTH), const),
            pl.BlockSpec((1, MEM_HEAD_DIM), const),
            pl.BlockSpec(wda.shape, const),
            pl.BlockSpec(wbp.shape, const),
            pl.BlockSpec(wbm.shape, const),
            pl.BlockSpec(wout.shape, const),
        ],
        out_specs=pl.BlockSpec((tm, d), lambda i: (i, 0)),
        out_shape=jax.ShapeDtypeStruct((t, d), F32),
        compiler_params=pltpu.CompilerParams(
            dimension_semantics=("parallel",), vmem_limit_bytes=_vmem_limit(est)),
        name="mix",
    )(x, proj, proj, proj, o_da, mem_k, mem_v, gmix, wg, bg, wpool, pscale, gmq, wda, wbp, wbm, wout)


def _split3(x):
    hi = x.astype(BF16)
    r = x - hi.astype(F32)
    mid = r.astype(BF16)
    lo = (r - mid.astype(F32)).astype(BF16)
    return hi, mid, lo


def _router_kernel(x_ref, g_ref, wr_ref, comb_ref):
    tm = x_ref.shape[0]
    h2 = _rms(x_ref[...]) * g_ref[...]
    h_hi, h_mid, _ = _split3(h2)
    w_hi, w_mid, _ = _split3(wr_ref[...])
    logits = _dot_nt(w_hi, h_hi) + (_dot_nt(w_hi, h_mid) + _dot_nt(w_mid, h_hi))
    eidx = lax.broadcasted_iota(jnp.int32, logits.shape, 0).astype(F32)
    m1 = jnp.max(logits, axis=0, keepdims=True)
    i1 = jnp.min(jnp.where(logits == m1, eidx, float(N_EXPERTS)), axis=0, keepdims=True)
    sel1 = eidx == i1
    rest = jnp.where(sel1, -jnp.inf, logits)
    m2 = jnp.max(rest, axis=0, keepdims=True)
    i2 = jnp.min(jnp.where(rest == m2, eidx, float(N_EXPERTS)), axis=0, keepdims=True)
    sel2 = eidx == i2
    e2 = jnp.exp(m2 - m1)
    g1 = 1.0 / (1.0 + e2)
    g2 = e2 / (1.0 + e2)
    comb_t = jnp.where(sel1, g1, 0.0) + jnp.where(sel2, g2, 0.0)
    ident = (lax.broadcasted_iota(jnp.int32, (tm, tm), 0)
             == lax.broadcasted_iota(jnp.int32, (tm, tm), 1)).astype(BF16)
    c_hi, c_mid, c_lo = _split3(comb_t)
    comb_ref[...] = _dot_nt(ident, c_hi) + (_dot_nt(ident, c_mid) + _dot_nt(ident, c_lo))


def _router(x, g, w_router_t):
    t, d = x.shape
    tm = ROW_TILE
    return pl.pallas_call(
        _router_kernel,
        grid=(t // tm,),
        in_specs=[
            pl.BlockSpec((tm, d), lambda i: (i, 0)),
            pl.BlockSpec((1, d), lambda i: (0, 0)),
            pl.BlockSpec((N_EXPERTS, d), lambda i: (0, 0)),
        ],
        out_specs=pl.BlockSpec((tm, N_EXPERTS), lambda i: (i, 0)),
        out_shape=jax.ShapeDtypeStruct((t, N_EXPERTS), F32),
        compiler_params=pltpu.CompilerParams(dimension_semantics=("parallel",)),
        name="router",
    )(x, g, w_router_t)


def _ffn_kernel(x_ref, g_ref, comb_ref, w1_ref, w3_ref, w2_ref, o_ref, hb_scr, acc_scr):
    e = pl.program_id(1)
    f = pl.program_id(2)

    @pl.when((e == 0) & (f == 0))
    def _():
        hb_scr[...] = (_rms(x_ref[...]) * g_ref[...]).astype(BF16)
        acc_scr[...] = jnp.zeros_like(acc_scr)

    hb = hb_scr[...]
    a = _dot(hb, w1_ref[...])
    b = _dot(hb, w3_ref[...])
    act = (a * jax.nn.sigmoid(a) * b).astype(BF16)
    comb = comb_ref[...]
    col = lax.broadcasted_iota(jnp.int32, comb.shape, 1)
    gate = jnp.sum(jnp.where(col == e, comb, 0.0), axis=1, keepdims=True)
    acc_scr[...] += gate * _dot(act, w2_ref[...])

    @pl.when((e == pl.num_programs(1) - 1) & (f == pl.num_programs(2) - 1))
    def _():
        o_ref[...] = x_ref[...] + acc_scr[...]


def _ffn(x, g, comb, w1, w3, w2):
    t, d = x.shape
    n_e, _, f_dim = w1.shape
    tm = FFN_ROW_TILE
    tf = FFN_COL_TILE if f_dim % FFN_COL_TILE == 0 else 256
    est = 2 * (2 * tm * d * 4 + 3 * d * tf * 2) + tm * d * 6 + 4 * tm * tf * 4
    return pl.pallas_call(
        _ffn_kernel,
        grid=(t // tm, n_e, f_dim // tf),
        in_specs=[
            pl.BlockSpec((tm, d), lambda i, e, f: (i, 0)),
            pl.BlockSpec((1, d), lambda i, e, f: (0, 0)),
            pl.BlockSpec((tm, n_e), lambda i, e, f: (i, 0)),
            pl.BlockSpec((None, d, tf), lambda i, e, f: (e, 0, f)),
            pl.BlockSpec((None, d, tf), lambda i, e, f: (e, 0, f)),
            pl.BlockSpec((None, tf, d), lambda i, e, f: (e, f, 0)),
        ],
        out_specs=pl.BlockSpec((tm, d), lambda i, e, f: (i, 0)),
        out_shape=jax.ShapeDtypeStruct((t, d), F32),
        scratch_shapes=[pltpu.VMEM((tm, d), BF16), pltpu.VMEM((tm, d), F32)],
        compiler_params=pltpu.CompilerParams(
            dimension_semantics=("parallel", "arbitrary", "arbitrary"),
            vmem_limit_bytes=_vmem_limit(est)),
        name="ffn",
    )(x, g, comb, w1, w3, w2)


def _lambda_init(layer):
    return 0.8 - 0.6 * math.exp(-0.3 * layer)


def kernel(x, mem, norm_mix, norm_mem, norm_ffn, w_in, w_gate, b_gate, da_q_norm, da_k_norm,
           da_lam_q1, da_lam_k1, da_lam_q2, da_lam_k2, da_out_norm, pool_w, pool_scale,
           mem_q_norm, mem_k_norm, w_mem_kv, w_br_da, w_br_pool, w_br_mem, w_out,
           ffn_w1, ffn_w3, ffn_w2, moe_router, moe_w1, moe_w3, moe_w2):
    b, s, d = x.shape
    depth = w_in.shape[0]
    xt = x.reshape(b * s, d)
    bf = lambda a: a.astype(BF16)
    row = lambda v: v.reshape(1, -1)
    for l in range(depth):
        lam0 = _lambda_init(l)
        proj = _in_proj(xt, row(norm_mix[l]), bf(w_in[l]))
        mem_k, mem_v = _mem_kv(mem, row(norm_mem[l]), bf(w_mem_kv[l]), row(mem_k_norm[l]))
        lam_vecs = jnp.stack([da_lam_q1[l], da_lam_k1[l], da_lam_q2[l], da_lam_k2[l]])
        o_da = _diff_attn(proj.reshape(b, s, -1), lam_vecs, row(da_q_norm[l]), row(da_k_norm[l]),
                          row(da_out_norm[l]), lam0)
        xt = _mix(xt, proj, o_da.reshape(b * s, -1), mem_k, mem_v, row(norm_mix[l]), bf(w_gate[l]),
                  row(b_gate[l]), bf(pool_w[l]), row(pool_scale[l]), row(mem_q_norm[l]),
                  bf(w_br_da[l]), bf(w_br_pool[l]), bf(w_br_mem[l]), bf(w_out[l]), s)
        j = l // 2
        if l % 2 == 0:
            ones = jnp.ones((b * s, 1), F32)
            xt = _ffn(xt, row(norm_ffn[l]), ones, bf(ffn_w1[j])[None], bf(ffn_w3[j])[None],
                      bf(ffn_w2[j])[None])
        else:
            comb = _router(xt, row(norm_ffn[l]), moe_router[j].T)
            xt = _ffn(xt, row(norm_ffn[l]), comb, bf(moe_w1[j]), bf(moe_w3[j]), bf(moe_w2[j]))
    return xt.reshape(b, s, d)
```

```python
import functools
import math

import jax
import jax.numpy as jnp
from jax import lax
from jax.experimental import pallas as pl
from jax.experimental.pallas import tpu as pltpu

F32 = jnp.float32
BF16 = jnp.bfloat16

D_MODEL = 1024
CHUNK = 64
RMS_EPS = 1e-6
DA_HEADS = 8
DA_QK_DIM = 64
DA_V_DIM = 128
POOL_WINDOWS = (2, 4, 8, 16)
POOL_GROUP_DIM = 128
POOL_WIDTH = 512
MEM_HEADS = 4
MEM_HEAD_DIM = 128
MEM_WIDTH = 512
N_EXPERTS = 8
LOG2E = 1.4426950408889634
MASK_VALUE = -1e30

V7X_VMEM_BYTES = 64 * 1024 * 1024
VMEM_LIMIT_CAP = 56 * 1024 * 1024
LANES = 128

ROW_TILE = 512
ATTN_TILE = 256
POOL_HALO = 16
FFN_ROW_TILE = 1024
FFN_COL_TILE = 512


def _vmem_limit(estimate_bytes):
    return int(min(VMEM_LIMIT_CAP, max(32 * 1024 * 1024, 2 * estimate_bytes)))


def _rms(x):
    return x * lax.rsqrt(jnp.mean(x * x, axis=-1, keepdims=True) + RMS_EPS)


def _dot(a, b):
    return jnp.dot(a, b, preferred_element_type=F32)


def _dot_nt(a, b):
    return lax.dot_general(a, b, (((1,), (1,)), ((), ())), preferred_element_type=F32)


def _in_proj_kernel(x_ref, g_ref, w_ref, o_ref, *, col_tile):
    hb = (_rms(x_ref[...]) * g_ref[...]).astype(BF16)
    n = w_ref.shape[1]
    for c0 in range(0, n, col_tile):
        o_ref[:, c0:c0 + col_tile] = _dot(hb, w_ref[:, c0:c0 + col_tile]).astype(BF16)


def _in_proj(x, g, w_bf16):
    t, d = x.shape
    n = w_bf16.shape[1]
    tm = ROW_TILE
    est = 2 * (tm * d * 4 + d * n * 2 + tm * n * 2) + tm * 1024 * 4
    return pl.pallas_call(
        functools.partial(_in_proj_kernel, col_tile=1024),
        grid=(t // tm,),
        in_specs=[
            pl.BlockSpec((tm, d), lambda i: (i, 0)),
            pl.BlockSpec((1, d), lambda i: (0, 0)),
            pl.BlockSpec((d, n), lambda i: (0, 0)),
        ],
        out_specs=pl.BlockSpec((tm, n), lambda i: (i, 0)),
        out_shape=jax.ShapeDtypeStruct((t, n), BF16),
        compiler_params=pltpu.CompilerParams(
            dimension_semantics=("parallel",), vmem_limit_bytes=_vmem_limit(est)),
        name="in_proj",
    )(x, g, w_bf16)


def _mem_kv_kernel(mem_ref, g_ref, w_ref, gk_ref, k_ref, v_ref):
    mh = (_rms(mem_ref[...]) * g_ref[...]).astype(BF16)
    kv = _dot(mh, w_ref[...])
    for hh in range(MEM_HEADS):
        c0 = hh * MEM_HEAD_DIM
        k_ref[:, c0:c0 + MEM_HEAD_DIM] = (_rms(kv[:, c0:c0 + MEM_HEAD_DIM]) * gk_ref[...]).astype(BF16)
    v_ref[...] = kv[:, MEM_WIDTH:].astype(BF16)


def _mem_kv(mem, g, w_bf16, gk):
    b, m, d = mem.shape
    out = jax.ShapeDtypeStruct((b, m, MEM_WIDTH), BF16)
    return pl.pallas_call(
        _mem_kv_kernel,
        grid=(b,),
        in_specs=[
            pl.BlockSpec((None, m, d), lambda i: (i, 0, 0)),
            pl.BlockSpec((1, d), lambda i: (0, 0)),
            pl.BlockSpec((d, 2 * MEM_WIDTH), lambda i: (0, 0)),
            pl.BlockSpec((1, MEM_HEAD_DIM), lambda i: (0, 0)),
        ],
        out_specs=[pl.BlockSpec((None, m, MEM_WIDTH), lambda i: (i, 0, 0))] * 2,
        out_shape=[out, out],
        compiler_params=pltpu.CompilerParams(dimension_semantics=("parallel",)),
        name="mem_kv",
    )(mem, g, w_bf16, gk)


def _diff_attn_kernel(lam_ref, q_ref, k_ref, v_ref, gq_ref, gk_ref, go_ref, o_ref,
                      qn_scr, kz0_scr, kz1_scr, vext_scr, s0_scr, s1_scr, *, lam_init, seq):
    tq = ATTN_TILE
    head = pl.program_id(1)
    slope = jnp.exp2(-(jnp.zeros((1, LANES), F32) + (head + 1).astype(F32)))[:, :1] * LOG2E

    lam_v = lam_ref[...]
    lam = (jnp.exp(jnp.sum(lam_v[0:1] * lam_v[1:2], axis=-1, keepdims=True))
           - jnp.exp(jnp.sum(lam_v[2:3] * lam_v[3:4], axis=-1, keepdims=True)) + lam_init)

    r_i = lax.broadcasted_iota(jnp.int32, (LANES, LANES), 0) // DA_QK_DIM
    c_i = lax.broadcasted_iota(jnp.int32, (LANES, LANES), 1) // DA_QK_DIM
    group_mean = jnp.where(r_i == c_i, 1.0 / DA_QK_DIM, 0.0).astype(BF16)

    def qk_norm(x_bf16, gain):
        xf = x_bf16.astype(F32)
        sq = xf * xf
        hi = sq.astype(BF16)
        lo = (sq - hi.astype(F32)).astype(BF16)
        ms = _dot(hi, group_mean) + _dot(lo, group_mean)
        return xf * lax.rsqrt(ms + RMS_EPS) * gain

    qn_scr[...] = (qk_norm(q_ref[...], gq_ref[...]) * (DA_QK_DIM ** -0.5 * LOG2E)).astype(BF16)
    kn = qk_norm(k_ref[...], gk_ref[...])
    lane = lax.broadcasted_iota(jnp.int32, (seq, LANES), 1)
    kz0_scr[...] = jnp.where(lane < DA_QK_DIM, kn, 0.0).astype(BF16)
    kz1_scr[...] = jnp.where(lane >= DA_QK_DIM, kn, 0.0).astype(BF16)
    vext_scr[:, :DA_V_DIM] = v_ref[...]
    vext_scr[:, DA_V_DIM:] = jnp.ones((seq, DA_V_DIM), BF16)

    qrel = lax.broadcasted_iota(jnp.int32, (tq, tq), 0)
    krel = lax.broadcasted_iota(jnp.int32, (tq, tq), 1)
    allowed = (krel // CHUNK) <= (qrel // CHUNK)
    diag_bias = slope * (qrel - jnp.abs(qrel - krel)).astype(F32)
    kcol = lax.broadcasted_iota(jnp.int32, (1, tq), 1).astype(F32)

    for i in range(seq // tq):
        q_i = qn_scr[i * tq:(i + 1) * tq, :]
        mx = [jnp.full((tq, LANES), -jnp.inf, F32), jnp.full((tq, LANES), -jnp.inf, F32)]
        for j in range(i + 1):
            ks = slice(j * tq, (j + 1) * tq)
            for m, (kz, s_scr) in enumerate(((kz0_scr, s0_scr), (kz1_scr, s1_scr))):
                s = _dot_nt(q_i, kz[ks, :])
                if j < i:
                    s = s + slope * (kcol + float((j - i) * tq))
                else:
                    s = jnp.where(allowed, s + diag_bias, MASK_VALUE)
                s_scr[:, ks] = s
                for c0 in range(0, tq, LANES):
                    mx[m] = jnp.maximum(mx[m], s[:, c0:c0 + LANES])
        row_max = [jnp.max(mx[0], axis=-1, keepdims=True), jnp.max(mx[1], axis=-1, keepdims=True)]
        acc = [jnp.zeros((tq, 2 * DA_V_DIM), F32), jnp.zeros((tq, 2 * DA_V_DIM), F32)]
        for j in range(i + 1):
            ks = slice(j * tq, (j + 1) * tq)
            for m, s_scr in enumerate((s0_scr, s1_scr)):
                p = jnp.exp2(s_scr[:, ks] - row_max[m]).astype(BF16)
                acc[m] = acc[m] + _dot(p, vext_scr[ks, :])
        o = (acc[0][:, :DA_V_DIM] / acc[0][:, DA_V_DIM:]
             - lam * (acc[1][:, :DA_V_DIM] / acc[1][:, DA_V_DIM:]))
        o = _rms(o) * go_ref[...] * (1.0 - lam_init)
        o_ref[i * tq:(i + 1) * tq, :] = o.astype(BF16)


def _diff_attn(proj3, lam_vecs, gq, gk, go, lam_init):
    b, s, _ = proj3.shape
    hd = 2 * DA_QK_DIM
    kern = functools.partial(_diff_attn_kernel, lam_init=lam_init, seq=s)
    est = (2 * 4 * s * hd * 2 + 3 * s * hd * 2 + s * 2 * hd * 2 + 2 * ATTN_TILE * s * 4
           + 8 * s * hd * 4)
    return pl.pallas_call(
        kern,
        grid=(b, DA_HEADS),
        in_specs=[
            pl.BlockSpec((4, DA_QK_DIM), lambda bi, h: (0, 0)),
            pl.BlockSpec((None, s, hd), lambda bi, h: (bi, 0, h)),
            pl.BlockSpec((None, s, hd), lambda bi, h: (bi, 0, DA_HEADS + h)),
            pl.BlockSpec((None, s, DA_V_DIM), lambda bi, h: (bi, 0, 2 * DA_HEADS + h)),
            pl.BlockSpec((1, hd), lambda bi, h: (0, 0)),
            pl.BlockSpec((1, hd), lambda bi, h: (0, 0)),
            pl.BlockSpec((1, DA_V_DIM), lambda bi, h: (0, 0)),
        ],
        out_specs=pl.BlockSpec((None, s, DA_V_DIM), lambda bi, h: (bi, 0, h)),
        out_shape=jax.ShapeDtypeStruct((b, s, DA_HEADS * DA_V_DIM), BF16),
        scratch_shapes=[
            pltpu.VMEM((s, hd), BF16),
            pltpu.VMEM((s, hd), BF16),
            pltpu.VMEM((s, hd), BF16),
            pltpu.VMEM((s, 2 * DA_V_DIM), BF16),
            pltpu.VMEM((ATTN_TILE, s), F32),
            pltpu.VMEM((ATTN_TILE, s), F32),
        ],
        compiler_params=pltpu.CompilerParams(
            dimension_semantics=("parallel", "parallel"), vmem_limit_bytes=_vmem_limit(est)),
        name="diff_attn",
    )(lam_vecs, proj3, proj3, proj3, gq, gk, go)


def _mix_kernel(x_ref, pool_ref, halo_ref, mq_ref, oda_ref, mk_ref, mv_ref,
                gmix_ref, wg_ref, bg_ref, wpool_ref, pscale_ref, gmq_ref,
                wda_ref, wbp_ref, wbm_ref, wout_ref, o_ref, *, tiles_per_seq):
    tm = x_ref.shape[0]
    d = D_MODEL
    x = x_ref[...]
    hb = (_rms(x) * gmix_ref[...]).astype(BF16)

    tile_in_seq = pl.program_id(0) % tiles_per_seq
    p_cur = pool_ref[...].astype(F32)
    halo = jnp.where(tile_in_seq == 0, 0.0, halo_ref[...].astype(F32))
    ext = jnp.concatenate([halo, p_cur], axis=0)
    t_pos = tile_in_seq * tm + lax.broadcasted_iota(jnp.int32, (tm, 1), 0)
    pool_parts = []
    for g, w in enumerate(POOL_WINDOWS):
        c0 = g * POOL_GROUP_DIM
        s = ext[:, c0:c0 + POOL_GROUP_DIM]
        sh = 1
        while sh < w:
            s = s + pltpu.roll(s, sh, axis=0)
            sh *= 2
        cnt = jnp.minimum(t_pos + 1, w).astype(F32)
        dlt = (s[POOL_HALO:] / cnt - p_cur[:, c0:c0 + POOL_GROUP_DIM]).astype(BF16)
        pool_parts.append(_dot(dlt, wpool_ref[g]))
    o_pool = (jnp.concatenate(pool_parts, axis=1) * pscale_ref[...]).astype(BF16)

    mq = mq_ref[...].astype(F32)
    mem_parts = []
    for hh in range(MEM_HEADS):
        c0 = hh * MEM_HEAD_DIM
        qh = (_rms(mq[:, c0:c0 + MEM_HEAD_DIM]) * gmq_ref[...] * (MEM_HEAD_DIM ** -0.5)).astype(BF16)
        s = _dot_nt(qh, mk_ref[:, c0:c0 + MEM_HEAD_DIM])
        p = jnp.exp(s - jnp.max(s, axis=-1, keepdims=True))
        l = jnp.sum(p, axis=-1, keepdims=True)
        mem_parts.append(_dot(p.astype(BF16), mv_ref[:, c0:c0 + MEM_HEAD_DIM]) / l)
    o_mem = jnp.concatenate(mem_parts, axis=1).astype(BF16)

    def gate(k):
        return jax.nn.sigmoid(_dot(hb, wg_ref[:, k * d:(k + 1) * d]) + bg_ref[:, k * d:(k + 1) * d])

    merged = gate(0) * _dot(oda_ref[...], wda_ref[...])
    merged = merged + gate(1) * _dot(o_pool, wbp_ref[...])
    merged = merged + gate(2) * _dot(o_mem, wbm_ref[...])
    o_ref[...] = x + _dot(merged.astype(BF16), wout_ref[...])


def _mix(x, proj, o_da, mem_k, mem_v, gmix, wg, bg, wpool, pscale, gmq, wda, wbp, wbm, wout, seq):
    t, d = x.shape
    tm = ROW_TILE
    tiles_per_seq = seq // tm
    halo_blocks_per_tile = tm // POOL_HALO
    pool_col = (proj.shape[1] - POOL_WIDTH - MEM_WIDTH) // POOL_WIDTH
    mem_len = mem_k.shape[1]
    const = lambda i: (0, 0)
    weights_bytes = 2 * (wg.size + wda.size + wbp.size + wbm.size + wout.size + wpool.size)
    est = 2 * weights_bytes + 2 * (2 * tm * d * 4 + tm * (d + 2 * POOL_WIDTH) * 2) + 8 * tm * d * 4
    return pl.pallas_call(
        functools.partial(_mix_kernel, tiles_per_seq=tiles_per_seq),
        grid=(t // tm,),
        in_specs=[
            pl.BlockSpec((tm, d), lambda i: (i, 0)),
            pl.BlockSpec((tm, POOL_WIDTH), lambda i: (i, pool_col)),
            pl.BlockSpec((POOL_HALO, POOL_WIDTH),
                         lambda i: (jnp.maximum(i * halo_blocks_per_tile - 1, 0), pool_col)),
            pl.BlockSpec((tm, MEM_WIDTH), lambda i: (i, pool_col + 1)),
            pl.BlockSpec((tm, d), lambda i: (i, 0)),
            pl.BlockSpec((None, mem_len, MEM_WIDTH), lambda i: (i // tiles_per_seq, 0, 0)),
            pl.BlockSpec((None, mem_len, MEM_WIDTH), lambda i: (i // tiles_per_seq, 0, 0)),
            pl.BlockSpec((1, d), const),
            pl.BlockSpec(wg.shape, const),
            pl.BlockSpec((1, bg.shape[1]), const),
            pl.BlockSpec(wpool.shape, lambda i: (0, 0, 0)),
            pl.BlockSpec((1, POOL_WIDTH), const),
            pl.BlockSpec((1, MEM_HEAD_DIM), const),
            pl.BlockSpec(wda.shape, const),
            pl.BlockSpec(wbp.shape, const),
            pl.BlockSpec(wbm.shape, const),
            pl.BlockSpec(wout.shape, const),
        ],
        out_specs=pl.BlockSpec((tm, d), lambda i: (i, 0)),
        out_shape=jax.ShapeDtypeStruct((t, d), F32),
        compiler_params=pltpu.CompilerParams(
            dimension_semantics=("parallel",), vmem_limit_bytes=_vmem_limit(est)),
        name="mix",
    )(x, proj, proj, proj, o_da, mem_k, mem_v, gmix, wg, bg, wpool, pscale, gmq, wda, wbp, wbm, wout)


def _split3(x):
    hi = x.astype(BF16)
    r = x - hi.astype(F32)
    mid = r.astype(BF16)
    lo = (r - mid.astype(F32)).astype(BF16)
    return hi, mid, lo


MOE_TILE = 512
SEG_MAIN = 192
SEG_OVER = MOE_TILE - SEG_MAIN
SEG_ALIGN = 16
NOT_ROUTED = -1e6
UP_COL_TILE = 896
DOWN_COL_TILE = 512
COPY_SIZES = (512, 256, 128, 64, 32, 16)


def _max_expert_tiles(t):
    padded_rows = 2 * t + (t // MOE_TILE) * N_EXPERTS * (SEG_ALIGN - 1)
    return padded_rows // MOE_TILE + N_EXPERTS


def _row_copies(src, dst, src_row0, dst_row0, n_rows, max_rows, sem):
    pairs = []
    pos = 0
    for size in COPY_SIZES:
        if size > max_rows:
            continue
        cond = jnp.bitwise_and(n_rows, size) != 0
        cp = pltpu.make_async_copy(
            src.at[pl.ds(pl.multiple_of(src_row0 + pos, SEG_ALIGN), size), :],
            dst.at[pl.ds(pl.multiple_of(dst_row0 + pos, SEG_ALIGN), size), :], sem)
        pairs.append((cond, cp))
        pos = pos + jnp.where(cond, size, 0)
    return pairs


def _start_all(pairs):
    for cond, cp in pairs:
        pl.when(cond)(cp.start)


def _wait_all(pairs):
    for cond, cp in pairs:
        pl.when(cond)(cp.wait)


def _onehot_rows(rank_row, n_rows, row0):
    r = lax.broadcasted_iota(jnp.int32, (n_rows, rank_row.shape[1]), 0).astype(F32) + float(row0)
    return jnp.where(r == rank_row, 1.0, 0.0).astype(BF16)


def _onehot_cols(rank_col, n_cols):
    c = lax.broadcasted_iota(jnp.int32, (rank_col.shape[0], n_cols), 1).astype(F32)
    return jnp.where(c == rank_col, 1.0, 0.0).astype(BF16)


def _route_kernel(x_ref, g_ref, wr_ref, rank_ref, meta_ref, cnts_ref):
    i = pl.program_id(0)
    tm = x_ref.shape[0]
    h2 = _rms(x_ref[...]) * g_ref[...]
    h_hi, h_mid, _ = _split3(h2)
    w_hi, w_mid, _ = _split3(wr_ref[...])
    logits = _dot_nt(w_hi, h_hi) + (_dot_nt(w_hi, h_mid) + _dot_nt(w_mid, h_hi))
    eidx = lax.broadcasted_iota(jnp.int32, logits.shape, 0).astype(F32)
    m1 = jnp.max(logits, axis=0, keepdims=True)
    i1 = jnp.min(jnp.where(logits == m1, eidx, float(N_EXPERTS)), axis=0, keepdims=True)
    sel1 = eidx == i1
    rest = jnp.where(sel1, -jnp.inf, logits)
    m2 = jnp.max(rest, axis=0, keepdims=True)
    i2 = jnp.min(jnp.where(rest == m2, eidx, float(N_EXPERTS)), axis=0, keepdims=True)
    sel2 = eidx == i2
    e2 = jnp.exp(m2 - m1)
    g1 = 1.0 / (1.0 + e2)
    g2 = e2 / (1.0 + e2)

    mask = jnp.where(sel1 | sel2, 1.0, 0.0)
    before = (lax.broadcasted_iota(jnp.int32, (tm, tm), 0)
              < lax.broadcasted_iota(jnp.int32, (tm, tm), 1)).astype(BF16)
    rank = _dot(mask.astype(BF16), before)
    counts = jnp.sum(mask, axis=1, keepdims=True)
    rank_m = jnp.where(mask > 0.5, rank, NOT_ROUTED)

    rk1 = jnp.where(sel1, rank, NOT_ROUTED)
    rk2 = jnp.where(sel2, rank, NOT_ROUTED)
    gates = jnp.where(eidx == 0.0, g1, jnp.where(eidx == 1.0, g2, 0.0))
    packed = jnp.concatenate([rk1, rk2, gates], axis=0)
    ident = (lax.broadcasted_iota(jnp.int32, (tm, tm), 0)
             == lax.broadcasted_iota(jnp.int32, (tm, tm), 1)).astype(BF16)
    p_hi, p_mid, p_lo = _split3(packed)
    meta_ref[...] = _dot_nt(ident, p_hi) + (_dot_nt(ident, p_mid) + _dot_nt(ident, p_lo))

    rank_ref[...] = rank_m
    for e in range(N_EXPERTS):
        cnt = counts[e, 0].astype(jnp.int32)
        cnts_ref[i * N_EXPERTS + e] = jnp.bitwise_and(cnt + (SEG_ALIGN - 1), -SEG_ALIGN)


def _route(x, g, w_router_t):
    t, d = x.shape
    tm = MOE_TILE
    nt = t // tm
    est = 2 * tm * d * 4 + 6 * tm * tm * 4 + 4 * tm * d * 4
    return pl.pallas_call(
        _route_kernel,
        grid=(nt,),
        in_specs=[
            pl.BlockSpec((tm, d), lambda i: (i, 0)),
            pl.BlockSpec((1, d), lambda i: (0, 0)),
            pl.BlockSpec((N_EXPERTS, d), lambda i: (0, 0)),
        ],
        out_specs=[
            pl.BlockSpec((N_EXPERTS, tm), lambda i: (0, i)),
            pl.BlockSpec((tm, 3 * N_EXPERTS), lambda i: (i, 0)),
            pl.BlockSpec(memory_space=pltpu.SMEM),
        ],
        out_shape=[
            jax.ShapeDtypeStruct((N_EXPERTS, t), F32),
            jax.ShapeDtypeStruct((t, 3 * N_EXPERTS), F32),
            jax.ShapeDtypeStruct((nt * N_EXPERTS,), jnp.int32),
        ],
        compiler_params=pltpu.CompilerParams(
            dimension_semantics=("arbitrary",), vmem_limit_bytes=_vmem_limit(est)),
        name="route",
    )(x, g, w_router_t)


def _scatter_kernel(offs_ref, cnts_ref, fill_ref, nv_ref, x_ref, g_ref, rank_ref, xs_hbm,
                    stage, over_stage, zero_buf, sems, over_sem, *, n_tiles):
    i = pl.program_id(0)
    hb = (_rms(x_ref[...]) * g_ref[...]).astype(BF16)
    rank_m = rank_ref[...]

    def main_copies(step, e):
        n_main = jnp.minimum(cnts_ref[step * N_EXPERTS + e], SEG_MAIN)
        return _row_copies(stage.at[e], xs_hbm, 0, offs_ref[step * N_EXPERTS + e], n_main, SEG_MAIN,
                           sems.at[e])

    @pl.when(i > 0)
    def _():
        for e in range(N_EXPERTS):
            _wait_all(main_copies(i - 1, e))

    for e in range(N_EXPERTS):
        rank_e = rank_m[e:e + 1, :]
        stage[e] = _dot(_onehot_rows(rank_e, SEG_MAIN, 0), hb).astype(BF16)
        _start_all(main_copies(i, e))
        n_over = cnts_ref[i * N_EXPERTS + e] - SEG_MAIN

        @pl.when(n_over > 0)
        def _():
            over_stage[...] = _dot(_onehot_rows(rank_e, SEG_OVER, SEG_MAIN), hb).astype(BF16)
            pairs = _row_copies(over_stage, xs_hbm, 0, offs_ref[i * N_EXPERTS + e] + SEG_MAIN, n_over,
                                SEG_OVER, over_sem)
            _start_all(pairs)
            _wait_all(pairs)

    @pl.when(i == pl.num_programs(0) - 1)
    def _():
        for e in range(N_EXPERTS):
            _wait_all(main_copies(i, e))
        zero_buf[...] = jnp.zeros_like(zero_buf)
        for e in range(N_EXPERTS):
            pairs = _row_copies(zero_buf, xs_hbm, 0, fill_ref[e], fill_ref[N_EXPERTS + e], MOE_TILE,
                                sems.at[e])
            _start_all(pairs)
            _wait_all(pairs)

        def zero_tile(tile, carry):
            dst = xs_hbm.at[pl.ds(pl.multiple_of(tile * MOE_TILE, MOE_TILE), MOE_TILE), :]
            cp = pltpu.make_async_copy(zero_buf, dst, over_sem)
            cp.start()
            cp.wait()
            return carry

        lax.fori_loop(nv_ref[0], n_tiles, zero_tile, 0)


def _scatter(x, g, rank, offs, cnts, fill, n_valid, n_tiles):
    t, d = x.shape
    tm = MOE_TILE
    est = 2 * tm * d * 4 + (N_EXPERTS * SEG_MAIN + SEG_OVER + MOE_TILE) * d * 2 + 6 * tm * d * 4
    grid_spec = pltpu.PrefetchScalarGridSpec(
        num_scalar_prefetch=4,
        grid=(t // tm,),
        in_specs=[
            pl.BlockSpec((tm, d), lambda i, *_: (i, 0)),
            pl.BlockSpec((1, d), lambda i, *_: (0, 0)),
            pl.BlockSpec((N_EXPERTS, tm), lambda i, *_: (0, i)),
        ],
        out_specs=pl.BlockSpec(memory_space=pl.ANY),
        scratch_shapes=[
            pltpu.VMEM((N_EXPERTS, SEG_MAIN, d), BF16),
            pltpu.VMEM((SEG_OVER, d), BF16),
            pltpu.VMEM((MOE_TILE, d), BF16),
            pltpu.SemaphoreType.DMA((N_EXPERTS,)),
            pltpu.SemaphoreType.DMA(()),
        ],
    )
    return pl.pallas_call(
        functools.partial(_scatter_kernel, n_tiles=n_tiles),
        grid_spec=grid_spec,
        out_shape=jax.ShapeDtypeStruct((n_tiles * MOE_TILE, d), BF16),
        compiler_params=pltpu.CompilerParams(
            dimension_semantics=("arbitrary",), vmem_limit_bytes=_vmem_limit(est)),
        name="scatter",
    )(offs, cnts, fill, n_valid, x, g, rank)


def _expert_changed(te_ref, t):
    return (t == 0) | (te_ref[t] != te_ref[jnp.maximum(t - 1, 0)])


def _expert_up_kernel(te_ref, nv_ref, x_ref, w1_ref, w3_ref, h_ref, w1b, w3b):
    t = pl.program_id(1)

    @pl.when(_expert_changed(te_ref, t))
    def _():
        w1b[...] = w1_ref[...].astype(BF16)
        w3b[...] = w3_ref[...].astype(BF16)

    @pl.when(t < nv_ref[0])
    def _():
        x = x_ref[...]
        a = _dot(x, w1b[...])
        b = _dot(x, w3b[...])
        h_ref[...] = (a * jax.nn.sigmoid(a) * b).astype(BF16)

    @pl.when(t >= nv_ref[0])
    def _():
        h_ref[...] = jnp.zeros_like(h_ref)


def _expert_up(xs, w1, w3, tile_expert, n_valid):
    rows, d = xs.shape
    _, _, f_dim = w1.shape
    tf = UP_COL_TILE
    est = 2 * (MOE_TILE * d * 2 + 2 * d * tf * 4 + MOE_TILE * tf * 2) + 2 * d * tf * 2 + 3 * MOE_TILE * tf * 4
    grid_spec = pltpu.PrefetchScalarGridSpec(
        num_scalar_prefetch=2,
        grid=(f_dim // tf, rows // MOE_TILE),
        in_specs=[
            pl.BlockSpec((MOE_TILE, d), lambda f, t, te, nv: (t, 0)),
            pl.BlockSpec((None, d, tf), lambda f, t, te, nv: (te[t], 0, f)),
            pl.BlockSpec((None, d, tf), lambda f, t, te, nv: (te[t], 0, f)),
        ],
        out_specs=pl.BlockSpec((MOE_TILE, tf), lambda f, t, te, nv: (t, f)),
        scratch_shapes=[pltpu.VMEM((d, tf), BF16), pltpu.VMEM((d, tf), BF16)],
    )
    return pl.pallas_call(
        _expert_up_kernel,
        grid_spec=grid_spec,
        out_shape=jax.ShapeDtypeStruct((rows, f_dim), BF16),
        compiler_params=pltpu.CompilerParams(
            dimension_semantics=("arbitrary", "arbitrary"), vmem_limit_bytes=_vmem_limit(est)),
        name="expert_up",
    )(tile_expert, n_valid, xs, w1, w3)


def _expert_down_kernel(te_ref, nv_ref, h_ref, w2_ref, y_ref, w2b):
    t = pl.program_id(1)

    @pl.when(_expert_changed(te_ref, t))
    def _():
        w2b[...] = w2_ref[...].astype(BF16)

    @pl.when(t < nv_ref[0])
    def _():
        y_ref[...] = _dot(h_ref[...], w2b[...]).astype(BF16)

    @pl.when(t >= nv_ref[0])
    def _():
        y_ref[...] = jnp.zeros_like(y_ref)


def _expert_down(hs, w2, tile_expert, n_valid):
    rows, f_dim = hs.shape
    d = w2.shape[2]
    tn = DOWN_COL_TILE
    est = 2 * (MOE_TILE * f_dim * 2 + f_dim * tn * 4 + MOE_TILE * tn * 2) + f_dim * tn * 2 + MOE_TILE * tn * 4
    grid_spec = pltpu.PrefetchScalarGridSpec(
        num_scalar_prefetch=2,
        grid=(d // tn, rows // MOE_TILE),
        in_specs=[
            pl.BlockSpec((MOE_TILE, f_dim), lambda n, t, te, nv: (t, 0)),
            pl.BlockSpec((None, f_dim, tn), lambda n, t, te, nv: (te[t], 0, n)),
        ],
        out_specs=pl.BlockSpec((MOE_TILE, tn), lambda n, t, te, nv: (t, n)),
        scratch_shapes=[pltpu.VMEM((f_dim, tn), BF16)],
    )
    return pl.pallas_call(
        _expert_down_kernel,
        grid_spec=grid_spec,
        out_shape=jax.ShapeDtypeStruct((rows, d), BF16),
        compiler_params=pltpu.CompilerParams(
            dimension_semantics=("arbitrary", "arbitrary"), vmem_limit_bytes=_vmem_limit(est)),
        name="expert_down",
    )(tile_expert, n_valid, hs, w2)


def _combine_kernel(offs_ref, cnts_ref, x_ref, meta_ref, y_hbm, o_ref,
                    ybuf, over_buf, over1, over2, sems, over_sem):
    i = pl.program_id(0)
    tm = x_ref.shape[0]
    total_rows = y_hbm.shape[0]
    meta = meta_ref[...]
    g1 = meta[:, 2 * N_EXPERTS:2 * N_EXPERTS + 1]
    g2 = meta[:, 2 * N_EXPERTS + 1:2 * N_EXPERTS + 2]

    def window(e, seg_row0, n_rows):
        off = offs_ref[i * N_EXPERTS + e] + seg_row0
        start = jnp.minimum(off, total_rows - n_rows)
        return start, off - start

    copies = []
    for e in range(N_EXPERTS):
        start, _ = window(e, 0, SEG_MAIN)
        src = y_hbm.at[pl.ds(pl.multiple_of(start, SEG_ALIGN), SEG_MAIN), :]
        copies.append(pltpu.make_async_copy(src, ybuf.at[e], sems.at[e]))
        copies[-1].start()

    out1 = jnp.zeros((tm, D_MODEL), F32)
    out2 = jnp.zeros((tm, D_MODEL), F32)
    any_over = cnts_ref[i * N_EXPERTS] > SEG_MAIN
    for e in range(N_EXPERTS):
        _, shift = window(e, 0, SEG_MAIN)
        sh = shift.astype(F32)
        sel_a = _onehot_cols(meta[:, e:e + 1] + sh, SEG_MAIN)
        sel_b = _onehot_cols(meta[:, N_EXPERTS + e:N_EXPERTS + e + 1] + sh, SEG_MAIN)
        copies[e].wait()
        out1 = out1 + _dot(sel_a, ybuf[e])
        out2 = out2 + _dot(sel_b, ybuf[e])
        if e > 0:
            any_over = any_over | (cnts_ref[i * N_EXPERTS + e] > SEG_MAIN)
    o_ref[...] = x_ref[...] + (g1 * out1 + g2 * out2)

    @pl.when(any_over)
    def _():
        over1[...] = jnp.zeros_like(over1)
        over2[...] = jnp.zeros_like(over2)
        for e in range(N_EXPERTS):
            @pl.when(cnts_ref[i * N_EXPERTS + e] > SEG_MAIN)
            def _():
                start, shift = window(e, SEG_MAIN, SEG_OVER)
                src = y_hbm.at[pl.ds(pl.multiple_of(start, SEG_ALIGN), SEG_OVER), :]
                cp = pltpu.make_async_copy(src, over_buf, over_sem)
                cp.start()
                cp.wait()
                sh = shift.astype(F32) - float(SEG_MAIN)
                yo = over_buf[...]
                over1[...] += _dot(_onehot_cols(meta[:, e:e + 1] + sh, SEG_OVER), yo)
                over2[...] += _dot(_onehot_cols(meta[:, N_EXPERTS + e:N_EXPERTS + e + 1] + sh, SEG_OVER), yo)
        o_ref[...] += g1 * over1[...] + g2 * over2[...]


def _combine(x, meta, ys, offs, cnts):
    t, d = x.shape
    tm = MOE_TILE
    est = (4 * tm * d * 4 + (N_EXPERTS * SEG_MAIN + SEG_OVER) * d * 2 + 2 * tm * d * 4
           + 2 * tm * LANES * 4 + 6 * tm * d * 4)
    grid_spec = pltpu.PrefetchScalarGridSpec(
        num_scalar_prefetch=2,
        grid=(t // tm,),
        in_specs=[
            pl.BlockSpec((tm, d), lambda i, *_: (i, 0)),
            pl.BlockSpec((tm, 3 * N_EXPERTS), lambda i, *_: (i, 0)),
            pl.BlockSpec(memory_space=pl.ANY),
        ],
        out_specs=pl.BlockSpec((tm, d), lambda i, *_: (i, 0)),
        scratch_shapes=[
            pltpu.VMEM((N_EXPERTS, SEG_MAIN, d), BF16),
            pltpu.VMEM((SEG_OVER, d), BF16),
            pltpu.VMEM((tm, d), F32),
            pltpu.VMEM((tm, d), F32),
            pltpu.SemaphoreType.DMA((N_EXPERTS,)),
            pltpu.SemaphoreType.DMA(()),
        ],
    )
    return pl.pallas_call(
        _combine_kernel,
        grid_spec=grid_spec,
        out_shape=jax.ShapeDtypeStruct((t, d), F32),
        compiler_params=pltpu.CompilerParams(
            dimension_semantics=("arbitrary",), vmem_limit_bytes=_vmem_limit(est)),
        name="combine",
    )(offs, cnts, x, meta, ys)


def _moe(x, g, w_router, w1, w3, w2):
    t, _ = x.shape
    nt = t // MOE_TILE
    n_tiles = _max_expert_tiles(t)
    rank, meta, cnts = _route(x, g, w_router.T)
    cnt2 = cnts.reshape(nt, N_EXPERTS)
    rows_e = jnp.sum(cnt2, axis=0)
    tiles_e = jnp.maximum((rows_e + MOE_TILE - 1) // MOE_TILE, 1)
    ends = jnp.cumsum(tiles_e)
    base_e = (ends - tiles_e) * MOE_TILE
    offs = (base_e[None, :] + jnp.cumsum(cnt2, axis=0) - cnt2).reshape(-1).astype(jnp.int32)
    fill = jnp.concatenate([base_e + rows_e, tiles_e * MOE_TILE - rows_e]).astype(jnp.int32)
    n_valid = ends[-1].reshape(1).astype(jnp.int32)
    tid = jnp.arange(n_tiles, dtype=jnp.int32)
    tile_expert = jnp.minimum(jnp.sum(tid[:, None] >= ends[None, :], axis=1), N_EXPERTS - 1).astype(jnp.int32)
    xs = _scatter(x, g, rank, offs, cnts, fill, n_valid, n_tiles)
    hs = _expert_up(xs, w1, w3, tile_expert, n_valid)
    ys = _expert_down(hs, w2, tile_expert, n_valid)
    return _combine(x, meta, ys, offs, cnts)


def _ffn_kernel(x_ref, g_ref, comb_ref, w1_ref, w3_ref, w2_ref, o_ref, hb_scr, acc_scr):
    e = pl.program_id(1)
    f = pl.program_id(2)

    @pl.when((e == 0) & (f == 0))
    def _():
        hb_scr[...] = (_rms(x_ref[...]) * g_ref[...]).astype(BF16)
        acc_scr[...] = jnp.zeros_like(acc_scr)

    hb = hb_scr[...]
    a = _dot(hb, w1_ref[...])
    b = _dot(hb, w3_ref[...])
    act = (a * jax.nn.sigmoid(a) * b).astype(BF16)
    comb = comb_ref[...]
    col = lax.broadcasted_iota(jnp.int32, comb.shape, 1)
    gate = jnp.sum(jnp.where(col == e, comb, 0.0), axis=1, keepdims=True)
    acc_scr[...] += gate * _dot(act, w2_ref[...])

    @pl.when((e == pl.num_programs(1) - 1) & (f == pl.num_programs(2) - 1))
    def _():
        o_ref[...] = x_ref[...] + acc_scr[...]


def _ffn(x, g, comb, w1, w3, w2):
    t, d = x.shape
    n_e, _, f_dim = w1.shape
    tm = FFN_ROW_TILE
    tf = FFN_COL_TILE if f_dim % FFN_COL_TILE == 0 else 256
    est = 2 * (2 * tm * d * 4 + 3 * d * tf * 2) + tm * d * 6 + 4 * tm * tf * 4
    return pl.pallas_call(
        _ffn_kernel,
        grid=(t // tm, n_e, f_dim // tf),
        in_specs=[
            pl.BlockSpec((tm, d), lambda i, e, f: (i, 0)),
            pl.BlockSpec((1, d), lambda i, e, f: (0, 0)),
            pl.BlockSpec((tm, n_e), lambda i, e, f: (i, 0)),
            pl.BlockSpec((None, d, tf), lambda i, e, f: (e, 0, f)),
            pl.BlockSpec((None, d, tf), lambda i, e, f: (e, 0, f)),
            pl.BlockSpec((None, tf, d), lambda i, e, f: (e, f, 0)),
        ],
        out_specs=pl.BlockSpec((tm, d), lambda i, e, f: (i, 0)),
        out_shape=jax.ShapeDtypeStruct((t, d), F32),
        scratch_shapes=[pltpu.VMEM((tm, d), BF16), pltpu.VMEM((tm, d), F32)],
        compiler_params=pltpu.CompilerParams(
            dimension_semantics=("parallel", "arbitrary", "arbitrary"),
            vmem_limit_bytes=_vmem_limit(est)),
        name="ffn",
    )(x, g, comb, w1, w3, w2)


def _lambda_init(layer):
    return 0.8 - 0.6 * math.exp(-0.3 * layer)


def kernel(x, mem, norm_mix, norm_mem, norm_ffn, w_in, w_gate, b_gate, da_q_norm, da_k_norm,
           da_lam_q1, da_lam_k1, da_lam_q2, da_lam_k2, da_out_norm, pool_w, pool_scale,
           mem_q_norm, mem_k_norm, w_mem_kv, w_br_da, w_br_pool, w_br_mem, w_out,
           ffn_w1, ffn_w3, ffn_w2, moe_router, moe_w1, moe_w3, moe_w2):
    b, s, d = x.shape
    depth = w_in.shape[0]
    xt = x.reshape(b * s, d)
    bf = lambda a: a.astype(BF16)
    row = lambda v: v.reshape(1, -1)
    for l in range(depth):
        lam0 = _lambda_init(l)
        proj = _in_proj(xt, row(norm_mix[l]), bf(w_in[l]))
        mem_k, mem_v = _mem_kv(mem, row(norm_mem[l]), bf(w_mem_kv[l]), row(mem_k_norm[l]))
        lam_vecs = jnp.stack([da_lam_q1[l], da_lam_k1[l], da_lam_q2[l], da_lam_k2[l]])
        o_da = _diff_attn(proj.reshape(b, s, -1), lam_vecs, row(da_q_norm[l]), row(da_k_norm[l]),
                          row(da_out_norm[l]), lam0)
        xt = _mix(xt, proj, o_da.reshape(b * s, -1), mem_k, mem_v, row(norm_mix[l]), bf(w_gate[l]),
                  row(b_gate[l]), bf(pool_w[l]), row(pool_scale[l]), row(mem_q_norm[l]),
                  bf(w_br_da[l]), bf(w_br_pool[l]), bf(w_br_mem[l]), bf(w_out[l]), s)
        j = l // 2
        if l % 2 == 0:
            ones = jnp.ones((b * s, 1), F32)
            xt = _ffn(xt, row(norm_ffn[l]), ones, bf(ffn_w1[j])[None], bf(ffn_w3[j])[None],
                      bf(ffn_w2[j])[None])
        else:
            xt = _moe(xt, row(norm_ffn[l]), moe_router[j], moe_w1[j], moe_w3[j], moe_w2[j])
    return xt.reshape(b, s, d)
```

```python
import functools
import math

import jax
import jax.numpy as jnp
import numpy as np
from jax import lax
from jax.experimental import pallas as pl
from jax.experimental.pallas import tpu as pltpu

F32 = jnp.float32
BF16 = jnp.bfloat16

D_MODEL = 1024
CHUNK = 64
RMS_EPS = 1e-6
DA_HEADS = 8
DA_QK_DIM = 64
DA_V_DIM = 128
POOL_WINDOWS = (2, 4, 8, 16)
POOL_GROUP_DIM = 128
POOL_WIDTH = 512
MEM_HEADS = 4
MEM_HEAD_DIM = 128
MEM_WIDTH = 512
N_EXPERTS = 8
LOG2E = 1.4426950408889634
MASK_VALUE = -1e30

V7X_VMEM_BYTES = 64 * 1024 * 1024
VMEM_LIMIT_CAP = 56 * 1024 * 1024
LANES = 128

ROW_TILE = 512
ATTN_TILE = 256
POOL_HALO = 16
FFN_ROW_TILE = 512
FFN_HIDDEN_CHUNK = 1536


def _vmem_limit(estimate_bytes):
    return int(min(VMEM_LIMIT_CAP, max(32 * 1024 * 1024, 2 * estimate_bytes)))


def _rms(x):
    return x * lax.rsqrt(jnp.mean(x * x, axis=-1, keepdims=True) + RMS_EPS)


def _dot(a, b):
    return jnp.dot(a, b, preferred_element_type=F32)


def _dot_nt(a, b):
    return lax.dot_general(a, b, (((1,), (1,)), ((), ())), preferred_element_type=F32)


def _in_proj_kernel(x_ref, g_ref, w_ref, o_ref, *, col_tile):
    hb = (_rms(x_ref[...]) * g_ref[...]).astype(BF16)
    n = w_ref.shape[1]
    for c0 in range(0, n, col_tile):
        o_ref[:, c0:c0 + col_tile] = _dot(hb, w_ref[:, c0:c0 + col_tile]).astype(BF16)


def _in_proj(x, g, w_bf16):
    t, d = x.shape
    n = w_bf16.shape[1]
    tm = ROW_TILE
    est = 2 * (tm * d * 4 + d * n * 2 + tm * n * 2) + tm * 1024 * 4
    return pl.pallas_call(
        functools.partial(_in_proj_kernel, col_tile=1024),
        grid=(t // tm,),
        in_specs=[
            pl.BlockSpec((tm, d), lambda i: (i, 0)),
            pl.BlockSpec((1, d), lambda i: (0, 0)),
            pl.BlockSpec((d, n), lambda i: (0, 0)),
        ],
        out_specs=pl.BlockSpec((tm, n), lambda i: (i, 0)),
        out_shape=jax.ShapeDtypeStruct((t, n), BF16),
        compiler_params=pltpu.CompilerParams(
            dimension_semantics=("parallel",), vmem_limit_bytes=_vmem_limit(est)),
        name="in_proj",
    )(x, g, w_bf16)


def _mem_kv_kernel(mem_ref, g_ref, w_ref, gk_ref, k_ref, v_ref):
    mh = (_rms(mem_ref[...]) * g_ref[...]).astype(BF16)
    kv = _dot(mh, w_ref[...])
    for hh in range(MEM_HEADS):
        c0 = hh * MEM_HEAD_DIM
        k_ref[:, c0:c0 + MEM_HEAD_DIM] = (_rms(kv[:, c0:c0 + MEM_HEAD_DIM]) * gk_ref[...]).astype(BF16)
    v_ref[...] = kv[:, MEM_WIDTH:].astype(BF16)


def _mem_kv(mem, g, w_bf16, gk):
    b, m, d = mem.shape
    out = jax.ShapeDtypeStruct((b, m, MEM_WIDTH), BF16)
    return pl.pallas_call(
        _mem_kv_kernel,
        grid=(b,),
        in_specs=[
            pl.BlockSpec((None, m, d), lambda i: (i, 0, 0)),
            pl.BlockSpec((1, d), lambda i: (0, 0)),
            pl.BlockSpec((d, 2 * MEM_WIDTH), lambda i: (0, 0)),
            pl.BlockSpec((1, MEM_HEAD_DIM), lambda i: (0, 0)),
        ],
        out_specs=[pl.BlockSpec((None, m, MEM_WIDTH), lambda i: (i, 0, 0))] * 2,
        out_shape=[out, out],
        compiler_params=pltpu.CompilerParams(dimension_semantics=("parallel",)),
        name="mem_kv",
    )(mem, g, w_bf16, gk)


def _diff_attn_kernel(lam_ref, pos_ref, q_ref, k_ref, v_ref, gq_ref, gk_ref, go_ref, o_ref,
                      qa_scr, ka_scr, vt_scr, s_scr, *, lam_init, seq):
    tq = ATTN_TILE
    head = pl.program_id(1)
    slope = jnp.exp2(-(jnp.zeros((1, LANES), F32) + (head + 1).astype(F32)))[:, :1] * LOG2E

    lam_v = lam_ref[...]
    lam = (jnp.exp(jnp.sum(lam_v[0:1] * lam_v[1:2], axis=-1, keepdims=True))
           - jnp.exp(jnp.sum(lam_v[2:3] * lam_v[3:4], axis=-1, keepdims=True)) + lam_init)

    r_i = lax.broadcasted_iota(jnp.int32, (LANES, LANES), 0) // DA_QK_DIM
    c_i = lax.broadcasted_iota(jnp.int32, (LANES, LANES), 1) // DA_QK_DIM
    group_mean = jnp.where(r_i == c_i, 1.0 / DA_QK_DIM, 0.0).astype(BF16)

    def qk_norm(x_bf16, gain):
        xf = x_bf16.astype(F32)
        ms = _dot((xf * xf).astype(BF16), group_mean)
        return xf * (lax.rsqrt(ms + RMS_EPS) * gain)

    lane = lax.broadcasted_iota(jnp.int32, (tq, LANES), 1)
    in_map = (lane < DA_QK_DIM, lane >= DA_QK_DIM)
    lane_row = lax.broadcasted_iota(jnp.int32, (1, LANES), 1)
    ones_at_pieces = jnp.where(lane_row < ALIBI_PIECES, 1.0, 0.0)
    pow2 = jnp.exp2(-(jnp.zeros((1, LANES), F32) + (head + 1).astype(F32)))
    q_gain = gq_ref[...] * (DA_QK_DIM ** -0.5 * LOG2E)
    n_blocks = seq // tq

    def prepare(blk):
        rows = slice(blk * tq, (blk + 1) * tq)
        qn = qk_norm(q_ref[rows, :], q_gain)
        for m in range(2):
            q_rows = slice((2 * blk + m) * tq, (2 * blk + m + 1) * tq)
            qa_scr[q_rows, :LANES] = jnp.where(in_map[m], qn, 0.0).astype(BF16)
            qa_scr[q_rows, LANES:] = jnp.broadcast_to(ones_at_pieces, (tq, LANES)).astype(BF16)
        ka_scr[rows, :LANES] = qk_norm(k_ref[rows, :], gk_ref[...]).astype(BF16)
        ka_scr[rows, LANES:] = (pos_ref[rows, :] * pow2).astype(BF16)
        vt_scr[0:DA_V_DIM, rows] = v_ref[rows, :].astype(F32).T.astype(BF16)
        vt_scr[DA_V_DIM:, rows] = jnp.ones((ATTN_ONES_ROWS, tq), BF16)

    krel = lax.broadcasted_iota(jnp.int32, (tq, 2 * tq), 0)
    qrel = lax.broadcasted_iota(jnp.int32, (tq, 2 * tq), 1) % tq
    allowed = (krel // CHUNK) <= (qrel // CHUNK)
    diag_corr = slope * (-2.0 * jnp.maximum(krel - qrel, 0).astype(F32))

    def col_max_of(s):
        mx = s[0:8, :]
        for r0 in range(8, s.shape[0], 8):
            mx = jnp.maximum(mx, s[r0:r0 + 8, :])
        return jnp.max(mx, axis=0, keepdims=True)

    def scores(i):
        q_i = qa_scr[2 * i * tq:2 * (i + 1) * tq, :]
        diag = slice(i * tq, (i + 1) * tq)
        s = jnp.where(allowed, _dot_nt(ka_scr[diag, :], q_i) + diag_corr, MASK_VALUE)
        s_scr[i % 2, diag, :] = s
        cmax = col_max_of(s)
        if i > 0:
            s = _dot_nt(ka_scr[0:i * tq, :], q_i)
            s_scr[i % 2, 0:i * tq, :] = s
            cmax = jnp.maximum(cmax, col_max_of(s))
        return cmax

    def values(i, col_max):
        n_keys = (i + 1) * tq
        p = jnp.exp2(s_scr[i % 2, 0:n_keys, :] - col_max).astype(BF16)
        acc = _dot(vt_scr[:, 0:n_keys], p)
        ratio = acc[:DA_V_DIM] / acc[DA_V_DIM:DA_V_DIM + 1]
        o = ratio[:, :tq] - lam * ratio[:, tq:]
        o = o * lax.rsqrt(jnp.mean(o * o, axis=0, keepdims=True) + RMS_EPS) * go_ref[...] * (1.0 - lam_init)
        o_ref[i * tq:(i + 1) * tq, :] = o.T.astype(BF16)

    prepare(0)
    col_max = scores(0)
    prepare(1)
    for i in range(n_blocks):
        if i + 2 < n_blocks:
            prepare(i + 2)
        next_max = scores(i + 1) if i + 1 < n_blocks else None
        values(i, col_max)
        col_max = next_max


ATTN_ONES_ROWS = 16
ALIBI_PIECES = 4


def _alibi_position_table(seq):
    rest = np.arange(seq, dtype=np.float64) * LOG2E
    table = np.zeros((seq, LANES), np.float32)
    for p in range(ALIBI_PIECES):
        piece = rest.astype(np.float32).astype(BF16).astype(np.float32)
        table[:, p] = piece
        rest = rest - piece.astype(np.float64)
    return jnp.asarray(table)


def _diff_attn(proj3, lam_vecs, gq, gk, go_col, lam_init):
    b, s, _ = proj3.shape
    hd = 2 * DA_QK_DIM
    kern = functools.partial(_diff_attn_kernel, lam_init=lam_init, seq=s)
    est = (2 * 4 * s * hd * 2 + 4 * s * hd * 2 + (DA_V_DIM + ATTN_ONES_ROWS) * s * 2 + 4 * ATTN_TILE * s * 4
           + 8 * s * hd * 4)
    return pl.pallas_call(
        kern,
        grid=(b, DA_HEADS),
        in_specs=[
            pl.BlockSpec((4, DA_QK_DIM), lambda bi, h: (0, 0)),
            pl.BlockSpec((s, LANES), lambda bi, h: (0, 0)),
            pl.BlockSpec((None, s, hd), lambda bi, h: (bi, 0, h)),
            pl.BlockSpec((None, s, hd), lambda bi, h: (bi, 0, DA_HEADS + h)),
            pl.BlockSpec((None, s, DA_V_DIM), lambda bi, h: (bi, 0, 2 * DA_HEADS + h)),
            pl.BlockSpec((1, hd), lambda bi, h: (0, 0)),
            pl.BlockSpec((1, hd), lambda bi, h: (0, 0)),
            pl.BlockSpec((DA_V_DIM, 1), lambda bi, h: (0, 0)),
        ],
        out_specs=pl.BlockSpec((None, s, DA_V_DIM), lambda bi, h: (bi, 0, h)),
        out_shape=jax.ShapeDtypeStruct((b, s, DA_HEADS * DA_V_DIM), BF16),
        scratch_shapes=[
            pltpu.VMEM((2 * s, 2 * hd), BF16),
            pltpu.VMEM((s, 2 * hd), BF16),
            pltpu.VMEM((DA_V_DIM + ATTN_ONES_ROWS, s), BF16),
            pltpu.VMEM((2, s, 2 * ATTN_TILE), F32),
        ],
        compiler_params=pltpu.CompilerParams(
            dimension_semantics=("parallel", "parallel"), vmem_limit_bytes=_vmem_limit(est)),
        name="diff_attn",
    )(lam_vecs, _alibi_position_table(s), proj3, proj3, proj3, gq, gk, go_col)


def _mix_kernel(x_ref, pool_ref, halo_ref, mq_ref, oda_ref, mk_ref, mv_ref,
                gmix_ref, wg_ref, bg_ref, wpool_ref, pscale_ref, gmq_ref,
                wda_ref, wbp_ref, wbm_ref, wout_ref, o_ref, *, tiles_per_seq):
    tm = x_ref.shape[0]
    d = D_MODEL
    x = x_ref[...]
    hb = (_rms(x) * gmix_ref[...]).astype(BF16)

    tile_in_seq = pl.program_id(0) % tiles_per_seq
    p_cur = pool_ref[...].astype(F32)
    halo = jnp.where(tile_in_seq == 0, 0.0, halo_ref[...].astype(F32))
    ext = jnp.concatenate([halo, p_cur], axis=0)
    t_pos = tile_in_seq * tm + lax.broadcasted_iota(jnp.int32, (tm, 1), 0)
    pool_parts = []
    for g, w in enumerate(POOL_WINDOWS):
        c0 = g * POOL_GROUP_DIM
        s = ext[:, c0:c0 + POOL_GROUP_DIM]
        sh = 1
        while sh < w:
            s = s + pltpu.roll(s, sh, axis=0)
            sh *= 2
        cnt = jnp.minimum(t_pos + 1, w).astype(F32)
        dlt = (s[POOL_HALO:] / cnt - p_cur[:, c0:c0 + POOL_GROUP_DIM]).astype(BF16)
        pool_parts.append(_dot(dlt, wpool_ref[g]))
    o_pool = (jnp.concatenate(pool_parts, axis=1) * pscale_ref[...]).astype(BF16)

    mq = mq_ref[...].astype(F32)
    mem_parts = []
    for hh in range(MEM_HEADS):
        c0 = hh * MEM_HEAD_DIM
        qh = (_rms(mq[:, c0:c0 + MEM_HEAD_DIM]) * gmq_ref[...] * (MEM_HEAD_DIM ** -0.5)).astype(BF16)
        s = _dot_nt(qh, mk_ref[:, c0:c0 + MEM_HEAD_DIM])
        p = jnp.exp(s - jnp.max(s, axis=-1, keepdims=True))
        l = jnp.sum(p, axis=-1, keepdims=True)
        mem_parts.append(_dot(p.astype(BF16), mv_ref[:, c0:c0 + MEM_HEAD_DIM]) / l)
    o_mem = jnp.concatenate(mem_parts, axis=1).astype(BF16)

    def gate(k):
        return jax.nn.sigmoid(_dot(hb, wg_ref[:, k * d:(k + 1) * d]) + bg_ref[:, k * d:(k + 1) * d])

    merged = gate(0) * _dot(oda_ref[...], wda_ref[...])
    merged = merged + gate(1) * _dot(o_pool, wbp_ref[...])
    merged = merged + gate(2) * _dot(o_mem, wbm_ref[...])
    o_ref[...] = x + _dot(merged.astype(BF16), wout_ref[...])


def _mix(x, proj, o_da, mem_k, mem_v, gmix, wg, bg, wpool, pscale, gmq, wda, wbp, wbm, wout, seq):
    t, d = x.shape
    tm = ROW_TILE
    tiles_per_seq = seq // tm
    halo_blocks_per_tile = tm // POOL_HALO
    pool_col = (proj.shape[1] - POOL_WIDTH - MEM_WIDTH) // POOL_WIDTH
    mem_len = mem_k.shape[1]
    const = lambda i: (0, 0)
    weights_bytes = 2 * (wg.size + wda.size + wbp.size + wbm.size + wout.size + wpool.size)
    est = 2 * weights_bytes + 2 * (2 * tm * d * 4 + tm * (d + 2 * POOL_WIDTH) * 2) + 8 * tm * d * 4
    return pl.pallas_call(
        functools.partial(_mix_kernel, tiles_per_seq=tiles_per_seq),
        grid=(t // tm,),
        in_specs=[
            pl.BlockSpec((tm, d), lambda i: (i, 0)),
            pl.BlockSpec((tm, POOL_WIDTH), lambda i: (i, pool_col)),
            pl.BlockSpec((POOL_HALO, POOL_WIDTH),
                         lambda i: (jnp.maximum(i * halo_blocks_per_tile - 1, 0), pool_col)),
            pl.BlockSpec((tm, MEM_WIDTH), lambda i: (i, pool_col + 1)),
            pl.BlockSpec((tm, d), lambda i: (i, 0)),
            pl.BlockSpec((None, mem_len, MEM_WIDTH), lambda i: (i // tiles_per_seq, 0, 0)),
            pl.BlockSpec((None, mem_len, MEM_WIDTH), lambda i: (i // tiles_per_seq, 0, 0)),
            pl.BlockSpec((1, d), const),
            pl.BlockSpec(wg.shape, const),
            pl.BlockSpec((1, bg.shape[1]), const),
            pl.BlockSpec(wpool.shape, lambda i: (0, 0, 0)),
            pl.BlockSpec((1, POOL_WIDTH), const),
            pl.BlockSpec((1, MEM_HEAD_DIM), const),
            pl.BlockSpec(wda.shape, const),
            pl.BlockSpec(wbp.shape, const),
            pl.BlockSpec(wbm.shape, const),
            pl.BlockSpec(wout.shape, const),
        ],
        out_specs=pl.BlockSpec((tm, d), lambda i: (i, 0)),
        out_shape=jax.ShapeDtypeStruct((t, d), F32),
        compiler_params=pltpu.CompilerParams(
            dimension_semantics=("parallel",), vmem_limit_bytes=_vmem_limit(est)),
        name="mix",
    )(x, proj, proj, proj, o_da, mem_k, mem_v, gmix, wg, bg, wpool, pscale, gmq, wda, wbp, wbm, wout)


def _split3(x):
    hi = x.astype(BF16)
    r = x - hi.astype(F32)
    mid = r.astype(BF16)
    lo = (r - mid.astype(F32)).astype(BF16)
    return hi, mid, lo


MOE_TILE = 512
SEG_MAIN = 192
SEG_OVER = MOE_TILE - SEG_MAIN
SEG_ALIGN = 16
NOT_ROUTED = -1e6
UP_COL_TILE = 1792
DOWN_COL_TILE = 512
COPY_SIZES = (512, 256, 128, 64, 32, 16)


def _max_expert_tiles(t):
    padded_rows = 2 * t + (t // MOE_TILE) * N_EXPERTS * (SEG_ALIGN - 1)
    return padded_rows // MOE_TILE + N_EXPERTS


def _row_copies(src, dst, src_row0, dst_row0, n_rows, max_rows, sem):
    pairs = []
    pos = 0
    for size in COPY_SIZES:
        if size > max_rows:
            continue
        cond = jnp.bitwise_and(n_rows, size) != 0
        cp = pltpu.make_async_copy(
            src.at[pl.ds(pl.multiple_of(src_row0 + pos, SEG_ALIGN), size), :],
            dst.at[pl.ds(pl.multiple_of(dst_row0 + pos, SEG_ALIGN), size), :], sem)
        pairs.append((cond, cp))
        pos = pos + jnp.where(cond, size, 0)
    return pairs


def _start_all(pairs):
    for cond, cp in pairs:
        pl.when(cond)(cp.start)


def _wait_all(pairs):
    for cond, cp in pairs:
        pl.when(cond)(cp.wait)


def _onehot_rows(rank_row, n_rows, row0):
    r = lax.broadcasted_iota(jnp.int32, (n_rows, rank_row.shape[1]), 0).astype(F32) + float(row0)
    return jnp.where(r == rank_row, 1.0, 0.0).astype(BF16)


def _onehot_cols(rank_col, n_cols):
    c = lax.broadcasted_iota(jnp.int32, (rank_col.shape[0], n_cols), 1).astype(F32)
    return jnp.where(c == rank_col, 1.0, 0.0).astype(BF16)


def _route_kernel(x_ref, g_ref, wr_ref, rank_ref, meta_ref, cnts_ref):
    i = pl.program_id(0)
    tm = x_ref.shape[0]
    h2 = _rms(x_ref[...]) * g_ref[...]
    h_hi, h_mid, _ = _split3(h2)
    w_hi, w_mid, _ = _split3(wr_ref[...])
    logits = _dot_nt(w_hi, h_hi) + (_dot_nt(w_hi, h_mid) + _dot_nt(w_mid, h_hi))
    eidx = lax.broadcasted_iota(jnp.int32, logits.shape, 0).astype(F32)
    m1 = jnp.max(logits, axis=0, keepdims=True)
    i1 = jnp.min(jnp.where(logits == m1, eidx, float(N_EXPERTS)), axis=0, keepdims=True)
    sel1 = eidx == i1
    rest = jnp.where(sel1, -jnp.inf, logits)
    m2 = jnp.max(rest, axis=0, keepdims=True)
    i2 = jnp.min(jnp.where(rest == m2, eidx, float(N_EXPERTS)), axis=0, keepdims=True)
    sel2 = eidx == i2
    e2 = jnp.exp(m2 - m1)
    g1 = 1.0 / (1.0 + e2)
    g2 = e2 / (1.0 + e2)

    mask = jnp.where(sel1 | sel2, 1.0, 0.0)
    before = (lax.broadcasted_iota(jnp.int32, (tm, tm), 0)
              < lax.broadcasted_iota(jnp.int32, (tm, tm), 1)).astype(BF16)
    rank = _dot(mask.astype(BF16), before)
    counts = jnp.sum(mask, axis=1, keepdims=True)
    rank_m = jnp.where(mask > 0.5, rank, NOT_ROUTED)

    rk1 = jnp.where(sel1, rank, NOT_ROUTED)
    rk2 = jnp.where(sel2, rank, NOT_ROUTED)
    gates = jnp.where(eidx == 0.0, g1, jnp.where(eidx == 1.0, g2, 0.0))
    packed = jnp.concatenate([rk1, rk2, gates], axis=0)
    ident = (lax.broadcasted_iota(jnp.int32, (tm, tm), 0)
             == lax.broadcasted_iota(jnp.int32, (tm, tm), 1)).astype(BF16)
    p_hi, p_mid, p_lo = _split3(packed)
    meta_ref[...] = _dot_nt(ident, p_hi) + (_dot_nt(ident, p_mid) + _dot_nt(ident, p_lo))

    rank_ref[...] = rank_m
    for e in range(N_EXPERTS):
        cnt = counts[e, 0].astype(jnp.int32)
        cnts_ref[i * N_EXPERTS + e] = jnp.bitwise_and(cnt + (SEG_ALIGN - 1), -SEG_ALIGN)


def _route(x, g, w_router_t):
    t, d = x.shape
    tm = MOE_TILE
    nt = t // tm
    est = 2 * tm * d * 4 + 6 * tm * tm * 4 + 4 * tm * d * 4
    return pl.pallas_call(
        _route_kernel,
        grid=(nt,),
        in_specs=[
            pl.BlockSpec((tm, d), lambda i: (i, 0)),
            pl.BlockSpec((1, d), lambda i: (0, 0)),
            pl.BlockSpec((N_EXPERTS, d), lambda i: (0, 0)),
        ],
        out_specs=[
            pl.BlockSpec((N_EXPERTS, tm), lambda i: (0, i)),
            pl.BlockSpec((tm, 3 * N_EXPERTS), lambda i: (i, 0)),
            pl.BlockSpec(memory_space=pltpu.SMEM),
        ],
        out_shape=[
            jax.ShapeDtypeStruct((N_EXPERTS, t), F32),
            jax.ShapeDtypeStruct((t, 3 * N_EXPERTS), F32),
            jax.ShapeDtypeStruct((nt * N_EXPERTS,), jnp.int32),
        ],
        compiler_params=pltpu.CompilerParams(
            dimension_semantics=("arbitrary",), vmem_limit_bytes=_vmem_limit(est)),
        name="route",
    )(x, g, w_router_t)


def _scatter_kernel(offs_ref, cnts_ref, fill_ref, nv_ref, x_ref, g_ref, rank_ref, xs_hbm,
                    stage, over_stage, zero_buf, sems, over_sem, *, n_tiles):
    i = pl.program_id(0)
    hb = (_rms(x_ref[...]) * g_ref[...]).astype(BF16)
    rank_m = rank_ref[...]

    def main_copies(step, e):
        n_main = jnp.minimum(cnts_ref[step * N_EXPERTS + e], SEG_MAIN)
        return _row_copies(stage.at[e], xs_hbm, 0, offs_ref[step * N_EXPERTS + e], n_main, SEG_MAIN,
                           sems.at[e])

    @pl.when(i > 0)
    def _():
        for e in range(N_EXPERTS):
            _wait_all(main_copies(i - 1, e))

    for e in range(N_EXPERTS):
        rank_e = rank_m[e:e + 1, :]
        stage[e] = _dot(_onehot_rows(rank_e, SEG_MAIN, 0), hb).astype(BF16)
        _start_all(main_copies(i, e))
        n_over = cnts_ref[i * N_EXPERTS + e] - SEG_MAIN

        @pl.when(n_over > 0)
        def _():
            over_stage[...] = _dot(_onehot_rows(rank_e, SEG_OVER, SEG_MAIN), hb).astype(BF16)
            pairs = _row_copies(over_stage, xs_hbm, 0, offs_ref[i * N_EXPERTS + e] + SEG_MAIN, n_over,
                                SEG_OVER, over_sem)
            _start_all(pairs)
            _wait_all(pairs)

    @pl.when(i == pl.num_programs(0) - 1)
    def _():
        for e in range(N_EXPERTS):
            _wait_all(main_copies(i, e))
        zero_buf[...] = jnp.zeros_like(zero_buf)
        for e in range(N_EXPERTS):
            pairs = _row_copies(zero_buf, xs_hbm, 0, fill_ref[e], fill_ref[N_EXPERTS + e], MOE_TILE,
                                sems.at[e])
            _start_all(pairs)
            _wait_all(pairs)

        def zero_tile(tile, carry):
            dst = xs_hbm.at[pl.ds(pl.multiple_of(tile * MOE_TILE, MOE_TILE), MOE_TILE), :]
            cp = pltpu.make_async_copy(zero_buf, dst, over_sem)
            cp.start()
            cp.wait()
            return carry

        lax.fori_loop(nv_ref[0], n_tiles, zero_tile, 0)


def _scatter(x, g, rank, offs, cnts, fill, n_valid, n_tiles):
    t, d = x.shape
    tm = MOE_TILE
    est = 2 * tm * d * 4 + (N_EXPERTS * SEG_MAIN + SEG_OVER + MOE_TILE) * d * 2 + 6 * tm * d * 4
    grid_spec = pltpu.PrefetchScalarGridSpec(
        num_scalar_prefetch=4,
        grid=(t // tm,),
        in_specs=[
            pl.BlockSpec((tm, d), lambda i, *_: (i, 0)),
            pl.BlockSpec((1, d), lambda i, *_: (0, 0)),
            pl.BlockSpec((N_EXPERTS, tm), lambda i, *_: (0, i)),
        ],
        out_specs=pl.BlockSpec(memory_space=pl.ANY),
        scratch_shapes=[
            pltpu.VMEM((N_EXPERTS, SEG_MAIN, d), BF16),
            pltpu.VMEM((SEG_OVER, d), BF16),
            pltpu.VMEM((MOE_TILE, d), BF16),
            pltpu.SemaphoreType.DMA((N_EXPERTS,)),
            pltpu.SemaphoreType.DMA(()),
        ],
    )
    return pl.pallas_call(
        functools.partial(_scatter_kernel, n_tiles=n_tiles),
        grid_spec=grid_spec,
        out_shape=jax.ShapeDtypeStruct((n_tiles * MOE_TILE, d), BF16),
        compiler_params=pltpu.CompilerParams(
            dimension_semantics=("arbitrary",), vmem_limit_bytes=_vmem_limit(est)),
        name="scatter",
    )(offs, cnts, fill, n_valid, x, g, rank)


def _expert_changed(te_ref, t):
    return (t == 0) | (te_ref[t] != te_ref[jnp.maximum(t - 1, 0)])


def _expert_up_kernel(te_ref, nv_ref, x_ref, w1_ref, w3_ref, h_ref, w1b, w3b):
    t = pl.program_id(1)

    @pl.when(_expert_changed(te_ref, t))
    def _():
        w1b[...] = w1_ref[...].astype(BF16)
        w3b[...] = w3_ref[...].astype(BF16)

    @pl.when(t < nv_ref[0])
    def _():
        x = x_ref[...]
        a = _dot(x, w1b[...])
        b = _dot(x, w3b[...])
        h_ref[...] = (a * jax.nn.sigmoid(a) * b).astype(BF16)

    @pl.when(t >= nv_ref[0])
    def _():
        h_ref[...] = jnp.zeros_like(h_ref)


def _expert_up(xs, w1, w3, tile_expert, n_valid):
    rows, d = xs.shape
    _, _, f_dim = w1.shape
    tf = UP_COL_TILE
    est = 2 * (MOE_TILE * d * 2 + 2 * d * tf * 4 + MOE_TILE * tf * 2) + 2 * d * tf * 2 + 3 * MOE_TILE * tf * 4
    grid_spec = pltpu.PrefetchScalarGridSpec(
        num_scalar_prefetch=2,
        grid=(f_dim // tf, rows // MOE_TILE),
        in_specs=[
            pl.BlockSpec((MOE_TILE, d), lambda f, t, te, nv: (t, 0)),
            pl.BlockSpec((None, d, tf), lambda f, t, te, nv: (te[t], 0, f)),
            pl.BlockSpec((None, d, tf), lambda f, t, te, nv: (te[t], 0, f)),
        ],
        out_specs=pl.BlockSpec((MOE_TILE, tf), lambda f, t, te, nv: (t, f)),
        scratch_shapes=[pltpu.VMEM((d, tf), BF16), pltpu.VMEM((d, tf), BF16)],
    )
    return pl.pallas_call(
        _expert_up_kernel,
        grid_spec=grid_spec,
        out_shape=jax.ShapeDtypeStruct((rows, f_dim), BF16),
        compiler_params=pltpu.CompilerParams(
            dimension_semantics=("arbitrary", "arbitrary"), vmem_limit_bytes=_vmem_limit(est)),
        name="expert_up",
    )(tile_expert, n_valid, xs, w1, w3)


def _expert_down_kernel(te_ref, nv_ref, h_ref, w2_ref, y_ref, w2b):
    t = pl.program_id(1)

    @pl.when(_expert_changed(te_ref, t))
    def _():
        w2b[...] = w2_ref[...].astype(BF16)

    @pl.when(t < nv_ref[0])
    def _():
        y_ref[...] = _dot(h_ref[...], w2b[...]).astype(BF16)

    @pl.when(t >= nv_ref[0])
    def _():
        y_ref[...] = jnp.zeros_like(y_ref)


def _expert_down(hs, w2, tile_expert, n_valid):
    rows, f_dim = hs.shape
    d = w2.shape[2]
    tn = DOWN_COL_TILE
    est = 2 * (MOE_TILE * f_dim * 2 + f_dim * tn * 4 + MOE_TILE * tn * 2) + f_dim * tn * 2 + MOE_TILE * tn * 4
    grid_spec = pltpu.PrefetchScalarGridSpec(
        num_scalar_prefetch=2,
        grid=(d // tn, rows // MOE_TILE),
        in_specs=[
            pl.BlockSpec((MOE_TILE, f_dim), lambda n, t, te, nv: (t, 0)),
            pl.BlockSpec((None, f_dim, tn), lambda n, t, te, nv: (te[t], 0, n)),
        ],
        out_specs=pl.BlockSpec((MOE_TILE, tn), lambda n, t, te, nv: (t, n)),
        scratch_shapes=[pltpu.VMEM((f_dim, tn), BF16)],
    )
    return pl.pallas_call(
        _expert_down_kernel,
        grid_spec=grid_spec,
        out_shape=jax.ShapeDtypeStruct((rows, d), BF16),
        compiler_params=pltpu.CompilerParams(
            dimension_semantics=("arbitrary", "arbitrary"), vmem_limit_bytes=_vmem_limit(est)),
        name="expert_down",
    )(tile_expert, n_valid, hs, w2)


def _combine_kernel(offs_ref, cnts_ref, x_ref, meta_ref, y_hbm, o_ref,
                    ybuf, over_buf, over1, over2, sems, over_sem):
    i = pl.program_id(0)
    tm = x_ref.shape[0]
    total_rows = y_hbm.shape[0]
    meta = meta_ref[...]
    g1 = meta[:, 2 * N_EXPERTS:2 * N_EXPERTS + 1]
    g2 = meta[:, 2 * N_EXPERTS + 1:2 * N_EXPERTS + 2]

    def window(e, seg_row0, n_rows):
        off = offs_ref[i * N_EXPERTS + e] + seg_row0
        start = jnp.minimum(off, total_rows - n_rows)
        return start, off - start

    copies = []
    for e in range(N_EXPERTS):
        start, _ = window(e, 0, SEG_MAIN)
        src = y_hbm.at[pl.ds(pl.multiple_of(start, SEG_ALIGN), SEG_MAIN), :]
        copies.append(pltpu.make_async_copy(src, ybuf.at[e], sems.at[e]))
        copies[-1].start()

    out1 = jnp.zeros((tm, D_MODEL), F32)
    out2 = jnp.zeros((tm, D_MODEL), F32)
    any_over = cnts_ref[i * N_EXPERTS] > SEG_MAIN
    for e in range(N_EXPERTS):
        _, shift = window(e, 0, SEG_MAIN)
        sh = shift.astype(F32)
        sel_a = _onehot_cols(meta[:, e:e + 1] + sh, SEG_MAIN)
        sel_b = _onehot_cols(meta[:, N_EXPERTS + e:N_EXPERTS + e + 1] + sh, SEG_MAIN)
        copies[e].wait()
        out1 = out1 + _dot(sel_a, ybuf[e])
        out2 = out2 + _dot(sel_b, ybuf[e])
        if e > 0:
            any_over = any_over | (cnts_ref[i * N_EXPERTS + e] > SEG_MAIN)
    o_ref[...] = x_ref[...] + (g1 * out1 + g2 * out2)

    @pl.when(any_over)
    def _():
        over1[...] = jnp.zeros_like(over1)
        over2[...] = jnp.zeros_like(over2)
        for e in range(N_EXPERTS):
            @pl.when(cnts_ref[i * N_EXPERTS + e] > SEG_MAIN)
            def _():
                start, shift = window(e, SEG_MAIN, SEG_OVER)
                src = y_hbm.at[pl.ds(pl.multiple_of(start, SEG_ALIGN), SEG_OVER), :]
                cp = pltpu.make_async_copy(src, over_buf, over_sem)
                cp.start()
                cp.wait()
                sh = shift.astype(F32) - float(SEG_MAIN)
                yo = over_buf[...]
                over1[...] += _dot(_onehot_cols(meta[:, e:e + 1] + sh, SEG_OVER), yo)
                over2[...] += _dot(_onehot_cols(meta[:, N_EXPERTS + e:N_EXPERTS + e + 1] + sh, SEG_OVER), yo)
        o_ref[...] += g1 * over1[...] + g2 * over2[...]


def _combine(x, meta, ys, offs, cnts):
    t, d = x.shape
    tm = MOE_TILE
    est = (4 * tm * d * 4 + (N_EXPERTS * SEG_MAIN + SEG_OVER) * d * 2 + 2 * tm * d * 4
           + 2 * tm * LANES * 4 + 6 * tm * d * 4)
    grid_spec = pltpu.PrefetchScalarGridSpec(
        num_scalar_prefetch=2,
        grid=(t // tm,),
        in_specs=[
            pl.BlockSpec((tm, d), lambda i, *_: (i, 0)),
            pl.BlockSpec((tm, 3 * N_EXPERTS), lambda i, *_: (i, 0)),
            pl.BlockSpec(memory_space=pl.ANY),
        ],
        out_specs=pl.BlockSpec((tm, d), lambda i, *_: (i, 0)),
        scratch_shapes=[
            pltpu.VMEM((N_EXPERTS, SEG_MAIN, d), BF16),
            pltpu.VMEM((SEG_OVER, d), BF16),
            pltpu.VMEM((tm, d), F32),
            pltpu.VMEM((tm, d), F32),
            pltpu.SemaphoreType.DMA((N_EXPERTS,)),
            pltpu.SemaphoreType.DMA(()),
        ],
    )
    return pl.pallas_call(
        _combine_kernel,
        grid_spec=grid_spec,
        out_shape=jax.ShapeDtypeStruct((t, d), F32),
        compiler_params=pltpu.CompilerParams(
            dimension_semantics=("arbitrary",), vmem_limit_bytes=_vmem_limit(est)),
        name="combine",
    )(offs, cnts, x, meta, ys)


def _moe(x, g, w_router, w1, w3, w2):
    t, _ = x.shape
    nt = t // MOE_TILE
    n_tiles = _max_expert_tiles(t)
    rank, meta, cnts = _route(x, g, w_router.T)
    cnt2 = cnts.reshape(nt, N_EXPERTS)
    rows_e = jnp.sum(cnt2, axis=0)
    tiles_e = jnp.maximum((rows_e + MOE_TILE - 1) // MOE_TILE, 1)
    ends = jnp.cumsum(tiles_e)
    base_e = (ends - tiles_e) * MOE_TILE
    offs = (base_e[None, :] + jnp.cumsum(cnt2, axis=0) - cnt2).reshape(-1).astype(jnp.int32)
    fill = jnp.concatenate([base_e + rows_e, tiles_e * MOE_TILE - rows_e]).astype(jnp.int32)
    n_valid = ends[-1].reshape(1).astype(jnp.int32)
    tid = jnp.arange(n_tiles, dtype=jnp.int32)
    tile_expert = jnp.minimum(jnp.sum(tid[:, None] >= ends[None, :], axis=1), N_EXPERTS - 1).astype(jnp.int32)
    xs = _scatter(x, g, rank, offs, cnts, fill, n_valid, n_tiles)
    hs = _expert_up(xs, w1, w3, tile_expert, n_valid)
    ys = _expert_down(hs, w2, tile_expert, n_valid)
    return _combine(x, meta, ys, offs, cnts)


def _ffn_kernel(x_ref, g_ref, w1_ref, w3_ref, w2_ref, o_ref):
    x = x_ref[...]
    hb = (_rms(x) * g_ref[...]).astype(BF16)
    f_dim = w1_ref.shape[1]
    out = x
    for c0 in range(0, f_dim, FFN_HIDDEN_CHUNK):
        c1 = min(c0 + FFN_HIDDEN_CHUNK, f_dim)
        a = _dot(hb, w1_ref[:, c0:c1])
        b = _dot(hb, w3_ref[:, c0:c1])
        out = out + _dot((a * jax.nn.sigmoid(a) * b).astype(BF16), w2_ref[c0:c1, :])
    o_ref[...] = out


def _ffn(x, g, w1, w3, w2):
    t, d = x.shape
    f_dim = w1.shape[1]
    tm = FFN_ROW_TILE
    resident = pl.Buffered(1)
    est = 3 * d * f_dim * 2 + 4 * tm * d * 4 + 4 * tm * FFN_HIDDEN_CHUNK * 4
    return pl.pallas_call(
        _ffn_kernel,
        grid=(t // tm,),
        in_specs=[
            pl.BlockSpec((tm, d), lambda i: (i, 0)),
            pl.BlockSpec((1, d), lambda i: (0, 0)),
            pl.BlockSpec((d, f_dim), lambda i: (0, 0), pipeline_mode=resident),
            pl.BlockSpec((d, f_dim), lambda i: (0, 0), pipeline_mode=resident),
            pl.BlockSpec((f_dim, d), lambda i: (0, 0), pipeline_mode=resident),
        ],
        out_specs=pl.BlockSpec((tm, d), lambda i: (i, 0)),
        out_shape=jax.ShapeDtypeStruct((t, d), F32),
        compiler_params=pltpu.CompilerParams(
            dimension_semantics=("parallel",), vmem_limit_bytes=_vmem_limit(est)),
        name="ffn",
    )(x, g, w1, w3, w2)


def _lambda_init(layer):
    return 0.8 - 0.6 * math.exp(-0.3 * layer)


def kernel(x, mem, norm_mix, norm_mem, norm_ffn, w_in, w_gate, b_gate, da_q_norm, da_k_norm,
           da_lam_q1, da_lam_k1, da_lam_q2, da_lam_k2, da_out_norm, pool_w, pool_scale,
           mem_q_norm, mem_k_norm, w_mem_kv, w_br_da, w_br_pool, w_br_mem, w_out,
           ffn_w1, ffn_w3, ffn_w2, moe_router, moe_w1, moe_w3, moe_w2):
    b, s, d = x.shape
    depth = w_in.shape[0]
    xt = x.reshape(b * s, d)
    bf = lambda a: a.astype(BF16)
    row = lambda v: v.reshape(1, -1)
    for l in range(depth):
        lam0 = _lambda_init(l)
        proj = _in_proj(xt, row(norm_mix[l]), bf(w_in[l]))
        mem_k, mem_v = _mem_kv(mem, row(norm_mem[l]), bf(w_mem_kv[l]), row(mem_k_norm[l]))
        lam_vecs = jnp.stack([da_lam_q1[l], da_lam_k1[l], da_lam_q2[l], da_lam_k2[l]])
        o_da = _diff_attn(proj.reshape(b, s, -1), lam_vecs, row(da_q_norm[l]), row(da_k_norm[l]),
                          da_out_norm[l].reshape(-1, 1), lam0)
        xt = _mix(xt, proj, o_da.reshape(b * s, -1), mem_k, mem_v, row(norm_mix[l]), bf(w_gate[l]),
                  row(b_gate[l]), bf(pool_w[l]), row(pool_scale[l]), row(mem_q_norm[l]),
                  bf(w_br_da[l]), bf(w_br_pool[l]), bf(w_br_mem[l]), bf(w_out[l]), s)
        j = l // 2
        if l % 2 == 0:
            xt = _ffn(xt, row(norm_ffn[l]), bf(ffn_w1[j]), bf(ffn_w3[j]), bf(ffn_w2[j]))
        else:
            xt = _moe(xt, row(norm_ffn[l]), moe_router[j], moe_w1[j], moe_w3[j], moe_w2[j])
    return xt.reshape(b, s, d)
```

```python
import functools
import math

import jax
import jax.numpy as jnp
import numpy as np
from jax import lax
from jax.experimental import pallas as pl
from jax.experimental.pallas import tpu as pltpu

F32 = jnp.float32
BF16 = jnp.bfloat16

D_MODEL = 1024
CHUNK = 64
RMS_EPS = 1e-6
DA_HEADS = 8
DA_QK_DIM = 64
DA_V_DIM = 128
POOL_WINDOWS = (2, 4, 8, 16)
POOL_GROUP_DIM = 128
POOL_WIDTH = 512
MEM_HEADS = 4
MEM_HEAD_DIM = 128
MEM_WIDTH = 512
N_EXPERTS = 8
LOG2E = 1.4426950408889634
MASK_VALUE = -1e30

V7X_VMEM_BYTES = 64 * 1024 * 1024
VMEM_LIMIT_CAP = 56 * 1024 * 1024
LANES = 128

ROW_TILE = 512
ATTN_TILE = 256
POOL_HALO = 16
FFN_ROW_TILE = 512
FFN_HIDDEN_CHUNK = 1536


def _vmem_limit(estimate_bytes):
    return int(min(VMEM_LIMIT_CAP, max(32 * 1024 * 1024, 2 * estimate_bytes)))


def _rms(x):
    return x * lax.rsqrt(jnp.mean(x * x, axis=-1, keepdims=True) + RMS_EPS)


def _dot(a, b):
    return jnp.dot(a, b, preferred_element_type=F32)


def _dot_nt(a, b):
    return lax.dot_general(a, b, (((1,), (1,)), ((), ())), preferred_element_type=F32)


def _in_proj_kernel(x_ref, g_ref, w_ref, o_ref, *, col_tile):
    hb = (_rms(x_ref[...]) * g_ref[...]).astype(BF16)
    n = w_ref.shape[1]
    for c0 in range(0, n, col_tile):
        o_ref[:, c0:c0 + col_tile] = _dot(hb, w_ref[:, c0:c0 + col_tile]).astype(BF16)


def _in_proj(x, g, w_bf16):
    t, d = x.shape
    n = w_bf16.shape[1]
    tm = ROW_TILE
    est = 2 * (tm * d * 4 + d * n * 2 + tm * n * 2) + tm * 1024 * 4
    return pl.pallas_call(
        functools.partial(_in_proj_kernel, col_tile=1024),
        grid=(t // tm,),
        in_specs=[
            pl.BlockSpec((tm, d), lambda i: (i, 0)),
            pl.BlockSpec((1, d), lambda i: (0, 0)),
            pl.BlockSpec((d, n), lambda i: (0, 0)),
        ],
        out_specs=pl.BlockSpec((tm, n), lambda i: (i, 0)),
        out_shape=jax.ShapeDtypeStruct((t, n), BF16),
        compiler_params=pltpu.CompilerParams(
            dimension_semantics=("parallel",), vmem_limit_bytes=_vmem_limit(est)),
        name="in_proj",
    )(x, g, w_bf16)


def _mem_kv_kernel(mem_ref, g_ref, w_ref, gk_ref, k_ref, v_ref):
    mh = (_rms(mem_ref[...]) * g_ref[...]).astype(BF16)
    kv = _dot(mh, w_ref[...])
    for hh in range(MEM_HEADS):
        c0 = hh * MEM_HEAD_DIM
        k_ref[:, c0:c0 + MEM_HEAD_DIM] = (_rms(kv[:, c0:c0 + MEM_HEAD_DIM]) * gk_ref[...]).astype(BF16)
    v_ref[...] = kv[:, MEM_WIDTH:].astype(BF16)


def _mem_kv(mem, g, w_bf16, gk):
    b, m, d = mem.shape
    out = jax.ShapeDtypeStruct((b, m, MEM_WIDTH), BF16)
    return pl.pallas_call(
        _mem_kv_kernel,
        grid=(b,),
        in_specs=[
            pl.BlockSpec((None, m, d), lambda i: (i, 0, 0)),
            pl.BlockSpec((1, d), lambda i: (0, 0)),
            pl.BlockSpec((d, 2 * MEM_WIDTH), lambda i: (0, 0)),
            pl.BlockSpec((1, MEM_HEAD_DIM), lambda i: (0, 0)),
        ],
        out_specs=[pl.BlockSpec((None, m, MEM_WIDTH), lambda i: (i, 0, 0))] * 2,
        out_shape=[out, out],
        compiler_params=pltpu.CompilerParams(dimension_semantics=("parallel",)),
        name="mem_kv",
    )(mem, g, w_bf16, gk)


def _diff_attn_kernel(lam_ref, pos_ref, q_ref, k_ref, v_ref, gq_ref, gk_ref, go_ref, o_ref,
                      qa_scr, ka_scr, vt_scr, s_scr, *, lam_init, seq):
    tq = ATTN_TILE
    head = pl.program_id(1)
    slope = jnp.exp2(-(jnp.zeros((1, LANES), F32) + (head + 1).astype(F32)))[:, :1] * LOG2E

    lam_v = lam_ref[...]
    lam = (jnp.exp(jnp.sum(lam_v[0:1] * lam_v[1:2], axis=-1, keepdims=True))
           - jnp.exp(jnp.sum(lam_v[2:3] * lam_v[3:4], axis=-1, keepdims=True)) + lam_init)

    r_i = lax.broadcasted_iota(jnp.int32, (LANES, LANES), 0) // DA_QK_DIM
    c_i = lax.broadcasted_iota(jnp.int32, (LANES, LANES), 1) // DA_QK_DIM
    group_mean = jnp.where(r_i == c_i, 1.0 / DA_QK_DIM, 0.0).astype(BF16)

    def qk_norm(x_bf16, gain):
        xf = x_bf16.astype(F32)
        ms = _dot((xf * xf).astype(BF16), group_mean)
        return xf * (lax.rsqrt(ms + RMS_EPS) * gain)

    lane = lax.broadcasted_iota(jnp.int32, (tq, LANES), 1)
    in_map = (lane < DA_QK_DIM, lane >= DA_QK_DIM)
    lane_row = lax.broadcasted_iota(jnp.int32, (1, LANES), 1)
    ones_at_pieces = jnp.where(lane_row < ALIBI_PIECES, 1.0, 0.0)
    pow2 = jnp.exp2(-(jnp.zeros((1, LANES), F32) + (head + 1).astype(F32)))
    q_gain = gq_ref[...] * (DA_QK_DIM ** -0.5 * LOG2E)
    n_blocks = seq // tq

    def prepare(blk):
        rows = slice(blk * tq, (blk + 1) * tq)
        qn = qk_norm(q_ref[rows, :], q_gain)
        for m in range(2):
            q_rows = slice((2 * blk + m) * tq, (2 * blk + m + 1) * tq)
            qa_scr[q_rows, :LANES] = jnp.where(in_map[m], qn, 0.0).astype(BF16)
            qa_scr[q_rows, LANES:] = jnp.broadcast_to(ones_at_pieces, (tq, LANES)).astype(BF16)
        ka_scr[rows, :LANES] = qk_norm(k_ref[rows, :], gk_ref[...]).astype(BF16)
        ka_scr[rows, LANES:] = (pos_ref[rows, :] * pow2).astype(BF16)
        vt_scr[0:DA_V_DIM, rows] = v_ref[rows, :].astype(F32).T.astype(BF16)
        vt_scr[DA_V_DIM:, rows] = jnp.ones((ATTN_ONES_ROWS, tq), BF16)

    krel = lax.broadcasted_iota(jnp.int32, (tq, 2 * tq), 0)
    qrel = lax.broadcasted_iota(jnp.int32, (tq, 2 * tq), 1) % tq
    allowed = (krel // CHUNK) <= (qrel // CHUNK)
    diag_corr = slope * (-2.0 * jnp.maximum(krel - qrel, 0).astype(F32))

    def col_max_of(s):
        mx = s[0:8, :]
        for r0 in range(8, s.shape[0], 8):
            mx = jnp.maximum(mx, s[r0:r0 + 8, :])
        return jnp.max(mx, axis=0, keepdims=True)

    def scores(i):
        q_i = qa_scr[2 * i * tq:2 * (i + 1) * tq, :]
        diag = slice(i * tq, (i + 1) * tq)
        s = jnp.where(allowed, _dot_nt(ka_scr[diag, :], q_i) + diag_corr, MASK_VALUE)
        s_scr[i % 2, diag, :] = s
        cmax = col_max_of(s)
        if i > 0:
            s = _dot_nt(ka_scr[0:i * tq, :], q_i)
            s_scr[i % 2, 0:i * tq, :] = s
            cmax = jnp.maximum(cmax, col_max_of(s))
        return cmax

    def values(i, col_max):
        n_keys = (i + 1) * tq
        p = jnp.exp2(s_scr[i % 2, 0:n_keys, :] - col_max).astype(BF16)
        acc = _dot(vt_scr[:, 0:n_keys], p)
        ratio = acc[:DA_V_DIM] / acc[DA_V_DIM:DA_V_DIM + 1]
        o = ratio[:, :tq] - lam * ratio[:, tq:]
        o = o * lax.rsqrt(jnp.mean(o * o, axis=0, keepdims=True) + RMS_EPS) * go_ref[...] * (1.0 - lam_init)
        o_ref[i * tq:(i + 1) * tq, :] = o.T.astype(BF16)

    prepare(0)
    col_max = scores(0)
    prepare(1)
    for i in range(n_blocks):
        if i + 2 < n_blocks:
            prepare(i + 2)
        next_max = scores(i + 1) if i + 1 < n_blocks else None
        values(i, col_max)
        col_max = next_max


ATTN_ONES_ROWS = 16
ALIBI_PIECES = 4


def _alibi_position_table(seq):
    rest = np.arange(seq, dtype=np.float64) * LOG2E
    table = np.zeros((seq, LANES), np.float32)
    for p in range(ALIBI_PIECES):
        piece = rest.astype(np.float32).astype(BF16).astype(np.float32)
        table[:, p] = piece
        rest = rest - piece.astype(np.float64)
    return jnp.asarray(table)


def _diff_attn(proj3, lam_vecs, gq, gk, go_col, lam_init):
    b, s, _ = proj3.shape
    hd = 2 * DA_QK_DIM
    kern = functools.partial(_diff_attn_kernel, lam_init=lam_init, seq=s)
    est = (2 * 4 * s * hd * 2 + 4 * s * hd * 2 + (DA_V_DIM + ATTN_ONES_ROWS) * s * 2 + 4 * ATTN_TILE * s * 4
           + 8 * s * hd * 4)
    return pl.pallas_call(
        kern,
        grid=(b, DA_HEADS),
        in_specs=[
            pl.BlockSpec((4, DA_QK_DIM), lambda bi, h: (0, 0)),
            pl.BlockSpec((s, LANES), lambda bi, h: (0, 0)),
            pl.BlockSpec((None, s, hd), lambda bi, h: (bi, 0, h)),
            pl.BlockSpec((None, s, hd), lambda bi, h: (bi, 0, DA_HEADS + h)),
            pl.BlockSpec((None, s, DA_V_DIM), lambda bi, h: (bi, 0, 2 * DA_HEADS + h)),
            pl.BlockSpec((1, hd), lambda bi, h: (0, 0)),
            pl.BlockSpec((1, hd), lambda bi, h: (0, 0)),
            pl.BlockSpec((DA_V_DIM, 1), lambda bi, h: (0, 0)),
        ],
        out_specs=pl.BlockSpec((None, s, DA_V_DIM), lambda bi, h: (bi, 0, h)),
        out_shape=jax.ShapeDtypeStruct((b, s, DA_HEADS * DA_V_DIM), BF16),
        scratch_shapes=[
            pltpu.VMEM((2 * s, 2 * hd), BF16),
            pltpu.VMEM((s, 2 * hd), BF16),
            pltpu.VMEM((DA_V_DIM + ATTN_ONES_ROWS, s), BF16),
            pltpu.VMEM((2, s, 2 * ATTN_TILE), F32),
        ],
        compiler_params=pltpu.CompilerParams(
            dimension_semantics=("parallel", "parallel"), vmem_limit_bytes=_vmem_limit(est)),
        name="diff_attn",
    )(lam_vecs, _alibi_position_table(s), proj3, proj3, proj3, gq, gk, go_col)


def _mix_kernel(x_ref, pool_ref, halo_ref, mq_ref, oda_ref, mk_ref, mv_ref,
                gmix_ref, wg_ref, bg_ref, wpool_ref, pscale_ref, gmq_ref,
                wda_ref, wbp_ref, wbm_ref, wout_ref, o_ref, *, tiles_per_seq):
    tm = x_ref.shape[0]
    d = D_MODEL
    x = x_ref[...]
    hb = (_rms(x) * gmix_ref[...]).astype(BF16)

    tile_in_seq = pl.program_id(0) % tiles_per_seq
    p_cur = pool_ref[...].astype(F32)
    halo = jnp.where(tile_in_seq == 0, 0.0, halo_ref[...].astype(F32))
    ext = jnp.concatenate([halo, p_cur], axis=0)
    t_pos = tile_in_seq * tm + lax.broadcasted_iota(jnp.int32, (tm, 1), 0)
    pool_parts = []
    for g, w in enumerate(POOL_WINDOWS):
        c0 = g * POOL_GROUP_DIM
        s = ext[:, c0:c0 + POOL_GROUP_DIM]
        sh = 1
        while sh < w:
            s = s + pltpu.roll(s, sh, axis=0)
            sh *= 2
        cnt = jnp.minimum(t_pos + 1, w).astype(F32)
        dlt = (s[POOL_HALO:] / cnt - p_cur[:, c0:c0 + POOL_GROUP_DIM]).astype(BF16)
        pool_parts.append(_dot(dlt, wpool_ref[g]))
    o_pool = (jnp.concatenate(pool_parts, axis=1) * pscale_ref[...]).astype(BF16)

    mq = mq_ref[...].astype(F32)
    mem_parts = []
    for hh in range(MEM_HEADS):
        c0 = hh * MEM_HEAD_DIM
        qh = (_rms(mq[:, c0:c0 + MEM_HEAD_DIM]) * gmq_ref[...] * (MEM_HEAD_DIM ** -0.5)).astype(BF16)
        s = _dot_nt(qh, mk_ref[:, c0:c0 + MEM_HEAD_DIM])
        p = jnp.exp(s - jnp.max(s, axis=-1, keepdims=True))
        l = jnp.sum(p, axis=-1, keepdims=True)
        mem_parts.append(_dot(p.astype(BF16), mv_ref[:, c0:c0 + MEM_HEAD_DIM]) / l)
    o_mem = jnp.concatenate(mem_parts, axis=1).astype(BF16)

    def gate(k):
        return jax.nn.sigmoid(_dot(hb, wg_ref[:, k * d:(k + 1) * d]) + bg_ref[:, k * d:(k + 1) * d])

    merged = gate(0) * _dot(oda_ref[...], wda_ref[...])
    merged = merged + gate(1) * _dot(o_pool, wbp_ref[...])
    merged = merged + gate(2) * _dot(o_mem, wbm_ref[...])
    o_ref[...] = x + _dot(merged.astype(BF16), wout_ref[...])


def _mix(x, proj, o_da, mem_k, mem_v, gmix, wg, bg, wpool, pscale, gmq, wda, wbp, wbm, wout, seq):
    t, d = x.shape
    tm = ROW_TILE
    tiles_per_seq = seq // tm
    halo_blocks_per_tile = tm // POOL_HALO
    pool_col = (proj.shape[1] - POOL_WIDTH - MEM_WIDTH) // POOL_WIDTH
    mem_len = mem_k.shape[1]
    const = lambda i: (0, 0)
    weights_bytes = 2 * (wg.size + wda.size + wbp.size + wbm.size + wout.size + wpool.size)
    est = 2 * weights_bytes + 2 * (2 * tm * d * 4 + tm * (d + 2 * POOL_WIDTH) * 2) + 8 * tm * d * 4
    return pl.pallas_call(
        functools.partial(_mix_kernel, tiles_per_seq=tiles_per_seq),
        grid=(t // tm,),
        in_specs=[
            pl.BlockSpec((tm, d), lambda i: (i, 0)),
            pl.BlockSpec((tm, POOL_WIDTH), lambda i: (i, pool_col)),
            pl.BlockSpec((POOL_HALO, POOL_WIDTH),
                         lambda i: (jnp.maximum(i * halo_blocks_per_tile - 1, 0), pool_col)),
            pl.BlockSpec((tm, MEM_WIDTH), lambda i: (i, pool_col + 1)),
            pl.BlockSpec((tm, d), lambda i: (i, 0)),
            pl.BlockSpec((None, mem_len, MEM_WIDTH), lambda i: (i // tiles_per_seq, 0, 0)),
            pl.BlockSpec((None, mem_len, MEM_WIDTH), lambda i: (i // tiles_per_seq, 0, 0)),
            pl.BlockSpec((1, d), const),
            pl.BlockSpec(wg.shape, const),
            pl.BlockSpec((1, bg.shape[1]), const),
            pl.BlockSpec(wpool.shape, lambda i: (0, 0, 0)),
            pl.BlockSpec((1, POOL_WIDTH), const),
            pl.BlockSpec((1, MEM_HEAD_DIM), const),
            pl.BlockSpec(wda.shape, const),
            pl.BlockSpec(wbp.shape, const),
            pl.BlockSpec(wbm.shape, const),
            pl.BlockSpec(wout.shape, const),
        ],
        out_specs=pl.BlockSpec((tm, d), lambda i: (i, 0)),
        out_shape=jax.ShapeDtypeStruct((t, d), F32),
        compiler_params=pltpu.CompilerParams(
            dimension_semantics=("parallel",), vmem_limit_bytes=_vmem_limit(est)),
        name="mix",
    )(x, proj, proj, proj, o_da, mem_k, mem_v, gmix, wg, bg, wpool, pscale, gmq, wda, wbp, wbm, wout)


def _split3(x):
    hi = x.astype(BF16)
    r = x - hi.astype(F32)
    mid = r.astype(BF16)
    lo = (r - mid.astype(F32)).astype(BF16)
    return hi, mid, lo


MOE_TILE = 512
SEG_MAIN = 192
SEG_OVER = MOE_TILE - SEG_MAIN
SEG_ALIGN = 16
NOT_ROUTED = -1e6
UP_COL_TILE = 1792
DOWN_COL_TILE = 1024
COPY_SIZES = (512, 256, 128, 64, 32, 16)


def _max_expert_tiles(t):
    padded_rows = 2 * t + (t // MOE_TILE) * N_EXPERTS * (SEG_ALIGN - 1)
    return padded_rows // MOE_TILE + N_EXPERTS


def _row_copies(src, dst, src_row0, dst_row0, n_rows, max_rows, sem):
    pairs = []
    pos = 0
    for size in COPY_SIZES:
        if size > max_rows:
            continue
        cond = jnp.bitwise_and(n_rows, size) != 0
        cp = pltpu.make_async_copy(
            src.at[pl.ds(pl.multiple_of(src_row0 + pos, SEG_ALIGN), size), :],
            dst.at[pl.ds(pl.multiple_of(dst_row0 + pos, SEG_ALIGN), size), :], sem)
        pairs.append((cond, cp))
        pos = pos + jnp.where(cond, size, 0)
    return pairs


def _start_all(pairs):
    for cond, cp in pairs:
        pl.when(cond)(cp.start)


def _wait_all(pairs):
    for cond, cp in pairs:
        pl.when(cond)(cp.wait)


def _onehot_rows(rank_row, n_rows, row0):
    r = lax.broadcasted_iota(jnp.int32, (n_rows, rank_row.shape[1]), 0).astype(F32) + float(row0)
    return jnp.where(r == rank_row, 1.0, 0.0).astype(BF16)


def _route_kernel(x_ref, g_ref, wr_ref, rank_ref, meta_ref, cnts_ref):
    i = pl.program_id(0)
    tm = x_ref.shape[0]
    h2 = _rms(x_ref[...]) * g_ref[...]
    h_hi, h_mid, _ = _split3(h2)
    w_hi, w_mid, _ = _split3(wr_ref[...])
    logits = _dot_nt(w_hi, h_hi) + (_dot_nt(w_hi, h_mid) + _dot_nt(w_mid, h_hi))
    eidx = lax.broadcasted_iota(jnp.int32, logits.shape, 0).astype(F32)
    m1 = jnp.max(logits, axis=0, keepdims=True)
    i1 = jnp.min(jnp.where(logits == m1, eidx, float(N_EXPERTS)), axis=0, keepdims=True)
    sel1 = eidx == i1
    rest = jnp.where(sel1, -jnp.inf, logits)
    m2 = jnp.max(rest, axis=0, keepdims=True)
    i2 = jnp.min(jnp.where(rest == m2, eidx, float(N_EXPERTS)), axis=0, keepdims=True)
    sel2 = eidx == i2
    e2 = jnp.exp(m2 - m1)
    g1 = 1.0 / (1.0 + e2)
    g2 = e2 / (1.0 + e2)

    mask = jnp.where(sel1 | sel2, 1.0, 0.0)
    before = (lax.broadcasted_iota(jnp.int32, (tm, tm), 0)
              < lax.broadcasted_iota(jnp.int32, (tm, tm), 1)).astype(BF16)
    rank = _dot(mask.astype(BF16), before)
    counts = jnp.sum(mask, axis=1, keepdims=True)
    rank_m = jnp.where(mask > 0.5, rank, NOT_ROUTED)

    rk1 = jnp.where(sel1, rank, NOT_ROUTED)
    rk2 = jnp.where(sel2, rank, NOT_ROUTED)
    gates = jnp.where(eidx == 0.0, g1, jnp.where(eidx == 1.0, g2, 0.0))
    packed = jnp.concatenate([rk1, rk2, gates], axis=0)
    ident = (lax.broadcasted_iota(jnp.int32, (tm, tm), 0)
             == lax.broadcasted_iota(jnp.int32, (tm, tm), 1)).astype(BF16)
    p_hi, p_mid, p_lo = _split3(packed)
    meta_ref[...] = _dot_nt(ident, p_hi) + (_dot_nt(ident, p_mid) + _dot_nt(ident, p_lo))

    rank_ref[...] = rank_m
    for e in range(N_EXPERTS):
        cnt = counts[e, 0].astype(jnp.int32)
        cnts_ref[i * N_EXPERTS + e] = jnp.bitwise_and(cnt + (SEG_ALIGN - 1), -SEG_ALIGN)


def _route(x, g, w_router_t):
    t, d = x.shape
    tm = MOE_TILE
    nt = t // tm
    est = 2 * tm * d * 4 + 6 * tm * tm * 4 + 4 * tm * d * 4
    return pl.pallas_call(
        _route_kernel,
        grid=(nt,),
        in_specs=[
            pl.BlockSpec((tm, d), lambda i: (i, 0)),
            pl.BlockSpec((1, d), lambda i: (0, 0)),
            pl.BlockSpec((N_EXPERTS, d), lambda i: (0, 0)),
        ],
        out_specs=[
            pl.BlockSpec((N_EXPERTS, tm), lambda i: (0, i)),
            pl.BlockSpec((tm, 3 * N_EXPERTS), lambda i: (i, 0)),
            pl.BlockSpec(memory_space=pltpu.SMEM),
        ],
        out_shape=[
            jax.ShapeDtypeStruct((N_EXPERTS, t), F32),
            jax.ShapeDtypeStruct((t, 3 * N_EXPERTS), F32),
            jax.ShapeDtypeStruct((nt * N_EXPERTS,), jnp.int32),
        ],
        compiler_params=pltpu.CompilerParams(
            dimension_semantics=("arbitrary",), vmem_limit_bytes=_vmem_limit(est)),
        name="route",
    )(x, g, w_router_t)


def _scatter_kernel(offs_ref, cnts_ref, fill_ref, nv_ref, x_ref, g_ref, rank_ref, xs_hbm,
                    stage, over_stage, zero_buf, sems, over_sem, *, n_tiles):
    i = pl.program_id(0)
    hb = (_rms(x_ref[...]) * g_ref[...]).astype(BF16)
    rank_m = rank_ref[...]

    def main_copies(step, e):
        n_main = jnp.minimum(cnts_ref[step * N_EXPERTS + e], SEG_MAIN)
        return _row_copies(stage.at[e], xs_hbm, 0, offs_ref[step * N_EXPERTS + e], n_main, SEG_MAIN,
                           sems.at[e])

    @pl.when(i > 0)
    def _():
        for e in range(N_EXPERTS):
            _wait_all(main_copies(i - 1, e))

    for e in range(N_EXPERTS):
        rank_e = rank_m[e:e + 1, :]
        stage[e] = _dot(_onehot_rows(rank_e, SEG_MAIN, 0), hb).astype(BF16)
        _start_all(main_copies(i, e))
        n_over = cnts_ref[i * N_EXPERTS + e] - SEG_MAIN

        @pl.when(n_over > 0)
        def _():
            over_stage[...] = _dot(_onehot_rows(rank_e, SEG_OVER, SEG_MAIN), hb).astype(BF16)
            pairs = _row_copies(over_stage, xs_hbm, 0, offs_ref[i * N_EXPERTS + e] + SEG_MAIN, n_over,
                                SEG_OVER, over_sem)
            _start_all(pairs)
            _wait_all(pairs)

    @pl.when(i == pl.num_programs(0) - 1)
    def _():
        for e in range(N_EXPERTS):
            _wait_all(main_copies(i, e))
        zero_buf[...] = jnp.zeros_like(zero_buf)
        for e in range(N_EXPERTS):
            pairs = _row_copies(zero_buf, xs_hbm, 0, fill_ref[e], fill_ref[N_EXPERTS + e], MOE_TILE,
                                sems.at[e])
            _start_all(pairs)
            _wait_all(pairs)

        def zero_tile(tile, carry):
            dst = xs_hbm.at[pl.ds(pl.multiple_of(tile * MOE_TILE, MOE_TILE), MOE_TILE), :]
            cp = pltpu.make_async_copy(zero_buf, dst, over_sem)
            cp.start()
            cp.wait()
            return carry

        lax.fori_loop(nv_ref[0], n_tiles, zero_tile, 0)


def _scatter(x, g, rank, offs, cnts, fill, n_valid, n_tiles):
    t, d = x.shape
    tm = MOE_TILE
    est = 2 * tm * d * 4 + (N_EXPERTS * SEG_MAIN + SEG_OVER + MOE_TILE) * d * 2 + 6 * tm * d * 4
    grid_spec = pltpu.PrefetchScalarGridSpec(
        num_scalar_prefetch=4,
        grid=(t // tm,),
        in_specs=[
            pl.BlockSpec((tm, d), lambda i, *_: (i, 0)),
            pl.BlockSpec((1, d), lambda i, *_: (0, 0)),
            pl.BlockSpec((N_EXPERTS, tm), lambda i, *_: (0, i)),
        ],
        out_specs=pl.BlockSpec(memory_space=pl.ANY),
        scratch_shapes=[
            pltpu.VMEM((N_EXPERTS, SEG_MAIN, d), BF16),
            pltpu.VMEM((SEG_OVER, d), BF16),
            pltpu.VMEM((MOE_TILE, d), BF16),
            pltpu.SemaphoreType.DMA((N_EXPERTS,)),
            pltpu.SemaphoreType.DMA(()),
        ],
    )
    return pl.pallas_call(
        functools.partial(_scatter_kernel, n_tiles=n_tiles),
        grid_spec=grid_spec,
        out_shape=jax.ShapeDtypeStruct((n_tiles * MOE_TILE, d), BF16),
        compiler_params=pltpu.CompilerParams(
            dimension_semantics=("arbitrary",), vmem_limit_bytes=_vmem_limit(est)),
        name="scatter",
    )(offs, cnts, fill, n_valid, x, g, rank)


def _expert_changed(te_ref, t):
    return (t == 0) | (te_ref[t] != te_ref[jnp.maximum(t - 1, 0)])


def _expert_up_kernel(te_ref, nv_ref, x_ref, w1_ref, w3_ref, h_ref, w1b, w3b):
    t = pl.program_id(1)

    @pl.when(_expert_changed(te_ref, t))
    def _():
        w1b[...] = w1_ref[...].astype(BF16)
        w3b[...] = w3_ref[...].astype(BF16)

    @pl.when(t < nv_ref[0])
    def _():
        x = x_ref[...]
        a = _dot(x, w1b[...])
        b = _dot(x, w3b[...])
        h_ref[...] = (a * jax.nn.sigmoid(a) * b).astype(BF16)

    @pl.when(t >= nv_ref[0])
    def _():
        h_ref[...] = jnp.zeros_like(h_ref)


def _expert_up(xs, w1, w3, tile_expert, n_valid):
    rows, d = xs.shape
    _, _, f_dim = w1.shape
    tf = UP_COL_TILE
    est = 2 * (MOE_TILE * d * 2 + 2 * d * tf * 4 + MOE_TILE * tf * 2) + 2 * d * tf * 2 + 3 * MOE_TILE * tf * 4
    grid_spec = pltpu.PrefetchScalarGridSpec(
        num_scalar_prefetch=2,
        grid=(f_dim // tf, rows // MOE_TILE),
        in_specs=[
            pl.BlockSpec((MOE_TILE, d), lambda f, t, te, nv: (t, 0)),
            pl.BlockSpec((None, d, tf), lambda f, t, te, nv: (te[t], 0, f)),
            pl.BlockSpec((None, d, tf), lambda f, t, te, nv: (te[t], 0, f)),
        ],
        out_specs=pl.BlockSpec((MOE_TILE, tf), lambda f, t, te, nv: (t, f)),
        scratch_shapes=[pltpu.VMEM((d, tf), BF16), pltpu.VMEM((d, tf), BF16)],
    )
    return pl.pallas_call(
        _expert_up_kernel,
        grid_spec=grid_spec,
        out_shape=jax.ShapeDtypeStruct((rows, f_dim), BF16),
        compiler_params=pltpu.CompilerParams(
            dimension_semantics=("arbitrary", "arbitrary"), vmem_limit_bytes=_vmem_limit(est)),
        name="expert_up",
    )(tile_expert, n_valid, xs, w1, w3)


def _expert_down_kernel(te_ref, nv_ref, h_ref, w2_ref, y_ref, w2b):
    t = pl.program_id(1)

    @pl.when(_expert_changed(te_ref, t))
    def _():
        w2b[...] = w2_ref[...].astype(BF16)

    @pl.when(t < nv_ref[0])
    def _():
        y_ref[...] = _dot(h_ref[...], w2b[...]).astype(BF16)

    @pl.when(t >= nv_ref[0])
    def _():
        y_ref[...] = jnp.zeros_like(y_ref)


def _expert_down(hs, w2, tile_expert, n_valid):
    rows, f_dim = hs.shape
    d = w2.shape[2]
    tn = DOWN_COL_TILE
    est = 2 * (MOE_TILE * f_dim * 2 + f_dim * tn * 4 + MOE_TILE * tn * 2) + f_dim * tn * 2 + MOE_TILE * tn * 4
    grid_spec = pltpu.PrefetchScalarGridSpec(
        num_scalar_prefetch=2,
        grid=(d // tn, rows // MOE_TILE),
        in_specs=[
            pl.BlockSpec((MOE_TILE, f_dim), lambda n, t, te, nv: (t, 0)),
            pl.BlockSpec((None, f_dim, tn), lambda n, t, te, nv: (te[t], 0, n)),
        ],
        out_specs=pl.BlockSpec((MOE_TILE, tn), lambda n, t, te, nv: (t, n)),
        scratch_shapes=[pltpu.VMEM((f_dim, tn), BF16)],
    )
    return pl.pallas_call(
        _expert_down_kernel,
        grid_spec=grid_spec,
        out_shape=jax.ShapeDtypeStruct((rows, d), BF16),
        compiler_params=pltpu.CompilerParams(
            dimension_semantics=("arbitrary", "arbitrary"), vmem_limit_bytes=_vmem_limit(est)),
        name="expert_down",
    )(tile_expert, n_valid, hs, w2)


def _combine_kernel(offs_ref, cnts_ref, x_ref, meta_ref, y_hbm, o_ref,
                    ybuf, over_buf, over_acc, sems, over_sem):
    i = pl.program_id(0)
    tm = x_ref.shape[0]
    total_rows = y_hbm.shape[0]
    meta = meta_ref[...]
    g1 = meta[:, 2 * N_EXPERTS:2 * N_EXPERTS + 1]
    g2 = meta[:, 2 * N_EXPERTS + 1:2 * N_EXPERTS + 2]

    def window(e, seg_row0, n_rows):
        off = offs_ref[i * N_EXPERTS + e] + seg_row0
        start = jnp.minimum(off, total_rows - n_rows)
        return start, off - start

    copies = []
    for e in range(N_EXPERTS):
        start, _ = window(e, 0, SEG_MAIN)
        src = y_hbm.at[pl.ds(pl.multiple_of(start, SEG_ALIGN), SEG_MAIN), :]
        copies.append(pltpu.make_async_copy(src, ybuf.at[e], sems.at[e]))
        copies[-1].start()

    def weights(e, shift, n_cols):
        c = lax.broadcasted_iota(jnp.int32, (tm, n_cols), 1).astype(F32)
        first = meta[:, e:e + 1] + shift
        second = meta[:, N_EXPERTS + e:N_EXPERTS + e + 1] + shift
        return jnp.where(c == first, g1, jnp.where(c == second, g2, 0.0)).astype(BF16)

    out = x_ref[...]
    any_over = cnts_ref[i * N_EXPERTS] > SEG_MAIN
    for e in range(N_EXPERTS):
        _, shift = window(e, 0, SEG_MAIN)
        w_e = weights(e, shift.astype(F32), SEG_MAIN)
        copies[e].wait()
        out = out + _dot(w_e, ybuf[e])
        if e > 0:
            any_over = any_over | (cnts_ref[i * N_EXPERTS + e] > SEG_MAIN)
    o_ref[...] = out

    @pl.when(any_over)
    def _():
        over_acc[...] = jnp.zeros_like(over_acc)
        for e in range(N_EXPERTS):
            @pl.when(cnts_ref[i * N_EXPERTS + e] > SEG_MAIN)
            def _():
                start, shift = window(e, SEG_MAIN, SEG_OVER)
                src = y_hbm.at[pl.ds(pl.multiple_of(start, SEG_ALIGN), SEG_OVER), :]
                cp = pltpu.make_async_copy(src, over_buf, over_sem)
                cp.start()
                cp.wait()
                over_acc[...] += _dot(weights(e, shift.astype(F32) - float(SEG_MAIN), SEG_OVER), over_buf[...])
        o_ref[...] += over_acc[...]


def _combine(x, meta, ys, offs, cnts):
    t, d = x.shape
    tm = MOE_TILE
    est = (4 * tm * d * 4 + (N_EXPERTS * SEG_MAIN + SEG_OVER) * d * 2 + 2 * tm * d * 4
           + 2 * tm * LANES * 4 + 6 * tm * d * 4)
    grid_spec = pltpu.PrefetchScalarGridSpec(
        num_scalar_prefetch=2,
        grid=(t // tm,),
        in_specs=[
            pl.BlockSpec((tm, d), lambda i, *_: (i, 0)),
            pl.BlockSpec((tm, 3 * N_EXPERTS), lambda i, *_: (i, 0)),
            pl.BlockSpec(memory_space=pl.ANY),
        ],
        out_specs=pl.BlockSpec((tm, d), lambda i, *_: (i, 0)),
        scratch_shapes=[
            pltpu.VMEM((N_EXPERTS, SEG_MAIN, d), BF16),
            pltpu.VMEM((SEG_OVER, d), BF16),
            pltpu.VMEM((tm, d), F32),
            pltpu.SemaphoreType.DMA((N_EXPERTS,)),
            pltpu.SemaphoreType.DMA(()),
        ],
    )
    return pl.pallas_call(
        _combine_kernel,
        grid_spec=grid_spec,
        out_shape=jax.ShapeDtypeStruct((t, d), F32),
        compiler_params=pltpu.CompilerParams(
            dimension_semantics=("arbitrary",), vmem_limit_bytes=_vmem_limit(est)),
        name="combine",
    )(offs, cnts, x, meta, ys)


def _moe(x, g, w_router, w1, w3, w2):
    t, _ = x.shape
    nt = t // MOE_TILE
    n_tiles = _max_expert_tiles(t)
    rank, meta, cnts = _route(x, g, w_router.T)
    cnt2 = cnts.reshape(nt, N_EXPERTS)
    rows_e = jnp.sum(cnt2, axis=0)
    tiles_e = jnp.maximum((rows_e + MOE_TILE - 1) // MOE_TILE, 1)
    ends = jnp.cumsum(tiles_e)
    base_e = (ends - tiles_e) * MOE_TILE
    offs = (base_e[None, :] + jnp.cumsum(cnt2, axis=0) - cnt2).reshape(-1).astype(jnp.int32)
    fill = jnp.concatenate([base_e + rows_e, tiles_e * MOE_TILE - rows_e]).astype(jnp.int32)
    n_valid = ends[-1].reshape(1).astype(jnp.int32)
    tid = jnp.arange(n_tiles, dtype=jnp.int32)
    tile_expert = jnp.minimum(jnp.sum(tid[:, None] >= ends[None, :], axis=1), N_EXPERTS - 1).astype(jnp.int32)
    xs = _scatter(x, g, rank, offs, cnts, fill, n_valid, n_tiles)
    hs = _expert_up(xs, w1, w3, tile_expert, n_valid)
    ys = _expert_down(hs, w2, tile_expert, n_valid)
    return _combine(x, meta, ys, offs, cnts)


def _ffn_kernel(x_ref, g_ref, w1_ref, w3_ref, w2_ref, o_ref):
    x = x_ref[...]
    hb = (_rms(x) * g_ref[...]).astype(BF16)
    f_dim = w1_ref.shape[1]
    out = x
    for c0 in range(0, f_dim, FFN_HIDDEN_CHUNK):
        c1 = min(c0 + FFN_HIDDEN_CHUNK, f_dim)
        a = _dot(hb, w1_ref[:, c0:c1])
        b = _dot(hb, w3_ref[:, c0:c1])
        out = out + _dot((a * jax.nn.sigmoid(a) * b).astype(BF16), w2_ref[c0:c1, :])
    o_ref[...] = out


def _ffn(x, g, w1, w3, w2):
    t, d = x.shape
    f_dim = w1.shape[1]
    tm = FFN_ROW_TILE
    resident = pl.Buffered(1)
    est = 3 * d * f_dim * 2 + 4 * tm * d * 4 + 4 * tm * FFN_HIDDEN_CHUNK * 4
    return pl.pallas_call(
        _ffn_kernel,
        grid=(t // tm,),
        in_specs=[
            pl.BlockSpec((tm, d), lambda i: (i, 0)),
            pl.BlockSpec((1, d), lambda i: (0, 0)),
            pl.BlockSpec((d, f_dim), lambda i: (0, 0), pipeline_mode=resident),
            pl.BlockSpec((d, f_dim), lambda i: (0, 0), pipeline_mode=resident),
            pl.BlockSpec((f_dim, d), lambda i: (0, 0), pipeline_mode=resident),
        ],
        out_specs=pl.BlockSpec((tm, d), lambda i: (i, 0)),
        out_shape=jax.ShapeDtypeStruct((t, d), F32),
        compiler_params=pltpu.CompilerParams(
            dimension_semantics=("parallel",), vmem_limit_bytes=_vmem_limit(est)),
        name="ffn",
    )(x, g, w1, w3, w2)


def _lambda_init(layer):
    return 0.8 - 0.6 * math.exp(-0.3 * layer)


def kernel(x, mem, norm_mix, norm_mem, norm_ffn, w_in, w_gate, b_gate, da_q_norm, da_k_norm,
           da_lam_q1, da_lam_k1, da_lam_q2, da_lam_k2, da_out_norm, pool_w, pool_scale,
           mem_q_norm, mem_k_norm, w_mem_kv, w_br_da, w_br_pool, w_br_mem, w_out,
           ffn_w1, ffn_w3, ffn_w2, moe_router, moe_w1, moe_w3, moe_w2):
    b, s, d = x.shape
    depth = w_in.shape[0]
    xt = x.reshape(b * s, d)
    bf = lambda a: a.astype(BF16)
    row = lambda v: v.reshape(1, -1)
    for l in range(depth):
        lam0 = _lambda_init(l)
        proj = _in_proj(xt, row(norm_mix[l]), bf(w_in[l]))
        mem_k, mem_v = _mem_kv(mem, row(norm_mem[l]), bf(w_mem_kv[l]), row(mem_k_norm[l]))
        lam_vecs = jnp.stack([da_lam_q1[l], da_lam_k1[l], da_lam_q2[l], da_lam_k2[l]])
        o_da = _diff_attn(proj.reshape(b, s, -1), lam_vecs, row(da_q_norm[l]), row(da_k_norm[l]),
                          da_out_norm[l].reshape(-1, 1), lam0)
        xt = _mix(xt, proj, o_da.reshape(b * s, -1), mem_k, mem_v, row(norm_mix[l]), bf(w_gate[l]),
                  row(b_gate[l]), bf(pool_w[l]), row(pool_scale[l]), row(mem_q_norm[l]),
                  bf(w_br_da[l]), bf(w_br_pool[l]), bf(w_br_mem[l]), bf(w_out[l]), s)
        j = l // 2
        if l % 2 == 0:
            xt = _ffn(xt, row(norm_ffn[l]), bf(ffn_w1[j]), bf(ffn_w3[j]), bf(ffn_w2[j]))
        else:
            xt = _moe(xt, row(norm_ffn[l]), moe_router[j], moe_w1[j], moe_w3[j], moe_w2[j])
    return xt.reshape(b, s, d)
```

```python
import functools
import math

import jax
import jax.numpy as jnp
import numpy as np
from jax import lax
from jax.experimental import pallas as pl
from jax.experimental.pallas import tpu as pltpu

F32 = jnp.float32
BF16 = jnp.bfloat16

D_MODEL = 1024
CHUNK = 64
RMS_EPS = 1e-6
DA_HEADS = 8
DA_QK_DIM = 64
DA_V_DIM = 128
POOL_WINDOWS = (2, 4, 8, 16)
POOL_GROUP_DIM = 128
POOL_WIDTH = 512
MEM_HEADS = 4
MEM_HEAD_DIM = 128
MEM_WIDTH = 512
N_EXPERTS = 8
LOG2E = 1.4426950408889634
MASK_VALUE = -1e30

V7X_VMEM_BYTES = 64 * 1024 * 1024
VMEM_LIMIT_CAP = 56 * 1024 * 1024
LANES = 128

ROW_TILE = 1024
IN_PROJ_ROW_TILE = 512
ATTN_TILE = 256
POOL_HALO = 16
FFN_ROW_TILE = 512
FFN_HIDDEN_CHUNK = 1536


def _vmem_limit(estimate_bytes):
    return int(min(VMEM_LIMIT_CAP, max(32 * 1024 * 1024, 2 * estimate_bytes)))


def _rms(x):
    return x * lax.rsqrt(jnp.mean(x * x, axis=-1, keepdims=True) + RMS_EPS)


def _dot(a, b):
    return jnp.dot(a, b, preferred_element_type=F32)


def _dot_nt(a, b):
    return lax.dot_general(a, b, (((1,), (1,)), ((), ())), preferred_element_type=F32)


def _in_proj_kernel(x_ref, g_ref, w_ref, o_ref, wb_scr, *, col_tile):
    n = w_ref.shape[1]

    @pl.when(pl.program_id(0) == 0)
    def _():
        for c0 in range(0, n, col_tile):
            wb_scr[:, c0:c0 + col_tile] = w_ref[:, c0:c0 + col_tile].astype(BF16)

    hb = (_rms(x_ref[...]) * g_ref[...]).astype(BF16)
    for c0 in range(0, n, col_tile):
        o_ref[:, c0:c0 + col_tile] = _dot(hb, wb_scr[:, c0:c0 + col_tile]).astype(BF16)


def _in_proj(x, g, w_stack, layer):
    t, d = x.shape
    n = w_stack.shape[2]
    tm = IN_PROJ_ROW_TILE
    est = 2 * (tm * d * 4 + tm * n * 2) + d * n * 6 + 2 * tm * 1024 * 4
    return pl.pallas_call(
        functools.partial(_in_proj_kernel, col_tile=1024),
        grid=(t // tm,),
        in_specs=[
            pl.BlockSpec((tm, d), lambda i: (i, 0)),
            pl.BlockSpec((1, d), lambda i: (0, 0)),
            pl.BlockSpec((None, d, n), lambda i: (layer, 0, 0), pipeline_mode=pl.Buffered(1)),
        ],
        out_specs=pl.BlockSpec((tm, n), lambda i: (i, 0)),
        out_shape=jax.ShapeDtypeStruct((t, n), BF16),
        scratch_shapes=[pltpu.VMEM((d, n), BF16)],
        compiler_params=pltpu.CompilerParams(
            dimension_semantics=("arbitrary",), vmem_limit_bytes=_vmem_limit(est)),
        name="in_proj",
    )(x, g, w_stack)


def _mem_kv_kernel(mem_ref, g_ref, w_ref, gk_ref, k_ref, v_ref):
    mh = (_rms(mem_ref[...]) * g_ref[...]).astype(BF16)
    kv = _dot(mh, w_ref[...])
    for hh in range(MEM_HEADS):
        c0 = hh * MEM_HEAD_DIM
        k_ref[:, c0:c0 + MEM_HEAD_DIM] = (_rms(kv[:, c0:c0 + MEM_HEAD_DIM]) * gk_ref[...]).astype(BF16)
    v_ref[...] = kv[:, MEM_WIDTH:].astype(BF16)


def _mem_kv(mem, g, w_bf16, gk):
    b, m, d = mem.shape
    out = jax.ShapeDtypeStruct((b, m, MEM_WIDTH), BF16)
    return pl.pallas_call(
        _mem_kv_kernel,
        grid=(b,),
        in_specs=[
            pl.BlockSpec((None, m, d), lambda i: (i, 0, 0)),
            pl.BlockSpec((1, d), lambda i: (0, 0)),
            pl.BlockSpec((d, 2 * MEM_WIDTH), lambda i: (0, 0)),
            pl.BlockSpec((1, MEM_HEAD_DIM), lambda i: (0, 0)),
        ],
        out_specs=[pl.BlockSpec((None, m, MEM_WIDTH), lambda i: (i, 0, 0))] * 2,
        out_shape=[out, out],
        compiler_params=pltpu.CompilerParams(dimension_semantics=("parallel",)),
        name="mem_kv",
    )(mem, g, w_bf16, gk)


def _diff_attn_kernel(lam_ref, pos_ref, q_ref, k_ref, v_ref, gq_ref, gk_ref, go_ref, o_ref,
                      qa_scr, ka_scr, vt_scr, s_scr, *, lam_init, seq):
    tq = ATTN_TILE
    head = pl.program_id(1)
    slope = jnp.exp2(-(jnp.zeros((1, LANES), F32) + (head + 1).astype(F32)))[:, :1] * LOG2E

    lam_v = lam_ref[...]
    lam = (jnp.exp(jnp.sum(lam_v[0:1] * lam_v[1:2], axis=-1, keepdims=True))
           - jnp.exp(jnp.sum(lam_v[2:3] * lam_v[3:4], axis=-1, keepdims=True)) + lam_init)

    r_i = lax.broadcasted_iota(jnp.int32, (LANES, LANES), 0) // DA_QK_DIM
    c_i = lax.broadcasted_iota(jnp.int32, (LANES, LANES), 1) // DA_QK_DIM
    group_mean = jnp.where(r_i == c_i, 1.0 / DA_QK_DIM, 0.0).astype(BF16)

    def qk_norm(x_bf16, gain):
        xf = x_bf16.astype(F32)
        ms = _dot((xf * xf).astype(BF16), group_mean)
        return xf * (lax.rsqrt(ms + RMS_EPS) * gain)

    lane = lax.broadcasted_iota(jnp.int32, (tq, LANES), 1)
    in_map = (lane < DA_QK_DIM, lane >= DA_QK_DIM)
    lane_row = lax.broadcasted_iota(jnp.int32, (1, LANES), 1)
    ones_at_pieces = jnp.where(lane_row < ALIBI_PIECES, 1.0, 0.0)
    pow2 = jnp.exp2(-(jnp.zeros((1, LANES), F32) + (head + 1).astype(F32)))
    q_gain = gq_ref[...] * (DA_QK_DIM ** -0.5 * LOG2E)
    n_blocks = seq // tq

    def prepare(blk):
        rows = slice(blk * tq, (blk + 1) * tq)
        qn = qk_norm(q_ref[rows, :], q_gain)
        for m in range(2):
            q_rows = slice((2 * blk + m) * tq, (2 * blk + m + 1) * tq)
            qa_scr[q_rows, :LANES] = jnp.where(in_map[m], qn, 0.0).astype(BF16)
            qa_scr[q_rows, LANES:] = jnp.broadcast_to(ones_at_pieces, (tq, LANES)).astype(BF16)
        ka_scr[rows, :LANES] = qk_norm(k_ref[rows, :], gk_ref[...]).astype(BF16)
        ka_scr[rows, LANES:] = (pos_ref[rows, :] * pow2).astype(BF16)
        vt_scr[0:DA_V_DIM, rows] = v_ref[rows, :].astype(F32).T.astype(BF16)
        vt_scr[DA_V_DIM:, rows] = jnp.ones((ATTN_ONES_ROWS, tq), BF16)

    krel = lax.broadcasted_iota(jnp.int32, (tq, 2 * tq), 0)
    qrel = lax.broadcasted_iota(jnp.int32, (tq, 2 * tq), 1) % tq
    allowed = (krel // CHUNK) <= (qrel // CHUNK)
    diag_corr = slope * (-2.0 * jnp.maximum(krel - qrel, 0).astype(F32))

    def col_max_of(s):
        mx = s[0:8, :]
        for r0 in range(8, s.shape[0], 8):
            mx = jnp.maximum(mx, s[r0:r0 + 8, :])
        return jnp.max(mx, axis=0, keepdims=True)

    def scores(i):
        q_i = qa_scr[2 * i * tq:2 * (i + 1) * tq, :]
        diag = slice(i * tq, (i + 1) * tq)
        s = jnp.where(allowed, _dot_nt(ka_scr[diag, :], q_i) + diag_corr, MASK_VALUE)
        s_scr[i % ATTN_SCORE_BUFFERS, diag, :] = s
        cmax = col_max_of(s)
        if i > 0:
            s = _dot_nt(ka_scr[0:i * tq, :], q_i)
            s_scr[i % ATTN_SCORE_BUFFERS, 0:i * tq, :] = s
            cmax = jnp.maximum(cmax, col_max_of(s))
        return cmax

    def values(i, col_max):
        n_keys = (i + 1) * tq
        p = jnp.exp2(s_scr[i % ATTN_SCORE_BUFFERS, 0:n_keys, :] - col_max).astype(BF16)
        acc = _dot(vt_scr[:, 0:n_keys], p)
        ratio = acc[:DA_V_DIM] / acc[DA_V_DIM:DA_V_DIM + 1]
        o = ratio[:, :tq] - lam * ratio[:, tq:]
        o = o * lax.rsqrt(jnp.mean(o * o, axis=0, keepdims=True) + RMS_EPS) * go_ref[...] * (1.0 - lam_init)
        o_ref[i * tq:(i + 1) * tq, :] = o.T.astype(BF16)

    ahead = ATTN_SCORE_BUFFERS - 1
    col_max = {}
    prepared = [0]

    def prepare_through(blk):
        while prepared[0] <= min(blk, n_blocks - 1):
            prepare(prepared[0])
            prepared[0] += 1

    prepare_through(ATTN_PREP_LEAD - 1)
    for blk in range(min(ahead, n_blocks)):
        prepare_through(blk + ATTN_PREP_LEAD)
        col_max[blk] = scores(blk)
    for i in range(n_blocks):
        prepare_through(i + ahead + ATTN_PREP_LEAD)
        if i + ahead < n_blocks:
            col_max[i + ahead] = scores(i + ahead)
        values(i, col_max.pop(i))


ATTN_ONES_ROWS = 16
ATTN_SCORE_BUFFERS = 3
ATTN_PREP_LEAD = 2
ALIBI_PIECES = 4


def _alibi_position_table(seq):
    rest = np.arange(seq, dtype=np.float64) * LOG2E
    table = np.zeros((seq, LANES), np.float32)
    for p in range(ALIBI_PIECES):
        piece = rest.astype(np.float32).astype(BF16).astype(np.float32)
        table[:, p] = piece
        rest = rest - piece.astype(np.float64)
    return jnp.asarray(table)


def _diff_attn(proj3, lam_vecs, gq, gk, go_col, lam_init):
    b, s, _ = proj3.shape
    hd = 2 * DA_QK_DIM
    kern = functools.partial(_diff_attn_kernel, lam_init=lam_init, seq=s)
    est = (2 * 4 * s * hd * 2 + 4 * s * hd * 2 + (DA_V_DIM + ATTN_ONES_ROWS) * s * 2 + 2 * ATTN_SCORE_BUFFERS * ATTN_TILE * s * 4
           + 8 * s * hd * 4)
    return pl.pallas_call(
        kern,
        grid=(b, DA_HEADS),
        in_specs=[
            pl.BlockSpec((4, DA_QK_DIM), lambda bi, h: (0, 0)),
            pl.BlockSpec((s, LANES), lambda bi, h: (0, 0)),
            pl.BlockSpec((None, s, hd), lambda bi, h: (bi, 0, h)),
            pl.BlockSpec((None, s, hd), lambda bi, h: (bi, 0, DA_HEADS + h)),
            pl.BlockSpec((None, s, DA_V_DIM), lambda bi, h: (bi, 0, 2 * DA_HEADS + h)),
            pl.BlockSpec((1, hd), lambda bi, h: (0, 0)),
            pl.BlockSpec((1, hd), lambda bi, h: (0, 0)),
            pl.BlockSpec((DA_V_DIM, 1), lambda bi, h: (0, 0)),
        ],
        out_specs=pl.BlockSpec((None, s, DA_V_DIM), lambda bi, h: (bi, 0, h)),
        out_shape=jax.ShapeDtypeStruct((b, s, DA_HEADS * DA_V_DIM), BF16),
        scratch_shapes=[
            pltpu.VMEM((2 * s, 2 * hd), BF16),
            pltpu.VMEM((s, 2 * hd), BF16),
            pltpu.VMEM((DA_V_DIM + ATTN_ONES_ROWS, s), BF16),
            pltpu.VMEM((ATTN_SCORE_BUFFERS, s, 2 * ATTN_TILE), F32),
        ],
        compiler_params=pltpu.CompilerParams(
            dimension_semantics=("parallel", "parallel"), vmem_limit_bytes=_vmem_limit(est)),
        name="diff_attn",
    )(lam_vecs, _alibi_position_table(s), proj3, proj3, proj3, gq, gk, go_col)


def _mix_kernel(x_ref, pool_ref, halo_ref, mq_ref, oda_ref, mk_ref, mv_ref,
                gmix_ref, wg_ref, bg_ref, wpool_ref, pscale_ref, gmq_ref,
                wda_ref, wbp_ref, wbm_ref, wout_ref, o_ref, *, tiles_per_seq):
    tm = x_ref.shape[0]
    d = D_MODEL
    x = x_ref[...]
    hb = (_rms(x) * gmix_ref[...]).astype(BF16)

    tile_in_seq = pl.program_id(0) % tiles_per_seq
    p_cur = pool_ref[...].astype(F32)
    halo = jnp.where(tile_in_seq == 0, 0.0, halo_ref[...].astype(F32))
    ext = jnp.concatenate([halo, p_cur], axis=0)
    t_pos = tile_in_seq * tm + lax.broadcasted_iota(jnp.int32, (tm, 1), 0)
    pool_parts = []
    for g, w in enumerate(POOL_WINDOWS):
        c0 = g * POOL_GROUP_DIM
        s = ext[:, c0:c0 + POOL_GROUP_DIM]
        sh = 1
        while sh < w:
            s = s + pltpu.roll(s, sh, axis=0)
            sh *= 2
        cnt = jnp.minimum(t_pos + 1, w).astype(F32)
        dlt = (s[POOL_HALO:] / cnt - p_cur[:, c0:c0 + POOL_GROUP_DIM]).astype(BF16)
        pool_parts.append(_dot(dlt, wpool_ref[g]))
    o_pool = (jnp.concatenate(pool_parts, axis=1) * pscale_ref[...]).astype(BF16)

    mq = mq_ref[...].astype(F32)
    mem_parts = []
    for hh in range(MEM_HEADS):
        c0 = hh * MEM_HEAD_DIM
        qh = (_rms(mq[:, c0:c0 + MEM_HEAD_DIM]) * gmq_ref[...] * (MEM_HEAD_DIM ** -0.5)).astype(BF16)
        s = _dot_nt(qh, mk_ref[:, c0:c0 + MEM_HEAD_DIM])
        p = jnp.exp(s - jnp.max(s, axis=-1, keepdims=True))
        l = jnp.sum(p, axis=-1, keepdims=True)
        mem_parts.append(_dot(p.astype(BF16), mv_ref[:, c0:c0 + MEM_HEAD_DIM]) / l)
    o_mem = jnp.concatenate(mem_parts, axis=1).astype(BF16)

    def gate(k):
        return jax.nn.sigmoid(_dot(hb, wg_ref[:, k * d:(k + 1) * d]) + bg_ref[:, k * d:(k + 1) * d])

    merged = gate(0) * _dot(oda_ref[...], wda_ref[...])
    merged = merged + gate(1) * _dot(o_pool, wbp_ref[...])
    merged = merged + gate(2) * _dot(o_mem, wbm_ref[...])
    o_ref[...] = x + _dot(merged.astype(BF16), wout_ref[...])


def _mix(x, proj, o_da, mem_k, mem_v, gmix, wg, bg, wpool, pscale, gmq, wda, wbp, wbm, wout, seq):
    t, d = x.shape
    tm = ROW_TILE
    tiles_per_seq = seq // tm
    halo_blocks_per_tile = tm // POOL_HALO
    pool_col = (proj.shape[1] - POOL_WIDTH - MEM_WIDTH) // POOL_WIDTH
    mem_len = mem_k.shape[1]
    const = lambda i: (0, 0)
    weights_bytes = 2 * (wg.size + wda.size + wbp.size + wbm.size + wout.size + wpool.size)
    est = 2 * weights_bytes + 2 * (2 * tm * d * 4 + tm * (d + 2 * POOL_WIDTH) * 2) + 8 * tm * d * 4
    return pl.pallas_call(
        functools.partial(_mix_kernel, tiles_per_seq=tiles_per_seq),
        grid=(t // tm,),
        in_specs=[
            pl.BlockSpec((tm, d), lambda i: (i, 0)),
            pl.BlockSpec((tm, POOL_WIDTH), lambda i: (i, pool_col)),
            pl.BlockSpec((POOL_HALO, POOL_WIDTH),
                         lambda i: (jnp.maximum(i * halo_blocks_per_tile - 1, 0), pool_col)),
            pl.BlockSpec((tm, MEM_WIDTH), lambda i: (i, pool_col + 1)),
            pl.BlockSpec((tm, d), lambda i: (i, 0)),
            pl.BlockSpec((None, mem_len, MEM_WIDTH), lambda i: (i // tiles_per_seq, 0, 0)),
            pl.BlockSpec((None, mem_len, MEM_WIDTH), lambda i: (i // tiles_per_seq, 0, 0)),
            pl.BlockSpec((1, d), const),
            pl.BlockSpec(wg.shape, const, pipeline_mode=pl.Buffered(1)),
            pl.BlockSpec((1, bg.shape[1]), const),
            pl.BlockSpec(wpool.shape, lambda i: (0, 0, 0)),
            pl.BlockSpec((1, POOL_WIDTH), const),
            pl.BlockSpec((1, MEM_HEAD_DIM), const),
            pl.BlockSpec(wda.shape, const, pipeline_mode=pl.Buffered(1)),
            pl.BlockSpec(wbp.shape, const, pipeline_mode=pl.Buffered(1)),
            pl.BlockSpec(wbm.shape, const, pipeline_mode=pl.Buffered(1)),
            pl.BlockSpec(wout.shape, const, pipeline_mode=pl.Buffered(1)),
        ],
        out_specs=pl.BlockSpec((tm, d), lambda i: (i, 0)),
        out_shape=jax.ShapeDtypeStruct((t, d), F32),
        compiler_params=pltpu.CompilerParams(
            dimension_semantics=("parallel",), vmem_limit_bytes=_vmem_limit(est)),
        name="mix",
    )(x, proj, proj, proj, o_da, mem_k, mem_v, gmix, wg, bg, wpool, pscale, gmq, wda, wbp, wbm, wout)


def _split3(x):
    hi = x.astype(BF16)
    r = x - hi.astype(F32)
    mid = r.astype(BF16)
    lo = (r - mid.astype(F32)).astype(BF16)
    return hi, mid, lo


MOE_TILE = 512
SEG_MAIN = 192
SEG_OVER = MOE_TILE - SEG_MAIN
SEG_ALIGN = 16
NOT_ROUTED = -1e6
UP_COL_TILE = 1792
DOWN_COL_TILE = 1024
COPY_SIZES = (512, 256, 128, 64, 32, 16)


def _max_expert_tiles(t):
    padded_rows = 2 * t + (t // MOE_TILE) * N_EXPERTS * (SEG_ALIGN - 1)
    return padded_rows // MOE_TILE + N_EXPERTS


def _row_copies(src, dst, src_row0, dst_row0, n_rows, max_rows, sem):
    pairs = []
    pos = 0
    for size in COPY_SIZES:
        if size > max_rows:
            continue
        cond = jnp.bitwise_and(n_rows, size) != 0
        cp = pltpu.make_async_copy(
            src.at[pl.ds(pl.multiple_of(src_row0 + pos, SEG_ALIGN), size), :],
            dst.at[pl.ds(pl.multiple_of(dst_row0 + pos, SEG_ALIGN), size), :], sem)
        pairs.append((cond, cp))
        pos = pos + jnp.where(cond, size, 0)
    return pairs


def _start_all(pairs):
    for cond, cp in pairs:
        pl.when(cond)(cp.start)


def _wait_all(pairs):
    for cond, cp in pairs:
        pl.when(cond)(cp.wait)


def _onehot_rows(rank_row, n_rows, row0):
    r = lax.broadcasted_iota(jnp.int32, (n_rows, rank_row.shape[1]), 0).astype(F32) + float(row0)
    return jnp.where(r == rank_row, 1.0, 0.0).astype(BF16)


def _route_kernel(x_ref, g_ref, wr_ref, rank_ref, meta_ref, cnts_ref):
    i = pl.program_id(0)
    tm = x_ref.shape[0]
    h2 = _rms(x_ref[...]) * g_ref[...]
    h_hi, h_mid, _ = _split3(h2)
    w_hi, w_mid, _ = _split3(wr_ref[...])
    logits = _dot_nt(w_hi, h_hi) + (_dot_nt(w_hi, h_mid) + _dot_nt(w_mid, h_hi))
    eidx = lax.broadcasted_iota(jnp.int32, logits.shape, 0).astype(F32)
    m1 = jnp.max(logits, axis=0, keepdims=True)
    i1 = jnp.min(jnp.where(logits == m1, eidx, float(N_EXPERTS)), axis=0, keepdims=True)
    sel1 = eidx == i1
    rest = jnp.where(sel1, -jnp.inf, logits)
    m2 = jnp.max(rest, axis=0, keepdims=True)
    i2 = jnp.min(jnp.where(rest == m2, eidx, float(N_EXPERTS)), axis=0, keepdims=True)
    sel2 = eidx == i2
    e2 = jnp.exp(m2 - m1)
    g1 = 1.0 / (1.0 + e2)
    g2 = e2 / (1.0 + e2)

    mask = jnp.where(sel1 | sel2, 1.0, 0.0)
    before = (lax.broadcasted_iota(jnp.int32, (tm, tm), 0)
              < lax.broadcasted_iota(jnp.int32, (tm, tm), 1)).astype(BF16)
    rank = _dot(mask.astype(BF16), before)
    counts = jnp.sum(mask, axis=1, keepdims=True)
    rank_m = jnp.where(mask > 0.5, rank, NOT_ROUTED)

    rk1 = jnp.where(sel1, rank, NOT_ROUTED)
    rk2 = jnp.where(sel2, rank, NOT_ROUTED)
    gates = jnp.where(eidx == 0.0, g1, jnp.where(eidx == 1.0, g2, 0.0))
    packed = jnp.concatenate([rk1, rk2, gates], axis=0)
    ident = (lax.broadcasted_iota(jnp.int32, (tm, tm), 0)
             == lax.broadcasted_iota(jnp.int32, (tm, tm), 1)).astype(BF16)
    p_hi, p_mid, p_lo = _split3(packed)
    meta_ref[...] = _dot_nt(ident, p_hi) + (_dot_nt(ident, p_mid) + _dot_nt(ident, p_lo))

    rank_ref[...] = rank_m
    for e in range(N_EXPERTS):
        cnt = counts[e, 0].astype(jnp.int32)
        cnts_ref[i * N_EXPERTS + e] = jnp.bitwise_and(cnt + (SEG_ALIGN - 1), -SEG_ALIGN)


def _route(x, g, w_router_t):
    t, d = x.shape
    tm = MOE_TILE
    nt = t // tm
    est = 2 * tm * d * 4 + 6 * tm * tm * 4 + 4 * tm * d * 4
    return pl.pallas_call(
        _route_kernel,
        grid=(nt,),
        in_specs=[
            pl.BlockSpec((tm, d), lambda i: (i, 0)),
            pl.BlockSpec((1, d), lambda i: (0, 0)),
            pl.BlockSpec((N_EXPERTS, d), lambda i: (0, 0)),
        ],
        out_specs=[
            pl.BlockSpec((N_EXPERTS, tm), lambda i: (0, i)),
            pl.BlockSpec((tm, 3 * N_EXPERTS), lambda i: (i, 0)),
            pl.BlockSpec(memory_space=pltpu.SMEM),
        ],
        out_shape=[
            jax.ShapeDtypeStruct((N_EXPERTS, t), F32),
            jax.ShapeDtypeStruct((t, 3 * N_EXPERTS), F32),
            jax.ShapeDtypeStruct((nt * N_EXPERTS,), jnp.int32),
        ],
        compiler_params=pltpu.CompilerParams(
            dimension_semantics=("arbitrary",), vmem_limit_bytes=_vmem_limit(est)),
        name="route",
    )(x, g, w_router_t)


def _scatter_kernel(offs_ref, cnts_ref, fill_ref, nv_ref, x_ref, g_ref, rank_ref, xs_hbm,
                    stage, over_stage, zero_buf, sems, over_sem, *, n_tiles):
    i = pl.program_id(0)
    hb = (_rms(x_ref[...]) * g_ref[...]).astype(BF16)
    rank_m = rank_ref[...]

    def main_copies(step, e):
        n_main = jnp.minimum(cnts_ref[step * N_EXPERTS + e], SEG_MAIN)
        return _row_copies(stage.at[e], xs_hbm, 0, offs_ref[step * N_EXPERTS + e], n_main, SEG_MAIN,
                           sems.at[e])

    @pl.when(i > 0)
    def _():
        for e in range(N_EXPERTS):
            _wait_all(main_copies(i - 1, e))

    for e in range(N_EXPERTS):
        rank_e = rank_m[e:e + 1, :]
        stage[e] = _dot(_onehot_rows(rank_e, SEG_MAIN, 0), hb).astype(BF16)
        _start_all(main_copies(i, e))
        n_over = cnts_ref[i * N_EXPERTS + e] - SEG_MAIN

        @pl.when(n_over > 0)
        def _():
            over_stage[...] = _dot(_onehot_rows(rank_e, SEG_OVER, SEG_MAIN), hb).astype(BF16)
            pairs = _row_copies(over_stage, xs_hbm, 0, offs_ref[i * N_EXPERTS + e] + SEG_MAIN, n_over,
                                SEG_OVER, over_sem)
            _start_all(pairs)
            _wait_all(pairs)

    @pl.when(i == pl.num_programs(0) - 1)
    def _():
        for e in range(N_EXPERTS):
            _wait_all(main_copies(i, e))
        zero_buf[...] = jnp.zeros_like(zero_buf)
        for e in range(N_EXPERTS):
            pairs = _row_copies(zero_buf, xs_hbm, 0, fill_ref[e], fill_ref[N_EXPERTS + e], MOE_TILE,
                                sems.at[e])
            _start_all(pairs)
            _wait_all(pairs)

        def zero_tile(tile, carry):
            dst = xs_hbm.at[pl.ds(pl.multiple_of(tile * MOE_TILE, MOE_TILE), MOE_TILE), :]
            cp = pltpu.make_async_copy(zero_buf, dst, over_sem)
            cp.start()
            cp.wait()
            return carry

        lax.fori_loop(nv_ref[0], n_tiles, zero_tile, 0)


def _scatter(x, g, rank, offs, cnts, fill, n_valid, n_tiles):
    t, d = x.shape
    tm = MOE_TILE
    est = 2 * tm * d * 4 + (N_EXPERTS * SEG_MAIN + SEG_OVER + MOE_TILE) * d * 2 + 6 * tm * d * 4
    grid_spec = pltpu.PrefetchScalarGridSpec(
        num_scalar_prefetch=4,
        grid=(t // tm,),
        in_specs=[
            pl.BlockSpec((tm, d), lambda i, *_: (i, 0)),
            pl.BlockSpec((1, d), lambda i, *_: (0, 0)),
            pl.BlockSpec((N_EXPERTS, tm), lambda i, *_: (0, i)),
        ],
        out_specs=pl.BlockSpec(memory_space=pl.ANY),
        scratch_shapes=[
            pltpu.VMEM((N_EXPERTS, SEG_MAIN, d), BF16),
            pltpu.VMEM((SEG_OVER, d), BF16),
            pltpu.VMEM((MOE_TILE, d), BF16),
            pltpu.SemaphoreType.DMA((N_EXPERTS,)),
            pltpu.SemaphoreType.DMA(()),
        ],
    )
    return pl.pallas_call(
        functools.partial(_scatter_kernel, n_tiles=n_tiles),
        grid_spec=grid_spec,
        out_shape=jax.ShapeDtypeStruct((n_tiles * MOE_TILE, d), BF16),
        compiler_params=pltpu.CompilerParams(
            dimension_semantics=("arbitrary",), vmem_limit_bytes=_vmem_limit(est)),
        name="scatter",
    )(offs, cnts, fill, n_valid, x, g, rank)


def _expert_changed(te_ref, t):
    return (t == 0) | (te_ref[t] != te_ref[jnp.maximum(t - 1, 0)])


def _expert_up_kernel(te_ref, nv_ref, x_ref, w1_ref, w3_ref, h_ref, w1b, w3b):
    t = pl.program_id(1)

    @pl.when(_expert_changed(te_ref, t))
    def _():
        w1b[...] = w1_ref[...].astype(BF16)
        w3b[...] = w3_ref[...].astype(BF16)

    @pl.when(t < nv_ref[0])
    def _():
        x = x_ref[...]
        a = _dot(x, w1b[...])
        b = _dot(x, w3b[...])
        h_ref[...] = (a * jax.nn.sigmoid(a) * b).astype(BF16)

    @pl.when(t >= nv_ref[0])
    def _():
        h_ref[...] = jnp.zeros_like(h_ref)


def _expert_up(xs, w1, w3, tile_expert, n_valid):
    rows, d = xs.shape
    _, _, f_dim = w1.shape
    tf = UP_COL_TILE
    est = 2 * (MOE_TILE * d * 2 + 2 * d * tf * 4 + MOE_TILE * tf * 2) + 2 * d * tf * 2 + 3 * MOE_TILE * tf * 4
    grid_spec = pltpu.PrefetchScalarGridSpec(
        num_scalar_prefetch=2,
        grid=(f_dim // tf, rows // MOE_TILE),
        in_specs=[
            pl.BlockSpec((MOE_TILE, d), lambda f, t, te, nv: (t, 0)),
            pl.BlockSpec((None, d, tf), lambda f, t, te, nv: (te[t], 0, f)),
            pl.BlockSpec((None, d, tf), lambda f, t, te, nv: (te[t], 0, f)),
        ],
        out_specs=pl.BlockSpec((MOE_TILE, tf), lambda f, t, te, nv: (t, f)),
        scratch_shapes=[pltpu.VMEM((d, tf), BF16), pltpu.VMEM((d, tf), BF16)],
    )
    return pl.pallas_call(
        _expert_up_kernel,
        grid_spec=grid_spec,
        out_shape=jax.ShapeDtypeStruct((rows, f_dim), BF16),
        compiler_params=pltpu.CompilerParams(
            dimension_semantics=("arbitrary", "arbitrary"), vmem_limit_bytes=_vmem_limit(est)),
        name="expert_up",
    )(tile_expert, n_valid, xs, w1, w3)


def _expert_down_kernel(te_ref, nv_ref, h_ref, w2_ref, y_ref, w2b):
    t = pl.program_id(1)

    @pl.when(_expert_changed(te_ref, t))
    def _():
        w2b[...] = w2_ref[...].astype(BF16)

    @pl.when(t < nv_ref[0])
    def _():
        y_ref[...] = _dot(h_ref[...], w2b[...]).astype(BF16)

    @pl.when(t >= nv_ref[0])
    def _():
        y_ref[...] = jnp.zeros_like(y_ref)


def _expert_down(hs, w2, tile_expert, n_valid):
    rows, f_dim = hs.shape
    d = w2.shape[2]
    tn = DOWN_COL_TILE
    est = 2 * (MOE_TILE * f_dim * 2 + f_dim * tn * 4 + MOE_TILE * tn * 2) + f_dim * tn * 2 + MOE_TILE * tn * 4
    grid_spec = pltpu.PrefetchScalarGridSpec(
        num_scalar_prefetch=2,
        grid=(d // tn, rows // MOE_TILE),
        in_specs=[
            pl.BlockSpec((MOE_TILE, f_dim), lambda n, t, te, nv: (t, 0)),
            pl.BlockSpec((None, f_dim, tn), lambda n, t, te, nv: (te[t], 0, n)),
        ],
        out_specs=pl.BlockSpec((MOE_TILE, tn), lambda n, t, te, nv: (t, n)),
        scratch_shapes=[pltpu.VMEM((f_dim, tn), BF16)],
    )
    return pl.pallas_call(
        _expert_down_kernel,
        grid_spec=grid_spec,
        out_shape=jax.ShapeDtypeStruct((rows, d), BF16),
        compiler_params=pltpu.CompilerParams(
            dimension_semantics=("arbitrary", "arbitrary"), vmem_limit_bytes=_vmem_limit(est)),
        name="expert_down",
    )(tile_expert, n_valid, hs, w2)


def _combine_kernel(offs_ref, cnts_ref, x_ref, meta_ref, y_hbm, o_ref,
                    ybuf, over_buf, over_acc, sems, over_sem):
    i = pl.program_id(0)
    tm = x_ref.shape[0]
    total_rows = y_hbm.shape[0]
    meta = meta_ref[...]
    g1 = meta[:, 2 * N_EXPERTS:2 * N_EXPERTS + 1]
    g2 = meta[:, 2 * N_EXPERTS + 1:2 * N_EXPERTS + 2]

    def window(e, seg_row0, n_rows):
        off = offs_ref[i * N_EXPERTS + e] + seg_row0
        start = jnp.minimum(off, total_rows - n_rows)
        return start, off - start

    copies = []
    for e in range(N_EXPERTS):
        start, _ = window(e, 0, SEG_MAIN)
        src = y_hbm.at[pl.ds(pl.multiple_of(start, SEG_ALIGN), SEG_MAIN), :]
        copies.append(pltpu.make_async_copy(src, ybuf.at[e], sems.at[e]))
        copies[-1].start()

    def weights(e, shift, n_cols):
        c = lax.broadcasted_iota(jnp.int32, (tm, n_cols), 1).astype(F32)
        first = meta[:, e:e + 1] + shift
        second = meta[:, N_EXPERTS + e:N_EXPERTS + e + 1] + shift
        return jnp.where(c == first, g1, jnp.where(c == second, g2, 0.0)).astype(BF16)

    out = x_ref[...]
    any_over = cnts_ref[i * N_EXPERTS] > SEG_MAIN
    for e in range(N_EXPERTS):
        _, shift = window(e, 0, SEG_MAIN)
        w_e = weights(e, shift.astype(F32), SEG_MAIN)
        copies[e].wait()
        out = out + _dot(w_e, ybuf[e])
        if e > 0:
            any_over = any_over | (cnts_ref[i * N_EXPERTS + e] > SEG_MAIN)
    o_ref[...] = out

    @pl.when(any_over)
    def _():
        over_acc[...] = jnp.zeros_like(over_acc)
        for e in range(N_EXPERTS):
            @pl.when(cnts_ref[i * N_EXPERTS + e] > SEG_MAIN)
            def _():
                start, shift = window(e, SEG_MAIN, SEG_OVER)
                src = y_hbm.at[pl.ds(pl.multiple_of(start, SEG_ALIGN), SEG_OVER), :]
                cp = pltpu.make_async_copy(src, over_buf, over_sem)
                cp.start()
                cp.wait()
                over_acc[...] += _dot(weights(e, shift.astype(F32) - float(SEG_MAIN), SEG_OVER), over_buf[...])
        o_ref[...] += over_acc[...]


def _combine(x, meta, ys, offs, cnts):
    t, d = x.shape
    tm = MOE_TILE
    est = (4 * tm * d * 4 + (N_EXPERTS * SEG_MAIN + SEG_OVER) * d * 2 + 2 * tm * d * 4
           + 2 * tm * LANES * 4 + 6 * tm * d * 4)
    grid_spec = pltpu.PrefetchScalarGridSpec(
        num_scalar_prefetch=2,
        grid=(t // tm,),
        in_specs=[
            pl.BlockSpec((tm, d), lambda i, *_: (i, 0)),
            pl.BlockSpec((tm, 3 * N_EXPERTS), lambda i, *_: (i, 0)),
            pl.BlockSpec(memory_space=pl.ANY),
        ],
        out_specs=pl.BlockSpec((tm, d), lambda i, *_: (i, 0)),
        scratch_shapes=[
            pltpu.VMEM((N_EXPERTS, SEG_MAIN, d), BF16),
            pltpu.VMEM((SEG_OVER, d), BF16),
            pltpu.VMEM((tm, d), F32),
            pltpu.SemaphoreType.DMA((N_EXPERTS,)),
            pltpu.SemaphoreType.DMA(()),
        ],
    )
    return pl.pallas_call(
        _combine_kernel,
        grid_spec=grid_spec,
        out_shape=jax.ShapeDtypeStruct((t, d), F32),
        compiler_params=pltpu.CompilerParams(
            dimension_semantics=("arbitrary",), vmem_limit_bytes=_vmem_limit(est)),
        name="combine",
    )(offs, cnts, x, meta, ys)


def _moe(x, g, w_router, w1, w3, w2):
    t, _ = x.shape
    nt = t // MOE_TILE
    n_tiles = _max_expert_tiles(t)
    rank, meta, cnts = _route(x, g, w_router.T)
    cnt2 = cnts.reshape(nt, N_EXPERTS)
    rows_e = jnp.sum(cnt2, axis=0)
    tiles_e = jnp.maximum((rows_e + MOE_TILE - 1) // MOE_TILE, 1)
    ends = jnp.cumsum(tiles_e)
    base_e = (ends - tiles_e) * MOE_TILE
    offs = (base_e[None, :] + jnp.cumsum(cnt2, axis=0) - cnt2).reshape(-1).astype(jnp.int32)
    fill = jnp.concatenate([base_e + rows_e, tiles_e * MOE_TILE - rows_e]).astype(jnp.int32)
    n_valid = ends[-1].reshape(1).astype(jnp.int32)
    tid = jnp.arange(n_tiles, dtype=jnp.int32)
    tile_expert = jnp.minimum(jnp.sum(tid[:, None] >= ends[None, :], axis=1), N_EXPERTS - 1).astype(jnp.int32)
    xs = _scatter(x, g, rank, offs, cnts, fill, n_valid, n_tiles)
    hs = _expert_up(xs, w1, w3, tile_expert, n_valid)
    ys = _expert_down(hs, w2, tile_expert, n_valid)
    return _combine(x, meta, ys, offs, cnts)


def _ffn_kernel(x_ref, g_ref, w1_ref, w3_ref, w2_ref, o_ref):
    x = x_ref[...]
    hb = (_rms(x) * g_ref[...]).astype(BF16)
    f_dim = w1_ref.shape[1]
    out = x
    for c0 in range(0, f_dim, FFN_HIDDEN_CHUNK):
        c1 = min(c0 + FFN_HIDDEN_CHUNK, f_dim)
        a = _dot(hb, w1_ref[:, c0:c1])
        b = _dot(hb, w3_ref[:, c0:c1])
        out = out + _dot((a * jax.nn.sigmoid(a) * b).astype(BF16), w2_ref[c0:c1, :])
    o_ref[...] = out


def _ffn(x, g, w1, w3, w2):
    t, d = x.shape
    f_dim = w1.shape[1]
    tm = FFN_ROW_TILE
    resident = pl.Buffered(1)
    est = 3 * d * f_dim * 2 + 4 * tm * d * 4 + 4 * tm * FFN_HIDDEN_CHUNK * 4
    return pl.pallas_call(
        _ffn_kernel,
        grid=(t // tm,),
        in_specs=[
            pl.BlockSpec((tm, d), lambda i: (i, 0)),
            pl.BlockSpec((1, d), lambda i: (0, 0)),
            pl.BlockSpec((d, f_dim), lambda i: (0, 0), pipeline_mode=resident),
            pl.BlockSpec((d, f_dim), lambda i: (0, 0), pipeline_mode=resident),
            pl.BlockSpec((f_dim, d), lambda i: (0, 0), pipeline_mode=resident),
        ],
        out_specs=pl.BlockSpec((tm, d), lambda i: (i, 0)),
        out_shape=jax.ShapeDtypeStruct((t, d), F32),
        compiler_params=pltpu.CompilerParams(
            dimension_semantics=("parallel",), vmem_limit_bytes=_vmem_limit(est)),
        name="ffn",
    )(x, g, w1, w3, w2)


def _lambda_init(layer):
    return 0.8 - 0.6 * math.exp(-0.3 * layer)


def kernel(x, mem, norm_mix, norm_mem, norm_ffn, w_in, w_gate, b_gate, da_q_norm, da_k_norm,
           da_lam_q1, da_lam_k1, da_lam_q2, da_lam_k2, da_out_norm, pool_w, pool_scale,
           mem_q_norm, mem_k_norm, w_mem_kv, w_br_da, w_br_pool, w_br_mem, w_out,
           ffn_w1, ffn_w3, ffn_w2, moe_router, moe_w1, moe_w3, moe_w2):
    b, s, d = x.shape
    depth = w_in.shape[0]
    xt = x.reshape(b * s, d)
    bf = lambda a: a.astype(BF16)
    row = lambda v: v.reshape(1, -1)
    for l in range(depth):
        lam0 = _lambda_init(l)
        proj = _in_proj(xt, row(norm_mix[l]), w_in, l)
        mem_k, mem_v = _mem_kv(mem, row(norm_mem[l]), bf(w_mem_kv[l]), row(mem_k_norm[l]))
        lam_vecs = jnp.stack([da_lam_q1[l], da_lam_k1[l], da_lam_q2[l], da_lam_k2[l]])
        o_da = _diff_attn(proj.reshape(b, s, -1), lam_vecs, row(da_q_norm[l]), row(da_k_norm[l]),
                          da_out_norm[l].reshape(-1, 1), lam0)
        xt = _mix(xt, proj, o_da.reshape(b * s, -1), mem_k, mem_v, row(norm_mix[l]), bf(w_gate[l]),
                  row(b_gate[l]), bf(pool_w[l]), row(pool_scale[l]), row(mem_q_norm[l]),
                  bf(w_br_da[l]), bf(w_br_pool[l]), bf(w_br_mem[l]), bf(w_out[l]), s)
        j = l // 2
        if l % 2 == 0:
            xt = _ffn(xt, row(norm_ffn[l]), bf(ffn_w1[j]), bf(ffn_w3[j]), bf(ffn_w2[j]))
        else:
            xt = _moe(xt, row(norm_ffn[l]), moe_router[j], moe_w1[j], moe_w3[j], moe_w2[j])
    return xt.reshape(b, s, d)
```

```python
import functools
import math

import jax
import jax.numpy as jnp
import numpy as np
from jax import lax
from jax.experimental import pallas as pl
from jax.experimental.pallas import tpu as pltpu

F32 = jnp.float32
BF16 = jnp.bfloat16

D_MODEL = 1024
CHUNK = 64
RMS_EPS = 1e-6
DA_HEADS = 8
DA_QK_DIM = 64
DA_V_DIM = 128
POOL_WINDOWS = (2, 4, 8, 16)
POOL_GROUP_DIM = 128
POOL_WIDTH = 512
MEM_HEADS = 4
MEM_HEAD_DIM = 128
MEM_WIDTH = 512
N_EXPERTS = 8
LOG2E = 1.4426950408889634
MASK_VALUE = -1e30

V7X_VMEM_BYTES = 64 * 1024 * 1024
VMEM_LIMIT_CAP = 56 * 1024 * 1024
LANES = 128

ROW_TILE = 1024
IN_PROJ_ROW_TILE = 512
ATTN_TILE = 256
POOL_HALO = 16
FFN_ROW_TILE = 512
FFN_HIDDEN_CHUNK = 1536


def _vmem_limit(estimate_bytes):
    return int(min(VMEM_LIMIT_CAP, max(32 * 1024 * 1024, 2 * estimate_bytes)))


def _rms(x):
    return x * lax.rsqrt(jnp.mean(x * x, axis=-1, keepdims=True) + RMS_EPS)


def _dot(a, b):
    return jnp.dot(a, b, preferred_element_type=F32)


def _dot_nt(a, b):
    return lax.dot_general(a, b, (((1,), (1,)), ((), ())), preferred_element_type=F32)


def _in_proj_kernel(x_ref, g_ref, w_ref, o_ref, wb_scr, *, col_tile):
    n = w_ref.shape[1]

    @pl.when(pl.program_id(0) == 0)
    def _():
        for c0 in range(0, n, col_tile):
            wb_scr[:, c0:c0 + col_tile] = w_ref[:, c0:c0 + col_tile].astype(BF16)

    hb = (_rms(x_ref[...]) * g_ref[...]).astype(BF16)
    for c0 in range(0, n, col_tile):
        o_ref[:, c0:c0 + col_tile] = _dot(hb, wb_scr[:, c0:c0 + col_tile]).astype(BF16)


def _in_proj(x, g, w_stack, layer):
    t, d = x.shape
    n = w_stack.shape[2]
    tm = IN_PROJ_ROW_TILE
    est = 2 * (tm * d * 4 + tm * n * 2) + d * n * 6 + 2 * tm * 1024 * 4
    return pl.pallas_call(
        functools.partial(_in_proj_kernel, col_tile=1024),
        grid=(t // tm,),
        in_specs=[
            pl.BlockSpec((tm, d), lambda i: (i, 0)),
            pl.BlockSpec((1, d), lambda i: (0, 0)),
            pl.BlockSpec((None, d, n), lambda i: (layer, 0, 0), pipeline_mode=pl.Buffered(1)),
        ],
        out_specs=pl.BlockSpec((tm, n), lambda i: (i, 0)),
        out_shape=jax.ShapeDtypeStruct((t, n), BF16),
        scratch_shapes=[pltpu.VMEM((d, n), BF16)],
        compiler_params=pltpu.CompilerParams(
            dimension_semantics=("arbitrary",), vmem_limit_bytes=_vmem_limit(est)),
        name="in_proj",
    )(x, g, w_stack)


def _mem_kv_kernel(mem_ref, g_ref, w_ref, gk_ref, k_ref, v_ref):
    mh = (_rms(mem_ref[...]) * g_ref[...]).astype(BF16)
    kv = _dot(mh, w_ref[...])
    for hh in range(MEM_HEADS):
        c0 = hh * MEM_HEAD_DIM
        k_ref[:, c0:c0 + MEM_HEAD_DIM] = (_rms(kv[:, c0:c0 + MEM_HEAD_DIM]) * gk_ref[...]).astype(BF16)
    v_ref[...] = kv[:, MEM_WIDTH:].astype(BF16)


def _mem_kv(mem, g, w_bf16, gk):
    b, m, d = mem.shape
    out = jax.ShapeDtypeStruct((b, m, MEM_WIDTH), BF16)
    return pl.pallas_call(
        _mem_kv_kernel,
        grid=(b,),
        in_specs=[
            pl.BlockSpec((None, m, d), lambda i: (i, 0, 0)),
            pl.BlockSpec((1, d), lambda i: (0, 0)),
            pl.BlockSpec((d, 2 * MEM_WIDTH), lambda i: (0, 0)),
            pl.BlockSpec((1, MEM_HEAD_DIM), lambda i: (0, 0)),
        ],
        out_specs=[pl.BlockSpec((None, m, MEM_WIDTH), lambda i: (i, 0, 0))] * 2,
        out_shape=[out, out],
        compiler_params=pltpu.CompilerParams(dimension_semantics=("parallel",)),
        name="mem_kv",
    )(mem, g, w_bf16, gk)


def _diff_attn_kernel(lam_ref, pos_ref, q_ref, k_ref, v_ref, gq_ref, gk_ref, go_ref, o_ref,
                      qa_scr, ka_scr, vt_scr, s_scr, *, lam_init, seq):
    tq = ATTN_TILE
    head = pl.program_id(1)
    slope = jnp.exp2(-(jnp.zeros((1, LANES), F32) + (head + 1).astype(F32)))[:, :1] * LOG2E

    lam_v = lam_ref[...]
    lam = (jnp.exp(jnp.sum(lam_v[0:1] * lam_v[1:2], axis=-1, keepdims=True))
           - jnp.exp(jnp.sum(lam_v[2:3] * lam_v[3:4], axis=-1, keepdims=True)) + lam_init)

    r_i = lax.broadcasted_iota(jnp.int32, (LANES, LANES), 0) // DA_QK_DIM
    c_i = lax.broadcasted_iota(jnp.int32, (LANES, LANES), 1) // DA_QK_DIM
    group_mean = jnp.where(r_i == c_i, 1.0 / DA_QK_DIM, 0.0).astype(BF16)

    def qk_norm(x_bf16, gain):
        xf = x_bf16.astype(F32)
        ms = _dot((xf * xf).astype(BF16), group_mean)
        return xf * (lax.rsqrt(ms + RMS_EPS) * gain)

    lane = lax.broadcasted_iota(jnp.int32, (tq, LANES), 1)
    in_map = (lane < DA_QK_DIM, lane >= DA_QK_DIM)
    lane_row = lax.broadcasted_iota(jnp.int32, (1, LANES), 1)
    ones_at_pieces = jnp.where(lane_row < ALIBI_PIECES, 1.0, 0.0)
    pow2 = jnp.exp2(-(jnp.zeros((1, LANES), F32) + (head + 1).astype(F32)))
    q_gain = gq_ref[...] * (DA_QK_DIM ** -0.5 * LOG2E)
    n_blocks = seq // tq

    def prepare(blk):
        rows = slice(blk * tq, (blk + 1) * tq)
        qn = qk_norm(q_ref[rows, :], q_gain)
        for m in range(2):
            q_rows = slice((2 * blk + m) * tq, (2 * blk + m + 1) * tq)
            qa_scr[q_rows, :LANES] = jnp.where(in_map[m], qn, 0.0).astype(BF16)
            qa_scr[q_rows, LANES:] = jnp.broadcast_to(ones_at_pieces, (tq, LANES)).astype(BF16)
        ka_scr[rows, :LANES] = qk_norm(k_ref[rows, :], gk_ref[...]).astype(BF16)
        ka_scr[rows, LANES:] = (pos_ref[rows, :] * pow2).astype(BF16)
        vt_scr[0:DA_V_DIM, rows] = v_ref[rows, :].astype(F32).T.astype(BF16)
        vt_scr[DA_V_DIM:, rows] = jnp.ones((ATTN_ONES_ROWS, tq), BF16)

    krel = lax.broadcasted_iota(jnp.int32, (tq, 2 * tq), 0)
    qrel = lax.broadcasted_iota(jnp.int32, (tq, 2 * tq), 1) % tq
    allowed = (krel // CHUNK) <= (qrel // CHUNK)
    diag_corr = slope * (-2.0 * jnp.maximum(krel - qrel, 0).astype(F32))

    def col_max_of(s):
        mx = s[0:8, :]
        for r0 in range(8, s.shape[0], 8):
            mx = jnp.maximum(mx, s[r0:r0 + 8, :])
        return jnp.max(mx, axis=0, keepdims=True)

    def scores(i):
        q_i = qa_scr[2 * i * tq:2 * (i + 1) * tq, :]
        diag = slice(i * tq, (i + 1) * tq)
        s = jnp.where(allowed, _dot_nt(ka_scr[diag, :], q_i) + diag_corr, MASK_VALUE)
        s_scr[i % ATTN_SCORE_BUFFERS, diag, :] = s
        cmax = col_max_of(s)
        if i > 0:
            s = _dot_nt(ka_scr[0:i * tq, :], q_i)
            s_scr[i % ATTN_SCORE_BUFFERS, 0:i * tq, :] = s
            cmax = jnp.maximum(cmax, col_max_of(s))
        return cmax

    def values(i, col_max):
        n_keys = (i + 1) * tq
        p = jnp.exp2((s_scr[i % ATTN_SCORE_BUFFERS, 0:n_keys, :] - col_max).astype(BF16))
        acc = _dot(vt_scr[:, 0:n_keys], p)
        ratio = acc[:DA_V_DIM] / acc[DA_V_DIM:DA_V_DIM + 1]
        o = ratio[:, :tq] - lam * ratio[:, tq:]
        o = o * lax.rsqrt(jnp.mean(o * o, axis=0, keepdims=True) + RMS_EPS) * go_ref[...] * (1.0 - lam_init)
        o_ref[i * tq:(i + 1) * tq, :] = o.T.astype(BF16)

    ahead = ATTN_SCORE_BUFFERS - 1
    col_max = {}
    prepared = [0]

    def prepare_through(blk):
        while prepared[0] <= min(blk, n_blocks - 1):
            prepare(prepared[0])
            prepared[0] += 1

    prepare_through(ATTN_PREP_LEAD - 1)
    for blk in range(min(ahead, n_blocks)):
        prepare_through(blk + ATTN_PREP_LEAD)
        col_max[blk] = scores(blk)
    for i in range(n_blocks):
        prepare_through(i + ahead + ATTN_PREP_LEAD)
        if i + ahead < n_blocks:
            col_max[i + ahead] = scores(i + ahead)
        values(i, col_max.pop(i))


ATTN_ONES_ROWS = 16
ATTN_SCORE_BUFFERS = 3
ATTN_PREP_LEAD = 2
ALIBI_PIECES = 4


def _alibi_position_table(seq):
    rest = np.arange(seq, dtype=np.float64) * LOG2E
    table = np.zeros((seq, LANES), np.float32)
    for p in range(ALIBI_PIECES):
        piece = rest.astype(np.float32).astype(BF16).astype(np.float32)
        table[:, p] = piece
        rest = rest - piece.astype(np.float64)
    return jnp.asarray(table)


def _diff_attn(proj3, lam_vecs, gq, gk, go_col, lam_init):
    b, s, _ = proj3.shape
    hd = 2 * DA_QK_DIM
    kern = functools.partial(_diff_attn_kernel, lam_init=lam_init, seq=s)
    est = (2 * 4 * s * hd * 2 + 4 * s * hd * 2 + (DA_V_DIM + ATTN_ONES_ROWS) * s * 2 + 2 * ATTN_SCORE_BUFFERS * ATTN_TILE * s * 4
           + 8 * s * hd * 4)
    return pl.pallas_call(
        kern,
        grid=(b, DA_HEADS),
        in_specs=[
            pl.BlockSpec((4, DA_QK_DIM), lambda bi, h: (0, 0)),
            pl.BlockSpec((s, LANES), lambda bi, h: (0, 0)),
            pl.BlockSpec((None, s, hd), lambda bi, h: (bi, 0, h)),
            pl.BlockSpec((None, s, hd), lambda bi, h: (bi, 0, DA_HEADS + h)),
            pl.BlockSpec((None, s, DA_V_DIM), lambda bi, h: (bi, 0, 2 * DA_HEADS + h)),
            pl.BlockSpec((1, hd), lambda bi, h: (0, 0)),
            pl.BlockSpec((1, hd), lambda bi, h: (0, 0)),
            pl.BlockSpec((DA_V_DIM, 1), lambda bi, h: (0, 0)),
        ],
        out_specs=pl.BlockSpec((None, s, DA_V_DIM), lambda bi, h: (bi, 0, h)),
        out_shape=jax.ShapeDtypeStruct((b, s, DA_HEADS * DA_V_DIM), BF16),
        scratch_shapes=[
            pltpu.VMEM((2 * s, 2 * hd), BF16),
            pltpu.VMEM((s, 2 * hd), BF16),
            pltpu.VMEM((DA_V_DIM + ATTN_ONES_ROWS, s), BF16),
            pltpu.VMEM((ATTN_SCORE_BUFFERS, s, 2 * ATTN_TILE), F32),
        ],
        compiler_params=pltpu.CompilerParams(
            dimension_semantics=("parallel", "parallel"), vmem_limit_bytes=_vmem_limit(est)),
        name="diff_attn",
    )(lam_vecs, _alibi_position_table(s), proj3, proj3, proj3, gq, gk, go_col)


def _mix_kernel(x_ref, pool_ref, halo_ref, mq_ref, oda_ref, mk_ref, mv_ref,
                gmix_ref, wg_ref, bg_ref, wpool_ref, pscale_ref, gmq_ref,
                wda_ref, wbp_ref, wbm_ref, wout_ref, o_ref, *, tiles_per_seq):
    tm = x_ref.shape[0]
    d = D_MODEL
    x = x_ref[...]
    hb = (_rms(x) * gmix_ref[...]).astype(BF16)

    tile_in_seq = pl.program_id(0) % tiles_per_seq
    p_cur = pool_ref[...].astype(F32)
    halo = jnp.where(tile_in_seq == 0, 0.0, halo_ref[...].astype(F32))
    ext = jnp.concatenate([halo, p_cur], axis=0)
    t_pos = tile_in_seq * tm + lax.broadcasted_iota(jnp.int32, (tm, 1), 0)
    pool_parts = []
    for g, w in enumerate(POOL_WINDOWS):
        c0 = g * POOL_GROUP_DIM
        s = ext[:, c0:c0 + POOL_GROUP_DIM]
        sh = 1
        while sh < w:
            s = s + pltpu.roll(s, sh, axis=0)
            sh *= 2
        cnt = jnp.minimum(t_pos + 1, w).astype(F32)
        dlt = (s[POOL_HALO:] / cnt - p_cur[:, c0:c0 + POOL_GROUP_DIM]).astype(BF16)
        pool_parts.append(_dot(dlt, wpool_ref[g]))
    o_pool = (jnp.concatenate(pool_parts, axis=1) * pscale_ref[...]).astype(BF16)

    mq = mq_ref[...].astype(F32)
    mem_parts = []
    for hh in range(MEM_HEADS):
        c0 = hh * MEM_HEAD_DIM
        qh = (_rms(mq[:, c0:c0 + MEM_HEAD_DIM]) * gmq_ref[...] * (MEM_HEAD_DIM ** -0.5)).astype(BF16)
        s = _dot_nt(qh, mk_ref[:, c0:c0 + MEM_HEAD_DIM])
        p = jnp.exp(s - jnp.max(s, axis=-1, keepdims=True))
        l = jnp.sum(p, axis=-1, keepdims=True)
        mem_parts.append(_dot(p.astype(BF16), mv_ref[:, c0:c0 + MEM_HEAD_DIM]) / l)
    o_mem = jnp.concatenate(mem_parts, axis=1).astype(BF16)

    def gate(k):
        return jax.nn.sigmoid(_dot(hb, wg_ref[:, k * d:(k + 1) * d]) + bg_ref[:, k * d:(k + 1) * d])

    merged = gate(0) * _dot(oda_ref[...], wda_ref[...])
    merged = merged + gate(1) * _dot(o_pool, wbp_ref[...])
    merged = merged + gate(2) * _dot(o_mem, wbm_ref[...])
    o_ref[...] = x + _dot(merged.astype(BF16), wout_ref[...])


def _mix(x, proj, o_da, mem_k, mem_v, gmix, wg, bg, wpool, pscale, gmq, wda, wbp, wbm, wout, seq):
    t, d = x.shape
    tm = ROW_TILE
    tiles_per_seq = seq // tm
    halo_blocks_per_tile = tm // POOL_HALO
    pool_col = (proj.shape[1] - POOL_WIDTH - MEM_WIDTH) // POOL_WIDTH
    mem_len = mem_k.shape[1]
    const = lambda i: (0, 0)
    weights_bytes = 2 * (wg.size + wda.size + wbp.size + wbm.size + wout.size + wpool.size)
    est = 2 * weights_bytes + 2 * (2 * tm * d * 4 + tm * (d + 2 * POOL_WIDTH) * 2) + 8 * tm * d * 4
    return pl.pallas_call(
        functools.partial(_mix_kernel, tiles_per_seq=tiles_per_seq),
        grid=(t // tm,),
        in_specs=[
            pl.BlockSpec((tm, d), lambda i: (i, 0)),
            pl.BlockSpec((tm, POOL_WIDTH), lambda i: (i, pool_col)),
            pl.BlockSpec((POOL_HALO, POOL_WIDTH),
                         lambda i: (jnp.maximum(i * halo_blocks_per_tile - 1, 0), pool_col)),
            pl.BlockSpec((tm, MEM_WIDTH), lambda i: (i, pool_col + 1)),
            pl.BlockSpec((tm, d), lambda i: (i, 0)),
            pl.BlockSpec((None, mem_len, MEM_WIDTH), lambda i: (i // tiles_per_seq, 0, 0)),
            pl.BlockSpec((None, mem_len, MEM_WIDTH), lambda i: (i // tiles_per_seq, 0, 0)),
            pl.BlockSpec((1, d), const),
            pl.BlockSpec(wg.shape, const, pipeline_mode=pl.Buffered(1)),
            pl.BlockSpec((1, bg.shape[1]), const),
            pl.BlockSpec(wpool.shape, lambda i: (0, 0, 0)),
            pl.BlockSpec((1, POOL_WIDTH), const),
            pl.BlockSpec((1, MEM_HEAD_DIM), const),
            pl.BlockSpec(wda.shape, const, pipeline_mode=pl.Buffered(1)),
            pl.BlockSpec(wbp.shape, const, pipeline_mode=pl.Buffered(1)),
            pl.BlockSpec(wbm.shape, const, pipeline_mode=pl.Buffered(1)),
            pl.BlockSpec(wout.shape, const, pipeline_mode=pl.Buffered(1)),
        ],
        out_specs=pl.BlockSpec((tm, d), lambda i: (i, 0)),
        out_shape=jax.ShapeDtypeStruct((t, d), F32),
        compiler_params=pltpu.CompilerParams(
            dimension_semantics=("parallel",), vmem_limit_bytes=_vmem_limit(est)),
        name="mix",
    )(x, proj, proj, proj, o_da, mem_k, mem_v, gmix, wg, bg, wpool, pscale, gmq, wda, wbp, wbm, wout)


def _split3(x):
    hi = x.astype(BF16)
    r = x - hi.astype(F32)
    mid = r.astype(BF16)
    lo = (r - mid.astype(F32)).astype(BF16)
    return hi, mid, lo


MOE_TILE = 512
SEG_MAIN = 192
SEG_OVER = MOE_TILE - SEG_MAIN
SEG_ALIGN = 16
NOT_ROUTED = -1e6
UP_COL_TILE = 1792
DOWN_COL_TILE = 1024
COPY_SIZES = (512, 256, 128, 64, 32, 16)


def _max_expert_tiles(t):
    padded_rows = 2 * t + (t // MOE_TILE) * N_EXPERTS * (SEG_ALIGN - 1)
    return padded_rows // MOE_TILE + N_EXPERTS


def _row_copies(src, dst, src_row0, dst_row0, n_rows, max_rows, sem):
    pairs = []
    pos = 0
    for size in COPY_SIZES:
        if size > max_rows:
            continue
        cond = jnp.bitwise_and(n_rows, size) != 0
        cp = pltpu.make_async_copy(
            src.at[pl.ds(pl.multiple_of(src_row0 + pos, SEG_ALIGN), size), :],
            dst.at[pl.ds(pl.multiple_of(dst_row0 + pos, SEG_ALIGN), size), :], sem)
        pairs.append((cond, cp))
        pos = pos + jnp.where(cond, size, 0)
    return pairs


def _start_all(pairs):
    for cond, cp in pairs:
        pl.when(cond)(cp.start)


def _wait_all(pairs):
    for cond, cp in pairs:
        pl.when(cond)(cp.wait)


def _onehot_rows(rank_row, n_rows, row0):
    r = lax.broadcasted_iota(jnp.int32, (n_rows, rank_row.shape[1]), 0).astype(F32) + float(row0)
    return jnp.where(r == rank_row, 1.0, 0.0).astype(BF16)


def _route_kernel(x_ref, g_ref, wr_ref, rank_ref, meta_ref, cnts_ref):
    i = pl.program_id(0)
    tm = x_ref.shape[0]
    h2 = _rms(x_ref[...]) * g_ref[...]
    h_hi, h_mid, _ = _split3(h2)
    w_hi, w_mid, _ = _split3(wr_ref[...])
    logits = _dot_nt(w_hi, h_hi) + (_dot_nt(w_hi, h_mid) + _dot_nt(w_mid, h_hi))
    eidx = lax.broadcasted_iota(jnp.int32, logits.shape, 0).astype(F32)
    m1 = jnp.max(logits, axis=0, keepdims=True)
    i1 = jnp.min(jnp.where(logits == m1, eidx, float(N_EXPERTS)), axis=0, keepdims=True)
    sel1 = eidx == i1
    rest = jnp.where(sel1, -jnp.inf, logits)
    m2 = jnp.max(rest, axis=0, keepdims=True)
    i2 = jnp.min(jnp.where(rest == m2, eidx, float(N_EXPERTS)), axis=0, keepdims=True)
    sel2 = eidx == i2
    e2 = jnp.exp(m2 - m1)
    g1 = 1.0 / (1.0 + e2)
    g2 = e2 / (1.0 + e2)

    mask = jnp.where(sel1 | sel2, 1.0, 0.0)
    before = (lax.broadcasted_iota(jnp.int32, (tm, tm), 0)
              < lax.broadcasted_iota(jnp.int32, (tm, tm), 1)).astype(BF16)
    rank = _dot(mask.astype(BF16), before)
    counts = jnp.sum(mask, axis=1, keepdims=True)
    rank_m = jnp.where(mask > 0.5, rank, NOT_ROUTED)

    rk1 = jnp.where(sel1, rank, NOT_ROUTED)
    rk2 = jnp.where(sel2, rank, NOT_ROUTED)
    gates = jnp.where(eidx == 0.0, g1, jnp.where(eidx == 1.0, g2, 0.0))
    packed = jnp.concatenate([rk1, rk2, gates], axis=0)
    ident = (lax.broadcasted_iota(jnp.int32, (tm, tm), 0)
             == lax.broadcasted_iota(jnp.int32, (tm, tm), 1)).astype(BF16)
    p_hi, p_mid, p_lo = _split3(packed)
    meta_ref[...] = _dot_nt(ident, p_hi) + (_dot_nt(ident, p_mid) + _dot_nt(ident, p_lo))

    rank_ref[...] = rank_m
    for e in range(N_EXPERTS):
        cnt = counts[e, 0].astype(jnp.int32)
        cnts_ref[i * N_EXPERTS + e] = jnp.bitwise_and(cnt + (SEG_ALIGN - 1), -SEG_ALIGN)


def _route(x, g, w_router_t):
    t, d = x.shape
    tm = MOE_TILE
    nt = t // tm
    est = 2 * tm * d * 4 + 6 * tm * tm * 4 + 4 * tm * d * 4
    return pl.pallas_call(
        _route_kernel,
        grid=(nt,),
        in_specs=[
            pl.BlockSpec((tm, d), lambda i: (i, 0)),
            pl.BlockSpec((1, d), lambda i: (0, 0)),
            pl.BlockSpec((N_EXPERTS, d), lambda i: (0, 0)),
        ],
        out_specs=[
            pl.BlockSpec((N_EXPERTS, tm), lambda i: (0, i)),
            pl.BlockSpec((tm, 3 * N_EXPERTS), lambda i: (i, 0)),
            pl.BlockSpec(memory_space=pltpu.SMEM),
        ],
        out_shape=[
            jax.ShapeDtypeStruct((N_EXPERTS, t), F32),
            jax.ShapeDtypeStruct((t, 3 * N_EXPERTS), F32),
            jax.ShapeDtypeStruct((nt * N_EXPERTS,), jnp.int32),
        ],
        compiler_params=pltpu.CompilerParams(
            dimension_semantics=("arbitrary",), vmem_limit_bytes=_vmem_limit(est)),
        name="route",
    )(x, g, w_router_t)


def _scatter_kernel(offs_ref, cnts_ref, fill_ref, nv_ref, x_ref, g_ref, rank_ref, xs_hbm,
                    stage, over_stage, zero_buf, sems, over_sem, *, n_tiles):
    i = pl.program_id(0)
    hb = (_rms(x_ref[...]) * g_ref[...]).astype(BF16)
    rank_m = rank_ref[...]

    def main_copies(step, e):
        n_main = jnp.minimum(cnts_ref[step * N_EXPERTS + e], SEG_MAIN)
        return _row_copies(stage.at[e], xs_hbm, 0, offs_ref[step * N_EXPERTS + e], n_main, SEG_MAIN,
                           sems.at[e])

    @pl.when(i > 0)
    def _():
        for e in range(N_EXPERTS):
            _wait_all(main_copies(i - 1, e))

    for e in range(N_EXPERTS):
        rank_e = rank_m[e:e + 1, :]
        stage[e] = _dot(_onehot_rows(rank_e, SEG_MAIN, 0), hb).astype(BF16)
        _start_all(main_copies(i, e))
        n_over = cnts_ref[i * N_EXPERTS + e] - SEG_MAIN

        @pl.when(n_over > 0)
        def _():
            over_stage[...] = _dot(_onehot_rows(rank_e, SEG_OVER, SEG_MAIN), hb).astype(BF16)
            pairs = _row_copies(over_stage, xs_hbm, 0, offs_ref[i * N_EXPERTS + e] + SEG_MAIN, n_over,
                                SEG_OVER, over_sem)
            _start_all(pairs)
            _wait_all(pairs)

    @pl.when(i == pl.num_programs(0) - 1)
    def _():
        for e in range(N_EXPERTS):
            _wait_all(main_copies(i, e))
        zero_buf[...] = jnp.zeros_like(zero_buf)
        for e in range(N_EXPERTS):
            pairs = _row_copies(zero_buf, xs_hbm, 0, fill_ref[e], fill_ref[N_EXPERTS + e], MOE_TILE,
                                sems.at[e])
            _start_all(pairs)
            _wait_all(pairs)

        def zero_tile(tile, carry):
            dst = xs_hbm.at[pl.ds(pl.multiple_of(tile * MOE_TILE, MOE_TILE), MOE_TILE), :]
            cp = pltpu.make_async_copy(zero_buf, dst, over_sem)
            cp.start()
            cp.wait()
            return carry

        lax.fori_loop(nv_ref[0], n_tiles, zero_tile, 0)


def _scatter(x, g, rank, offs, cnts, fill, n_valid, n_tiles):
    t, d = x.shape
    tm = MOE_TILE
    est = 2 * tm * d * 4 + (N_EXPERTS * SEG_MAIN + SEG_OVER + MOE_TILE) * d * 2 + 6 * tm * d * 4
    grid_spec = pltpu.PrefetchScalarGridSpec(
        num_scalar_prefetch=4,
        grid=(t // tm,),
        in_specs=[
            pl.BlockSpec((tm, d), lambda i, *_: (i, 0)),
            pl.BlockSpec((1, d), lambda i, *_: (0, 0)),
            pl.BlockSpec((N_EXPERTS, tm), lambda i, *_: (0, i)),
        ],
        out_specs=pl.BlockSpec(memory_space=pl.ANY),
        scratch_shapes=[
            pltpu.VMEM((N_EXPERTS, SEG_MAIN, d), BF16),
            pltpu.VMEM((SEG_OVER, d), BF16),
            pltpu.VMEM((MOE_TILE, d), BF16),
            pltpu.SemaphoreType.DMA((N_EXPERTS,)),
            pltpu.SemaphoreType.DMA(()),
        ],
    )
    return pl.pallas_call(
        functools.partial(_scatter_kernel, n_tiles=n_tiles),
        grid_spec=grid_spec,
        out_shape=jax.ShapeDtypeStruct((n_tiles * MOE_TILE, d), BF16),
        compiler_params=pltpu.CompilerParams(
            dimension_semantics=("arbitrary",), vmem_limit_bytes=_vmem_limit(est)),
        name="scatter",
    )(offs, cnts, fill, n_valid, x, g, rank)


def _expert_changed(te_ref, t):
    return (t == 0) | (te_ref[t] != te_ref[jnp.maximum(t - 1, 0)])


def _expert_up_kernel(te_ref, nv_ref, x_ref, w1_ref, w3_ref, h_ref, w1b, w3b):
    t = pl.program_id(1)

    @pl.when(_expert_changed(te_ref, t))
    def _():
        w1b[...] = w1_ref[...].astype(BF16)
        w3b[...] = w3_ref[...].astype(BF16)

    @pl.when(t < nv_ref[0])
    def _():
        x = x_ref[...]
        a = _dot(x, w1b[...])
        b = _dot(x, w3b[...])
        h_ref[...] = (a * jax.nn.sigmoid(a) * b).astype(BF16)

    @pl.when(t >= nv_ref[0])
    def _():
        h_ref[...] = jnp.zeros_like(h_ref)


def _expert_up(xs, w1, w3, tile_expert, n_valid):
    rows, d = xs.shape
    _, _, f_dim = w1.shape
    tf = UP_COL_TILE
    est = 2 * (MOE_TILE * d * 2 + 2 * d * tf * 4 + MOE_TILE * tf * 2) + 2 * d * tf * 2 + 3 * MOE_TILE * tf * 4
    grid_spec = pltpu.PrefetchScalarGridSpec(
        num_scalar_prefetch=2,
        grid=(f_dim // tf, rows // MOE_TILE),
        in_specs=[
            pl.BlockSpec((MOE_TILE, d), lambda f, t, te, nv: (t, 0)),
            pl.BlockSpec((None, d, tf), lambda f, t, te, nv: (te[t], 0, f)),
            pl.BlockSpec((None, d, tf), lambda f, t, te, nv: (te[t], 0, f)),
        ],
        out_specs=pl.BlockSpec((MOE_TILE, tf), lambda f, t, te, nv: (t, f)),
        scratch_shapes=[pltpu.VMEM((d, tf), BF16), pltpu.VMEM((d, tf), BF16)],
    )
    return pl.pallas_call(
        _expert_up_kernel,
        grid_spec=grid_spec,
        out_shape=jax.ShapeDtypeStruct((rows, f_dim), BF16),
        compiler_params=pltpu.CompilerParams(
            dimension_semantics=("arbitrary", "arbitrary"), vmem_limit_bytes=_vmem_limit(est)),
        name="expert_up",
    )(tile_expert, n_valid, xs, w1, w3)


def _expert_down_kernel(te_ref, nv_ref, h_ref, w2_ref, y_ref, w2b):
    t = pl.program_id(1)

    @pl.when(_expert_changed(te_ref, t))
    def _():
        w2b[...] = w2_ref[...].astype(BF16)

    @pl.when(t < nv_ref[0])
    def _():
        y_ref[...] = _dot(h_ref[...], w2b[...]).astype(BF16)

    @pl.when(t >= nv_ref[0])
    def _():
        y_ref[...] = jnp.zeros_like(y_ref)


def _expert_down(hs, w2, tile_expert, n_valid):
    rows, f_dim = hs.shape
    d = w2.shape[2]
    tn = DOWN_COL_TILE
    est = 2 * (MOE_TILE * f_dim * 2 + f_dim * tn * 4 + MOE_TILE * tn * 2) + f_dim * tn * 2 + MOE_TILE * tn * 4
    grid_spec = pltpu.PrefetchScalarGridSpec(
        num_scalar_prefetch=2,
        grid=(d // tn, rows // MOE_TILE),
        in_specs=[
            pl.BlockSpec((MOE_TILE, f_dim), lambda n, t, te, nv: (t, 0)),
            pl.BlockSpec((None, f_dim, tn), lambda n, t, te, nv: (te[t], 0, n)),
        ],
        out_specs=pl.BlockSpec((MOE_TILE, tn), lambda n, t, te, nv: (t, n)),
        scratch_shapes=[pltpu.VMEM((f_dim, tn), BF16)],
    )
    return pl.pallas_call(
        _expert_down_kernel,
        grid_spec=grid_spec,
        out_shape=jax.ShapeDtypeStruct((rows, d), BF16),
        compiler_params=pltpu.CompilerParams(
            dimension_semantics=("arbitrary", "arbitrary"), vmem_limit_bytes=_vmem_limit(est)),
        name="expert_down",
    )(tile_expert, n_valid, hs, w2)


def _combine_kernel(offs_ref, cnts_ref, x_ref, meta_ref, y_hbm, o_ref,
                    ybuf, over_buf, over_acc, sems, over_sem):
    i = pl.program_id(0)
    tm = x_ref.shape[0]
    total_rows = y_hbm.shape[0]
    meta = meta_ref[...]
    g1 = meta[:, 2 * N_EXPERTS:2 * N_EXPERTS + 1]
    g2 = meta[:, 2 * N_EXPERTS + 1:2 * N_EXPERTS + 2]

    def window(e, seg_row0, n_rows):
        off = offs_ref[i * N_EXPERTS + e] + seg_row0
        start = jnp.minimum(off, total_rows - n_rows)
        return start, off - start

    copies = []
    for e in range(N_EXPERTS):
        start, _ = window(e, 0, SEG_MAIN)
        src = y_hbm.at[pl.ds(pl.multiple_of(start, SEG_ALIGN), SEG_MAIN), :]
        copies.append(pltpu.make_async_copy(src, ybuf.at[e], sems.at[e]))
        copies[-1].start()

    def weights(e, shift, n_cols):
        c = lax.broadcasted_iota(jnp.int32, (tm, n_cols), 1).astype(F32)
        first = meta[:, e:e + 1] + shift
        second = meta[:, N_EXPERTS + e:N_EXPERTS + e + 1] + shift
        return jnp.where(c == first, g1, jnp.where(c == second, g2, 0.0)).astype(BF16)

    out = x_ref[...]
    any_over = cnts_ref[i * N_EXPERTS] > SEG_MAIN
    for e in range(N_EXPERTS):
        _, shift = window(e, 0, SEG_MAIN)
        w_e = weights(e, shift.astype(F32), SEG_MAIN)
        copies[e].wait()
        out = out + _dot(w_e, ybuf[e])
        if e > 0:
            any_over = any_over | (cnts_ref[i * N_EXPERTS + e] > SEG_MAIN)
    o_ref[...] = out

    @pl.when(any_over)
    def _():
        over_acc[...] = jnp.zeros_like(over_acc)
        for e in range(N_EXPERTS):
            @pl.when(cnts_ref[i * N_EXPERTS + e] > SEG_MAIN)
            def _():
                start, shift = window(e, SEG_MAIN, SEG_OVER)
                src = y_hbm.at[pl.ds(pl.multiple_of(start, SEG_ALIGN), SEG_OVER), :]
                cp = pltpu.make_async_copy(src, over_buf, over_sem)
                cp.start()
                cp.wait()
                over_acc[...] += _dot(weights(e, shift.astype(F32) - float(SEG_MAIN), SEG_OVER), over_buf[...])
        o_ref[...] += over_acc[...]


def _combine(x, meta, ys, offs, cnts):
    t, d = x.shape
    tm = MOE_TILE
    est = (4 * tm * d * 4 + (N_EXPERTS * SEG_MAIN + SEG_OVER) * d * 2 + 2 * tm * d * 4
           + 2 * tm * LANES * 4 + 6 * tm * d * 4)
    grid_spec = pltpu.PrefetchScalarGridSpec(
        num_scalar_prefetch=2,
        grid=(t // tm,),
        in_specs=[
            pl.BlockSpec((tm, d), lambda i, *_: (i, 0)),
            pl.BlockSpec((tm, 3 * N_EXPERTS), lambda i, *_: (i, 0)),
            pl.BlockSpec(memory_space=pl.ANY),
        ],
        out_specs=pl.BlockSpec((tm, d), lambda i, *_: (i, 0)),
        scratch_shapes=[
            pltpu.VMEM((N_EXPERTS, SEG_MAIN, d), BF16),
            pltpu.VMEM((SEG_OVER, d), BF16),
            pltpu.VMEM((tm, d), F32),
            pltpu.SemaphoreType.DMA((N_EXPERTS,)),
            pltpu.SemaphoreType.DMA(()),
        ],
    )
    return pl.pallas_call(
        _combine_kernel,
        grid_spec=grid_spec,
        out_shape=jax.ShapeDtypeStruct((t, d), F32),
        compiler_params=pltpu.CompilerParams(
            dimension_semantics=("arbitrary",), vmem_limit_bytes=_vmem_limit(est)),
        name="combine",
    )(offs, cnts, x, meta, ys)


def _moe(x, g, w_router, w1, w3, w2):
    t, _ = x.shape
    nt = t // MOE_TILE
    n_tiles = _max_expert_tiles(t)
    rank, meta, cnts = _route(x, g, w_router.T)
    cnt2 = cnts.reshape(nt, N_EXPERTS)
    rows_e = jnp.sum(cnt2, axis=0)
    tiles_e = jnp.maximum((rows_e + MOE_TILE - 1) // MOE_TILE, 1)
    ends = jnp.cumsum(tiles_e)
    base_e = (ends - tiles_e) * MOE_TILE
    offs = (base_e[None, :] + jnp.cumsum(cnt2, axis=0) - cnt2).reshape(-1).astype(jnp.int32)
    fill = jnp.concatenate([base_e + rows_e, tiles_e * MOE_TILE - rows_e]).astype(jnp.int32)
    n_valid = ends[-1].reshape(1).astype(jnp.int32)
    tid = jnp.arange(n_tiles, dtype=jnp.int32)
    tile_expert = jnp.minimum(jnp.sum(tid[:, None] >= ends[None, :], axis=1), N_EXPERTS - 1).astype(jnp.int32)
    xs = _scatter(x, g, rank, offs, cnts, fill, n_valid, n_tiles)
    hs = _expert_up(xs, w1, w3, tile_expert, n_valid)
    ys = _expert_down(hs, w2, tile_expert, n_valid)
    return _combine(x, meta, ys, offs, cnts)


def _ffn_kernel(x_ref, g_ref, w1_ref, w3_ref, w2_ref, o_ref):
    x = x_ref[...]
    hb = (_rms(x) * g_ref[...]).astype(BF16)
    f_dim = w1_ref.shape[1]
    out = x
    for c0 in range(0, f_dim, FFN_HIDDEN_CHUNK):
        c1 = min(c0 + FFN_HIDDEN_CHUNK, f_dim)
        a = _dot(hb, w1_ref[:, c0:c1])
        b = _dot(hb, w3_ref[:, c0:c1])
        out = out + _dot((a * jax.nn.sigmoid(a) * b).astype(BF16), w2_ref[c0:c1, :])
    o_ref[...] = out


def _ffn(x, g, w1, w3, w2):
    t, d = x.shape
    f_dim = w1.shape[1]
    tm = FFN_ROW_TILE
    resident = pl.Buffered(1)
    est = 3 * d * f_dim * 2 + 4 * tm * d * 4 + 4 * tm * FFN_HIDDEN_CHUNK * 4
    return pl.pallas_call(
        _ffn_kernel,
        grid=(t // tm,),
        in_specs=[
            pl.BlockSpec((tm, d), lambda i: (i, 0)),
            pl.BlockSpec((1, d), lambda i: (0, 0)),
            pl.BlockSpec((d, f_dim), lambda i: (0, 0), pipeline_mode=resident),
            pl.BlockSpec((d, f_dim), lambda i: (0, 0), pipeline_mode=resident),
            pl.BlockSpec((f_dim, d), lambda i: (0, 0), pipeline_mode=resident),
        ],
        out_specs=pl.BlockSpec((tm, d), lambda i: (i, 0)),
        out_shape=jax.ShapeDtypeStruct((t, d), F32),
        compiler_params=pltpu.CompilerParams(
            dimension_semantics=("parallel",), vmem_limit_bytes=_vmem_limit(est)),
        name="ffn",
    )(x, g, w1, w3, w2)


def _lambda_init(layer):
    return 0.8 - 0.6 * math.exp(-0.3 * layer)


def kernel(x, mem, norm_mix, norm_mem, norm_ffn, w_in, w_gate, b_gate, da_q_norm, da_k_norm,
           da_lam_q1, da_lam_k1, da_lam_q2, da_lam_k2, da_out_norm, pool_w, pool_scale,
           mem_q_norm, mem_k_norm, w_mem_kv, w_br_da, w_br_pool, w_br_mem, w_out,
           ffn_w1, ffn_w3, ffn_w2, moe_router, moe_w1, moe_w3, moe_w2):
    b, s, d = x.shape
    depth = w_in.shape[0]
    xt = x.reshape(b * s, d)
    bf = lambda a: a.astype(BF16)
    row = lambda v: v.reshape(1, -1)
    for l in range(depth):
        lam0 = _lambda_init(l)
        proj = _in_proj(xt, row(norm_mix[l]), w_in, l)
        mem_k, mem_v = _mem_kv(mem, row(norm_mem[l]), bf(w_mem_kv[l]), row(mem_k_norm[l]))
        lam_vecs = jnp.stack([da_lam_q1[l], da_lam_k1[l], da_lam_q2[l], da_lam_k2[l]])
        o_da = _diff_attn(proj.reshape(b, s, -1), lam_vecs, row(da_q_norm[l]), row(da_k_norm[l]),
                          da_out_norm[l].reshape(-1, 1), lam0)
        xt = _mix(xt, proj, o_da.reshape(b * s, -1), mem_k, mem_v, row(norm_mix[l]), bf(w_gate[l]),
                  row(b_gate[l]), bf(pool_w[l]), row(pool_scale[l]), row(mem_q_norm[l]),
                  bf(w_br_da[l]), bf(w_br_pool[l]), bf(w_br_mem[l]), bf(w_out[l]), s)
        j = l // 2
        if l % 2 == 0:
            xt = _ffn(xt, row(norm_ffn[l]), bf(ffn_w1[j]), bf(ffn_w3[j]), bf(ffn_w2[j]))
        else:
            xt = _moe(xt, row(norm_ffn[l]), moe_router[j], moe_w1[j], moe_w3[j], moe_w2[j])
    return xt.reshape(b, s, d)
```

```python
import functools
import math

import jax
import jax.numpy as jnp
import numpy as np
from jax import lax
from jax.experimental import pallas as pl
from jax.experimental.pallas import tpu as pltpu

F32 = jnp.float32
BF16 = jnp.bfloat16

D_MODEL = 1024
CHUNK = 64
RMS_EPS = 1e-6
DA_HEADS = 8
DA_QK_DIM = 64
DA_V_DIM = 128
POOL_WINDOWS = (2, 4, 8, 16)
POOL_GROUP_DIM = 128
POOL_WIDTH = 512
MEM_HEADS = 4
MEM_HEAD_DIM = 128
MEM_WIDTH = 512
N_EXPERTS = 8
LOG2E = 1.4426950408889634
MASK_VALUE = -1e30

V7X_VMEM_BYTES = 64 * 1024 * 1024
VMEM_LIMIT_CAP = 56 * 1024 * 1024
LANES = 128

ROW_TILE = 1024
IN_PROJ_ROW_TILE = 512
ATTN_TILE = 256
POOL_HALO = 16
FFN_ROW_TILE = 512
FFN_HIDDEN_CHUNK = 1536


def _vmem_limit(estimate_bytes):
    return int(min(VMEM_LIMIT_CAP, max(32 * 1024 * 1024, 2 * estimate_bytes)))


def _rms(x):
    return x * lax.rsqrt(jnp.mean(x * x, axis=-1, keepdims=True) + RMS_EPS)


def _dot(a, b):
    return jnp.dot(a, b, preferred_element_type=F32)


def _dot_nt(a, b):
    return lax.dot_general(a, b, (((1,), (1,)), ((), ())), preferred_element_type=F32)


def _in_proj_kernel(x_ref, g_ref, w_ref, o_ref, wb_scr, *, col_tile):
    n = w_ref.shape[1]

    @pl.when(pl.program_id(0) == 0)
    def _():
        for c0 in range(0, n, col_tile):
            wb_scr[:, c0:c0 + col_tile] = w_ref[:, c0:c0 + col_tile].astype(BF16)

    hb = (_rms(x_ref[...]) * g_ref[...]).astype(BF16)
    for c0 in range(0, n, col_tile):
        o_ref[:, c0:c0 + col_tile] = _dot(hb, wb_scr[:, c0:c0 + col_tile]).astype(BF16)


def _in_proj(x, g, w_stack, layer):
    t, d = x.shape
    n = w_stack.shape[2]
    tm = IN_PROJ_ROW_TILE
    est = 2 * (tm * d * 4 + tm * n * 2) + d * n * 6 + 2 * tm * 1024 * 4
    return pl.pallas_call(
        functools.partial(_in_proj_kernel, col_tile=1024),
        grid=(t // tm,),
        in_specs=[
            pl.BlockSpec((tm, d), lambda i: (i, 0)),
            pl.BlockSpec((1, d), lambda i: (0, 0)),
            pl.BlockSpec((None, d, n), lambda i: (layer, 0, 0), pipeline_mode=pl.Buffered(1)),
        ],
        out_specs=pl.BlockSpec((tm, n), lambda i: (i, 0)),
        out_shape=jax.ShapeDtypeStruct((t, n), BF16),
        scratch_shapes=[pltpu.VMEM((d, n), BF16)],
        compiler_params=pltpu.CompilerParams(
            dimension_semantics=("arbitrary",), vmem_limit_bytes=_vmem_limit(est)),
        name="in_proj",
    )(x, g, w_stack)


def _mem_kv_kernel(mem_ref, g_ref, w_ref, gk_ref, k_ref, v_ref):
    mh = (_rms(mem_ref[...]) * g_ref[...]).astype(BF16)
    kv = _dot(mh, w_ref[...])
    for hh in range(MEM_HEADS):
        c0 = hh * MEM_HEAD_DIM
        k_ref[:, c0:c0 + MEM_HEAD_DIM] = (_rms(kv[:, c0:c0 + MEM_HEAD_DIM]) * gk_ref[...]).astype(BF16)
    v_ref[...] = kv[:, MEM_WIDTH:].astype(BF16)


def _mem_kv(mem, g, w_bf16, gk):
    b, m, d = mem.shape
    out = jax.ShapeDtypeStruct((b, m, MEM_WIDTH), BF16)
    return pl.pallas_call(
        _mem_kv_kernel,
        grid=(b,),
        in_specs=[
            pl.BlockSpec((None, m, d), lambda i: (i, 0, 0)),
            pl.BlockSpec((1, d), lambda i: (0, 0)),
            pl.BlockSpec((d, 2 * MEM_WIDTH), lambda i: (0, 0)),
            pl.BlockSpec((1, MEM_HEAD_DIM), lambda i: (0, 0)),
        ],
        out_specs=[pl.BlockSpec((None, m, MEM_WIDTH), lambda i: (i, 0, 0))] * 2,
        out_shape=[out, out],
        compiler_params=pltpu.CompilerParams(dimension_semantics=("parallel",)),
        name="mem_kv",
    )(mem, g, w_bf16, gk)


def _diff_attn_kernel(lam_ref, pos_ref, q_ref, k_ref, v_ref, gq_ref, gk_ref, go_ref, o_ref,
                      qa_scr, ka_scr, vt_scr, s_scr, *, lam_init, seq):
    tq = ATTN_TILE
    head = pl.program_id(1)
    slope = jnp.exp2(-(jnp.zeros((1, LANES), F32) + (head + 1).astype(F32)))[:, :1] * LOG2E

    lam_v = lam_ref[...]
    lam = (jnp.exp(jnp.sum(lam_v[0:1] * lam_v[1:2], axis=-1, keepdims=True))
           - jnp.exp(jnp.sum(lam_v[2:3] * lam_v[3:4], axis=-1, keepdims=True)) + lam_init)

    r_i = lax.broadcasted_iota(jnp.int32, (LANES, LANES), 0) // DA_QK_DIM
    c_i = lax.broadcasted_iota(jnp.int32, (LANES, LANES), 1) // DA_QK_DIM
    group_mean = jnp.where(r_i == c_i, 1.0 / DA_QK_DIM, 0.0).astype(BF16)

    def qk_norm(x_bf16, gain):
        xf = x_bf16.astype(F32)
        ms = _dot((xf * xf).astype(BF16), group_mean)
        return xf * (lax.rsqrt(ms + RMS_EPS) * gain)

    lane = lax.broadcasted_iota(jnp.int32, (tq, LANES), 1)
    in_map = (lane < DA_QK_DIM, lane >= DA_QK_DIM)
    lane_row = lax.broadcasted_iota(jnp.int32, (1, LANES), 1)
    ones_at_pieces = jnp.where(lane_row < ALIBI_PIECES, 1.0, 0.0)
    pow2 = jnp.exp2(-(jnp.zeros((1, LANES), F32) + (head + 1).astype(F32)))
    q_gain = gq_ref[...] * (DA_QK_DIM ** -0.5 * LOG2E)
    n_blocks = seq // tq

    def prepare(blk):
        rows = slice(blk * tq, (blk + 1) * tq)
        qn = qk_norm(q_ref[rows, :], q_gain)
        for m in range(2):
            q_rows = slice((2 * blk + m) * tq, (2 * blk + m + 1) * tq)
            qa_scr[q_rows, :LANES] = jnp.where(in_map[m], qn, 0.0).astype(BF16)
            qa_scr[q_rows, LANES:] = jnp.broadcast_to(ones_at_pieces, (tq, LANES)).astype(BF16)
        ka_scr[rows, :LANES] = qk_norm(k_ref[rows, :], gk_ref[...]).astype(BF16)
        ka_scr[rows, LANES:] = (pos_ref[rows, :] * pow2).astype(BF16)
        vt_scr[0:DA_V_DIM, rows] = v_ref[rows, :].astype(F32).T.astype(BF16)
        vt_scr[DA_V_DIM:, rows] = jnp.ones((ATTN_ONES_ROWS, tq), BF16)

    krel = lax.broadcasted_iota(jnp.int32, (tq, 2 * tq), 0)
    qrel = lax.broadcasted_iota(jnp.int32, (tq, 2 * tq), 1) % tq
    allowed = (krel // CHUNK) <= (qrel // CHUNK)
    diag_corr = slope * (-2.0 * jnp.maximum(krel - qrel, 0).astype(F32))

    def col_max_of(s):
        mx = s[0:8, :]
        for r0 in range(8, s.shape[0], 8):
            mx = jnp.maximum(mx, s[r0:r0 + 8, :])
        return jnp.max(mx, axis=0, keepdims=True)

    def scores(i):
        q_i = qa_scr[2 * i * tq:2 * (i + 1) * tq, :]
        diag = slice(i * tq, (i + 1) * tq)
        s = jnp.where(allowed, _dot_nt(ka_scr[diag, :], q_i) + diag_corr, MASK_VALUE)
        s_scr[i % ATTN_SCORE_BUFFERS, diag, :] = s
        cmax = col_max_of(s)
        if i > 0:
            s = _dot_nt(ka_scr[0:i * tq, :], q_i)
            s_scr[i % ATTN_SCORE_BUFFERS, 0:i * tq, :] = s
            cmax = jnp.maximum(cmax, col_max_of(s))
        return cmax

    def values(i, col_max):
        n_keys = (i + 1) * tq
        p = jnp.exp2(s_scr[i % ATTN_SCORE_BUFFERS, 0:n_keys, :] - col_max).astype(BF16)
        acc = _dot(vt_scr[:, 0:n_keys], p)
        ratio = acc[:DA_V_DIM] / acc[DA_V_DIM:DA_V_DIM + 1]
        o = ratio[:, :tq] - lam * ratio[:, tq:]
        o = o * lax.rsqrt(jnp.mean(o * o, axis=0, keepdims=True) + RMS_EPS) * go_ref[...] * (1.0 - lam_init)
        o_ref[i * tq:(i + 1) * tq, :] = o.T.astype(BF16)

    ahead = ATTN_SCORE_BUFFERS - 1
    col_max = {}
    prepared = [0]

    def prepare_through(blk):
        while prepared[0] <= min(blk, n_blocks - 1):
            prepare(prepared[0])
            prepared[0] += 1

    prepare_through(ATTN_PREP_LEAD - 1)
    for blk in range(min(ahead, n_blocks)):
        prepare_through(blk + ATTN_PREP_LEAD)
        col_max[blk] = scores(blk)
    for i in range(n_blocks):
        prepare_through(i + ahead + ATTN_PREP_LEAD)
        if i + ahead < n_blocks:
            col_max[i + ahead] = scores(i + ahead)
        values(i, col_max.pop(i))


ATTN_ONES_ROWS = 16
ATTN_SCORE_BUFFERS = 3
ATTN_PREP_LEAD = 2
ALIBI_PIECES = 4


def _alibi_position_table(seq):
    rest = np.arange(seq, dtype=np.float64) * LOG2E
    table = np.zeros((seq, LANES), np.float32)
    for p in range(ALIBI_PIECES):
        piece = rest.astype(np.float32).astype(BF16).astype(np.float32)
        table[:, p] = piece
        rest = rest - piece.astype(np.float64)
    return jnp.asarray(table)


def _diff_attn(proj3, lam_vecs, gq, gk, go_col, lam_init):
    b, s, _ = proj3.shape
    hd = 2 * DA_QK_DIM
    kern = functools.partial(_diff_attn_kernel, lam_init=lam_init, seq=s)
    est = (2 * 4 * s * hd * 2 + 4 * s * hd * 2 + (DA_V_DIM + ATTN_ONES_ROWS) * s * 2 + 2 * ATTN_SCORE_BUFFERS * ATTN_TILE * s * 4
           + 8 * s * hd * 4)
    return pl.pallas_call(
        kern,
        grid=(b, DA_HEADS),
        in_specs=[
            pl.BlockSpec((4, DA_QK_DIM), lambda bi, h: (0, 0)),
            pl.BlockSpec((s, LANES), lambda bi, h: (0, 0)),
            pl.BlockSpec((None, s, hd), lambda bi, h: (bi, 0, h)),
            pl.BlockSpec((None, s, hd), lambda bi, h: (bi, 0, DA_HEADS + h)),
            pl.BlockSpec((None, s, DA_V_DIM), lambda bi, h: (bi, 0, 2 * DA_HEADS + h)),
            pl.BlockSpec((1, hd), lambda bi, h: (0, 0)),
            pl.BlockSpec((1, hd), lambda bi, h: (0, 0)),
            pl.BlockSpec((DA_V_DIM, 1), lambda bi, h: (0, 0)),
        ],
        out_specs=pl.BlockSpec((None, s, DA_V_DIM), lambda bi, h: (bi, 0, h)),
        out_shape=jax.ShapeDtypeStruct((b, s, DA_HEADS * DA_V_DIM), BF16),
        scratch_shapes=[
            pltpu.VMEM((2 * s, 2 * hd), BF16),
            pltpu.VMEM((s, 2 * hd), BF16),
            pltpu.VMEM((DA_V_DIM + ATTN_ONES_ROWS, s), BF16),
            pltpu.VMEM((ATTN_SCORE_BUFFERS, s, 2 * ATTN_TILE), F32),
        ],
        compiler_params=pltpu.CompilerParams(
            dimension_semantics=("parallel", "parallel"), vmem_limit_bytes=_vmem_limit(est)),
        name="diff_attn",
    )(lam_vecs, _alibi_position_table(s), proj3, proj3, proj3, gq, gk, go_col)


def _mix_kernel(x_ref, pool_ref, halo_ref, mq_ref, oda_ref, mk_ref, mv_ref,
                gmix_ref, wg_ref, bg_ref, wpool_ref, pscale_ref, gmq_ref,
                wda_ref, wbp_ref, wbm_ref, wout_ref, o_ref, *, tiles_per_seq):
    tm = x_ref.shape[0]
    d = D_MODEL
    x = x_ref[...]
    hb = (_rms(x) * gmix_ref[...]).astype(BF16)

    tile_in_seq = pl.program_id(0) % tiles_per_seq
    p_cur = pool_ref[...].astype(F32)
    halo = jnp.where(tile_in_seq == 0, 0.0, halo_ref[...].astype(F32))
    ext = jnp.concatenate([halo, p_cur], axis=0)
    t_pos = tile_in_seq * tm + lax.broadcasted_iota(jnp.int32, (tm, 1), 0)
    pool_parts = []
    for g, w in enumerate(POOL_WINDOWS):
        c0 = g * POOL_GROUP_DIM
        s = ext[:, c0:c0 + POOL_GROUP_DIM]
        sh = 1
        while sh < w:
            s = s + pltpu.roll(s, sh, axis=0)
            sh *= 2
        cnt = jnp.minimum(t_pos + 1, w).astype(F32)
        dlt = (s[POOL_HALO:] / cnt - p_cur[:, c0:c0 + POOL_GROUP_DIM]).astype(BF16)
        pool_parts.append(_dot(dlt, wpool_ref[g]))
    o_pool = (jnp.concatenate(pool_parts, axis=1) * pscale_ref[...]).astype(BF16)

    mq = mq_ref[...].astype(F32)
    mem_parts = []
    for hh in range(MEM_HEADS):
        c0 = hh * MEM_HEAD_DIM
        qh = (_rms(mq[:, c0:c0 + MEM_HEAD_DIM]) * gmq_ref[...] * (MEM_HEAD_DIM ** -0.5)).astype(BF16)
        s = _dot_nt(qh, mk_ref[:, c0:c0 + MEM_HEAD_DIM])
        p = jnp.exp(s - jnp.max(s, axis=-1, keepdims=True))
        l = jnp.sum(p, axis=-1, keepdims=True)
        mem_parts.append(_dot(p.astype(BF16), mv_ref[:, c0:c0 + MEM_HEAD_DIM]) / l)
    o_mem = jnp.concatenate(mem_parts, axis=1).astype(BF16)

    def gate(k):
        return jax.nn.sigmoid(_dot(hb, wg_ref[:, k * d:(k + 1) * d]) + bg_ref[:, k * d:(k + 1) * d])

    merged = gate(0) * _dot(oda_ref[...], wda_ref[...])
    merged = merged + gate(1) * _dot(o_pool, wbp_ref[...])
    merged = merged + gate(2) * _dot(o_mem, wbm_ref[...])
    o_ref[...] = x + _dot(merged.astype(BF16), wout_ref[...])


def _mix(x, proj, o_da, mem_k, mem_v, gmix, wg, bg, wpool, pscale, gmq, wda, wbp, wbm, wout, seq):
    t, d = x.shape
    tm = ROW_TILE
    tiles_per_seq = seq // tm
    halo_blocks_per_tile = tm // POOL_HALO
    pool_col = (proj.shape[1] - POOL_WIDTH - MEM_WIDTH) // POOL_WIDTH
    mem_len = mem_k.shape[1]
    const = lambda i: (0, 0)
    weights_bytes = 2 * (wg.size + wda.size + wbp.size + wbm.size + wout.size + wpool.size)
    est = 2 * weights_bytes + 2 * (2 * tm * d * 4 + tm * (d + 2 * POOL_WIDTH) * 2) + 8 * tm * d * 4
    return pl.pallas_call(
        functools.partial(_mix_kernel, tiles_per_seq=tiles_per_seq),
        grid=(t // tm,),
        in_specs=[
            pl.BlockSpec((tm, d), lambda i: (i, 0)),
            pl.BlockSpec((tm, POOL_WIDTH), lambda i: (i, pool_col)),
            pl.BlockSpec((POOL_HALO, POOL_WIDTH),
                         lambda i: (jnp.maximum(i * halo_blocks_per_tile - 1, 0), pool_col)),
            pl.BlockSpec((tm, MEM_WIDTH), lambda i: (i, pool_col + 1)),
            pl.BlockSpec((tm, d), lambda i: (i, 0)),
            pl.BlockSpec((None, mem_len, MEM_WIDTH), lambda i: (i // tiles_per_seq, 0, 0)),
            pl.BlockSpec((None, mem_len, MEM_WIDTH), lambda i: (i // tiles_per_seq, 0, 0)),
            pl.BlockSpec((1, d), const),
            pl.BlockSpec(wg.shape, const, pipeline_mode=pl.Buffered(1)),
            pl.BlockSpec((1, bg.shape[1]), const),
            pl.BlockSpec(wpool.shape, lambda i: (0, 0, 0)),
            pl.BlockSpec((1, POOL_WIDTH), const),
            pl.BlockSpec((1, MEM_HEAD_DIM), const),
            pl.BlockSpec(wda.shape, const, pipeline_mode=pl.Buffered(1)),
            pl.BlockSpec(wbp.shape, const, pipeline_mode=pl.Buffered(1)),
            pl.BlockSpec(wbm.shape, const, pipeline_mode=pl.Buffered(1)),
            pl.BlockSpec(wout.shape, const, pipeline_mode=pl.Buffered(1)),
        ],
        out_specs=pl.BlockSpec((tm, d), lambda i: (i, 0)),
        out_shape=jax.ShapeDtypeStruct((t, d), F32),
        compiler_params=pltpu.CompilerParams(
            dimension_semantics=("parallel",), vmem_limit_bytes=_vmem_limit(est)),
        name="mix",
    )(x, proj, proj, proj, o_da, mem_k, mem_v, gmix, wg, bg, wpool, pscale, gmq, wda, wbp, wbm, wout)


def _split3(x):
    hi = x.astype(BF16)
    r = x - hi.astype(F32)
    mid = r.astype(BF16)
    lo = (r - mid.astype(F32)).astype(BF16)
    return hi, mid, lo


MOE_TILE = 512
SEG_MAIN = 192
SEG_OVER = MOE_TILE - SEG_MAIN
SEG_ALIGN = 16
NOT_ROUTED = -1e6
UP_COL_TILE = 1792
DOWN_COL_TILE = 1024
COPY_SIZES = (512, 256, 128, 64, 32, 16)


def _max_expert_tiles(t):
    padded_rows = 2 * t + (t // MOE_TILE) * N_EXPERTS * (SEG_ALIGN - 1)
    return padded_rows // MOE_TILE + N_EXPERTS


def _row_copies(src, dst, src_row0, dst_row0, n_rows, max_rows, sem):
    pairs = []
    pos = 0
    for size in COPY_SIZES:
        if size > max_rows:
            continue
        cond = jnp.bitwise_and(n_rows, size) != 0
        cp = pltpu.make_async_copy(
            src.at[pl.ds(pl.multiple_of(src_row0 + pos, SEG_ALIGN), size), :],
            dst.at[pl.ds(pl.multiple_of(dst_row0 + pos, SEG_ALIGN), size), :], sem)
        pairs.append((cond, cp))
        pos = pos + jnp.where(cond, size, 0)
    return pairs


def _start_all(pairs):
    for cond, cp in pairs:
        pl.when(cond)(cp.start)


def _wait_all(pairs):
    for cond, cp in pairs:
        pl.when(cond)(cp.wait)


def _onehot_rows(rank_row, n_rows, row0):
    r = lax.broadcasted_iota(jnp.int32, (n_rows, rank_row.shape[1]), 0).astype(F32) + float(row0)
    return jnp.where(r == rank_row, 1.0, 0.0).astype(BF16)


def _route_kernel(x_ref, g_ref, wr_ref, rank_ref, meta_ref, cnts_ref):
    i = pl.program_id(0)
    tm = x_ref.shape[0]
    h2 = _rms(x_ref[...]) * g_ref[...]
    h_hi, h_mid, _ = _split3(h2)
    w_hi, w_mid, _ = _split3(wr_ref[...])
    logits = _dot_nt(w_hi, h_hi) + (_dot_nt(w_hi, h_mid) + _dot_nt(w_mid, h_hi))
    eidx = lax.broadcasted_iota(jnp.int32, logits.shape, 0).astype(F32)
    m1 = jnp.max(logits, axis=0, keepdims=True)
    i1 = jnp.min(jnp.where(logits == m1, eidx, float(N_EXPERTS)), axis=0, keepdims=True)
    sel1 = eidx == i1
    rest = jnp.where(sel1, -jnp.inf, logits)
    m2 = jnp.max(rest, axis=0, keepdims=True)
    i2 = jnp.min(jnp.where(rest == m2, eidx, float(N_EXPERTS)), axis=0, keepdims=True)
    sel2 = eidx == i2
    e2 = jnp.exp(m2 - m1)
    g1 = 1.0 / (1.0 + e2)
    g2 = e2 / (1.0 + e2)

    mask = jnp.where(sel1 | sel2, 1.0, 0.0)
    before = (lax.broadcasted_iota(jnp.int32, (tm, tm), 0)
              < lax.broadcasted_iota(jnp.int32, (tm, tm), 1)).astype(BF16)
    rank = _dot(mask.astype(BF16), before)
    counts = jnp.sum(mask, axis=1, keepdims=True)
    rank_m = jnp.where(mask > 0.5, rank, NOT_ROUTED)

    rk1 = jnp.where(sel1, rank, NOT_ROUTED)
    rk2 = jnp.where(sel2, rank, NOT_ROUTED)
    gates = jnp.where(eidx == 0.0, g1, jnp.where(eidx == 1.0, g2, 0.0))
    packed = jnp.concatenate([rk1, rk2, gates], axis=0)
    ident = (lax.broadcasted_iota(jnp.int32, (tm, tm), 0)
             == lax.broadcasted_iota(jnp.int32, (tm, tm), 1)).astype(BF16)
    p_hi, p_mid, p_lo = _split3(packed)
    meta_ref[...] = _dot_nt(ident, p_hi) + (_dot_nt(ident, p_mid) + _dot_nt(ident, p_lo))

    rank_ref[...] = rank_m
    for e in range(N_EXPERTS):
        cnt = counts[e, 0].astype(jnp.int32)
        cnts_ref[i * N_EXPERTS + e] = jnp.bitwise_and(cnt + (SEG_ALIGN - 1), -SEG_ALIGN)


def _route(x, g, w_router_t):
    t, d = x.shape
    tm = MOE_TILE
    nt = t // tm
    est = 2 * tm * d * 4 + 6 * tm * tm * 4 + 4 * tm * d * 4
    return pl.pallas_call(
        _route_kernel,
        grid=(nt,),
        in_specs=[
            pl.BlockSpec((tm, d), lambda i: (i, 0)),
            pl.BlockSpec((1, d), lambda i: (0, 0)),
            pl.BlockSpec((N_EXPERTS, d), lambda i: (0, 0)),
        ],
        out_specs=[
            pl.BlockSpec((N_EXPERTS, tm), lambda i: (0, i)),
            pl.BlockSpec((tm, 3 * N_EXPERTS), lambda i: (i, 0)),
            pl.BlockSpec(memory_space=pltpu.SMEM),
        ],
        out_shape=[
            jax.ShapeDtypeStruct((N_EXPERTS, t), F32),
            jax.ShapeDtypeStruct((t, 3 * N_EXPERTS), F32),
            jax.ShapeDtypeStruct((nt * N_EXPERTS,), jnp.int32),
        ],
        compiler_params=pltpu.CompilerParams(
            dimension_semantics=("arbitrary",), vmem_limit_bytes=_vmem_limit(est)),
        name="route",
    )(x, g, w_router_t)


def _scatter_kernel(offs_ref, cnts_ref, fill_ref, nv_ref, x_ref, g_ref, rank_ref, xs_hbm,
                    stage, over_stage, zero_buf, sems, over_sem, *, n_tiles):
    i = pl.program_id(0)
    hb = (_rms(x_ref[...]) * g_ref[...]).astype(BF16)
    rank_m = rank_ref[...]

    def main_copies(step, e):
        n_main = jnp.minimum(cnts_ref[step * N_EXPERTS + e], SEG_MAIN)
        return _row_copies(stage.at[e], xs_hbm, 0, offs_ref[step * N_EXPERTS + e], n_main, SEG_MAIN,
                           sems.at[e])

    @pl.when(i > 0)
    def _():
        for e in range(N_EXPERTS):
            _wait_all(main_copies(i - 1, e))

    for e in range(N_EXPERTS):
        rank_e = rank_m[e:e + 1, :]
        stage[e] = _dot(_onehot_rows(rank_e, SEG_MAIN, 0), hb).astype(BF16)
        _start_all(main_copies(i, e))
        n_over = cnts_ref[i * N_EXPERTS + e] - SEG_MAIN

        @pl.when(n_over > 0)
        def _():
            over_stage[...] = _dot(_onehot_rows(rank_e, SEG_OVER, SEG_MAIN), hb).astype(BF16)
            pairs = _row_copies(over_stage, xs_hbm, 0, offs_ref[i * N_EXPERTS + e] + SEG_MAIN, n_over,
                                SEG_OVER, over_sem)
            _start_all(pairs)
            _wait_all(pairs)

    @pl.when(i == pl.num_programs(0) - 1)
    def _():
        for e in range(N_EXPERTS):
            _wait_all(main_copies(i, e))
        zero_buf[...] = jnp.zeros_like(zero_buf)
        for e in range(N_EXPERTS):
            pairs = _row_copies(zero_buf, xs_hbm, 0, fill_ref[e], fill_ref[N_EXPERTS + e], MOE_TILE,
                                sems.at[e])
            _start_all(pairs)
            _wait_all(pairs)

        def zero_tile(tile, carry):
            dst = xs_hbm.at[pl.ds(pl.multiple_of(tile * MOE_TILE, MOE_TILE), MOE_TILE), :]
            cp = pltpu.make_async_copy(zero_buf, dst, over_sem)
            cp.start()
            cp.wait()
            return carry

        lax.fori_loop(nv_ref[0], n_tiles, zero_tile, 0)


def _scatter(x, g, rank, offs, cnts, fill, n_valid, n_tiles):
    t, d = x.shape
    tm = MOE_TILE
    est = 2 * tm * d * 4 + (N_EXPERTS * SEG_MAIN + SEG_OVER + MOE_TILE) * d * 2 + 6 * tm * d * 4
    grid_spec = pltpu.PrefetchScalarGridSpec(
        num_scalar_prefetch=4,
        grid=(t // tm,),
        in_specs=[
            pl.BlockSpec((tm, d), lambda i, *_: (i, 0)),
            pl.BlockSpec((1, d), lambda i, *_: (0, 0)),
            pl.BlockSpec((N_EXPERTS, tm), lambda i, *_: (0, i)),
        ],
        out_specs=pl.BlockSpec(memory_space=pl.ANY),
        scratch_shapes=[
            pltpu.VMEM((N_EXPERTS, SEG_MAIN, d), BF16),
            pltpu.VMEM((SEG_OVER, d), BF16),
            pltpu.VMEM((MOE_TILE, d), BF16),
            pltpu.SemaphoreType.DMA((N_EXPERTS,)),
            pltpu.SemaphoreType.DMA(()),
        ],
    )
    return pl.pallas_call(
        functools.partial(_scatter_kernel, n_tiles=n_tiles),
        grid_spec=grid_spec,
        out_shape=jax.ShapeDtypeStruct((n_tiles * MOE_TILE, d), BF16),
        compiler_params=pltpu.CompilerParams(
            dimension_semantics=("arbitrary",), vmem_limit_bytes=_vmem_limit(est)),
        name="scatter",
    )(offs, cnts, fill, n_valid, x, g, rank)


def _expert_changed(te_ref, t):
    return (t == 0) | (te_ref[t] != te_ref[jnp.maximum(t - 1, 0)])


def _expert_up_kernel(te_ref, nv_ref, x_ref, w1_ref, w3_ref, h_ref, w1b, w3b):
    t = pl.program_id(1)

    @pl.when(_expert_changed(te_ref, t))
    def _():
        w1b[...] = w1_ref[...].astype(BF16)
        w3b[...] = w3_ref[...].astype(BF16)

    @pl.when(t < nv_ref[0])
    def _():
        x = x_ref[...]
        a = _dot(x, w1b[...])
        b = _dot(x, w3b[...])
        h_ref[...] = (a * jax.nn.sigmoid(a) * b).astype(BF16)

    @pl.when(t >= nv_ref[0])
    def _():
        h_ref[...] = jnp.zeros_like(h_ref)


def _expert_up(xs, w1, w3, tile_expert, n_valid):
    rows, d = xs.shape
    _, _, f_dim = w1.shape
    tf = UP_COL_TILE
    est = 2 * (MOE_TILE * d * 2 + 2 * d * tf * 4 + MOE_TILE * tf * 2) + 2 * d * tf * 2 + 3 * MOE_TILE * tf * 4
    grid_spec = pltpu.PrefetchScalarGridSpec(
        num_scalar_prefetch=2,
        grid=(f_dim // tf, rows // MOE_TILE),
        in_specs=[
            pl.BlockSpec((MOE_TILE, d), lambda f, t, te, nv: (t, 0)),
            pl.BlockSpec((None, d, tf), lambda f, t, te, nv: (te[t], 0, f)),
            pl.BlockSpec((None, d, tf), lambda f, t, te, nv: (te[t], 0, f)),
        ],
        out_specs=pl.BlockSpec((MOE_TILE, tf), lambda f, t, te, nv: (t, f)),
        scratch_shapes=[pltpu.VMEM((d, tf), BF16), pltpu.VMEM((d, tf), BF16)],
    )
    return pl.pallas_call(
        _expert_up_kernel,
        grid_spec=grid_spec,
        out_shape=jax.ShapeDtypeStruct((rows, f_dim), BF16),
        compiler_params=pltpu.CompilerParams(
            dimension_semantics=("arbitrary", "arbitrary"), vmem_limit_bytes=_vmem_limit(est)),
        name="expert_up",
    )(tile_expert, n_valid, xs, w1, w3)


def _expert_down_kernel(te_ref, nv_ref, h_ref, w2_ref, y_ref, w2b):
    t = pl.program_id(1)

    @pl.when(_expert_changed(te_ref, t))
    def _():
        w2b[...] = w2_ref[...].astype(BF16)

    @pl.when(t < nv_ref[0])
    def _():
        y_ref[...] = _dot(h_ref[...], w2b[...]).astype(BF16)

    @pl.when(t >= nv_ref[0])
    def _():
        y_ref[...] = jnp.zeros_like(y_ref)


def _expert_down(hs, w2, tile_expert, n_valid):
    rows, f_dim = hs.shape
    d = w2.shape[2]
    tn = DOWN_COL_TILE
    est = 2 * (MOE_TILE * f_dim * 2 + f_dim * tn * 4 + MOE_TILE * tn * 2) + f_dim * tn * 2 + MOE_TILE * tn * 4
    grid_spec = pltpu.PrefetchScalarGridSpec(
        num_scalar_prefetch=2,
        grid=(d // tn, rows // MOE_TILE),
        in_specs=[
            pl.BlockSpec((MOE_TILE, f_dim), lambda n, t, te, nv: (t, 0)),
            pl.BlockSpec((None, f_dim, tn), lambda n, t, te, nv: (te[t], 0, n)),
        ],
        out_specs=pl.BlockSpec((MOE_TILE, tn), lambda n, t, te, nv: (t, n)),
        scratch_shapes=[pltpu.VMEM((f_dim, tn), BF16)],
    )
    return pl.pallas_call(
        _expert_down_kernel,
        grid_spec=grid_spec,
        out_shape=jax.ShapeDtypeStruct((rows, d), BF16),
        compiler_params=pltpu.CompilerParams(
            dimension_semantics=("arbitrary", "arbitrary"), vmem_limit_bytes=_vmem_limit(est)),
        name="expert_down",
    )(tile_expert, n_valid, hs, w2)


def _combine_kernel(offs_ref, cnts_ref, x_ref, meta_ref, y_hbm, o_ref,
                    ybuf, over_buf, over_acc, sems, over_sem):
    i = pl.program_id(0)
    tm = x_ref.shape[0]
    total_rows = y_hbm.shape[0]
    meta = meta_ref[...]
    g1 = meta[:, 2 * N_EXPERTS:2 * N_EXPERTS + 1]
    g2 = meta[:, 2 * N_EXPERTS + 1:2 * N_EXPERTS + 2]

    def window(step, e, seg_row0, n_rows):
        off = offs_ref[step * N_EXPERTS + e] + seg_row0
        start = jnp.minimum(off, total_rows - n_rows)
        return start, off - start

    def main_window_copy(step, e):
        start, _ = window(step, e, 0, SEG_MAIN)
        src = y_hbm.at[pl.ds(pl.multiple_of(start, SEG_ALIGN), SEG_MAIN), :]
        return pltpu.make_async_copy(src, ybuf.at[step % 2, e], sems.at[step % 2, e])

    @pl.when(i == 0)
    def _():
        for e in range(N_EXPERTS):
            main_window_copy(i, e).start()

    @pl.when(i + 1 < pl.num_programs(0))
    def _():
        for e in range(N_EXPERTS):
            main_window_copy(i + 1, e).start()

    def weights(e, shift, n_cols):
        c = lax.broadcasted_iota(jnp.int32, (tm, n_cols), 1).astype(F32)
        first = meta[:, e:e + 1] + shift
        second = meta[:, N_EXPERTS + e:N_EXPERTS + e + 1] + shift
        return jnp.where(c == first, g1, jnp.where(c == second, g2, 0.0)).astype(BF16)

    out = x_ref[...]
    any_over = cnts_ref[i * N_EXPERTS] > SEG_MAIN
    for e in range(N_EXPERTS):
        _, shift = window(i, e, 0, SEG_MAIN)
        w_e = weights(e, shift.astype(F32), SEG_MAIN)
        main_window_copy(i, e).wait()
        out = out + _dot(w_e, ybuf[i % 2, e])
        if e > 0:
            any_over = any_over | (cnts_ref[i * N_EXPERTS + e] > SEG_MAIN)
    o_ref[...] = out

    @pl.when(any_over)
    def _():
        over_acc[...] = jnp.zeros_like(over_acc)
        for e in range(N_EXPERTS):
            @pl.when(cnts_ref[i * N_EXPERTS + e] > SEG_MAIN)
            def _():
                start, shift = window(i, e, SEG_MAIN, SEG_OVER)
                src = y_hbm.at[pl.ds(pl.multiple_of(start, SEG_ALIGN), SEG_OVER), :]
                cp = pltpu.make_async_copy(src, over_buf, over_sem)
                cp.start()
                cp.wait()
                over_acc[...] += _dot(weights(e, shift.astype(F32) - float(SEG_MAIN), SEG_OVER), over_buf[...])
        o_ref[...] += over_acc[...]


def _combine(x, meta, ys, offs, cnts):
    t, d = x.shape
    tm = MOE_TILE
    est = (4 * tm * d * 4 + (2 * N_EXPERTS * SEG_MAIN + SEG_OVER) * d * 2 + 2 * tm * d * 4
           + 2 * tm * LANES * 4 + 6 * tm * d * 4)
    grid_spec = pltpu.PrefetchScalarGridSpec(
        num_scalar_prefetch=2,
        grid=(t // tm,),
        in_specs=[
            pl.BlockSpec((tm, d), lambda i, *_: (i, 0)),
            pl.BlockSpec((tm, 3 * N_EXPERTS), lambda i, *_: (i, 0)),
            pl.BlockSpec(memory_space=pl.ANY),
        ],
        out_specs=pl.BlockSpec((tm, d), lambda i, *_: (i, 0)),
        scratch_shapes=[
            pltpu.VMEM((2, N_EXPERTS, SEG_MAIN, d), BF16),
            pltpu.VMEM((SEG_OVER, d), BF16),
            pltpu.VMEM((tm, d), F32),
            pltpu.SemaphoreType.DMA((2, N_EXPERTS)),
            pltpu.SemaphoreType.DMA(()),
        ],
    )
    return pl.pallas_call(
        _combine_kernel,
        grid_spec=grid_spec,
        out_shape=jax.ShapeDtypeStruct((t, d), F32),
        compiler_params=pltpu.CompilerParams(
            dimension_semantics=("arbitrary",), vmem_limit_bytes=_vmem_limit(est)),
        name="combine",
    )(offs, cnts, x, meta, ys)


def _moe(x, g, w_router, w1, w3, w2):
    t, _ = x.shape
    nt = t // MOE_TILE
    n_tiles = _max_expert_tiles(t)
    rank, meta, cnts = _route(x, g, w_router.T)
    cnt2 = cnts.reshape(nt, N_EXPERTS)
    rows_e = jnp.sum(cnt2, axis=0)
    tiles_e = jnp.maximum((rows_e + MOE_TILE - 1) // MOE_TILE, 1)
    ends = jnp.cumsum(tiles_e)
    base_e = (ends - tiles_e) * MOE_TILE
    offs = (base_e[None, :] + jnp.cumsum(cnt2, axis=0) - cnt2).reshape(-1).astype(jnp.int32)
    fill = jnp.concatenate([base_e + rows_e, tiles_e * MOE_TILE - rows_e]).astype(jnp.int32)
    n_valid = ends[-1].reshape(1).astype(jnp.int32)
    tid = jnp.arange(n_tiles, dtype=jnp.int32)
    tile_expert = jnp.minimum(jnp.sum(tid[:, None] >= ends[None, :], axis=1), N_EXPERTS - 1).astype(jnp.int32)
    xs = _scatter(x, g, rank, offs, cnts, fill, n_valid, n_tiles)
    hs = _expert_up(xs, w1, w3, tile_expert, n_valid)
    ys = _expert_down(hs, w2, tile_expert, n_valid)
    return _combine(x, meta, ys, offs, cnts)


def _ffn_kernel(x_ref, g_ref, w1_ref, w3_ref, w2_ref, o_ref):
    x = x_ref[...]
    hb = (_rms(x) * g_ref[...]).astype(BF16)
    f_dim = w1_ref.shape[1]
    out = x
    for c0 in range(0, f_dim, FFN_HIDDEN_CHUNK):
        c1 = min(c0 + FFN_HIDDEN_CHUNK, f_dim)
        a = _dot(hb, w1_ref[:, c0:c1])
        b = _dot(hb, w3_ref[:, c0:c1])
        out = out + _dot((a * jax.nn.sigmoid(a) * b).astype(BF16), w2_ref[c0:c1, :])
    o_ref[...] = out


def _ffn(x, g, w1, w3, w2):
    t, d = x.shape
    f_dim = w1.shape[1]
    tm = FFN_ROW_TILE
    resident = pl.Buffered(1)
    est = 3 * d * f_dim * 2 + 4 * tm * d * 4 + 4 * tm * FFN_HIDDEN_CHUNK * 4
    return pl.pallas_call(
        _ffn_kernel,
        grid=(t // tm,),
        in_specs=[
            pl.BlockSpec((tm, d), lambda i: (i, 0)),
            pl.BlockSpec((1, d), lambda i: (0, 0)),
            pl.BlockSpec((d, f_dim), lambda i: (0, 0), pipeline_mode=resident),
            pl.BlockSpec((d, f_dim), lambda i: (0, 0), pipeline_mode=resident),
            pl.BlockSpec((f_dim, d), lambda i: (0, 0), pipeline_mode=resident),
        ],
        out_specs=pl.BlockSpec((tm, d), lambda i: (i, 0)),
        out_shape=jax.ShapeDtypeStruct((t, d), F32),
        compiler_params=pltpu.CompilerParams(
            dimension_semantics=("parallel",), vmem_limit_bytes=_vmem_limit(est)),
        name="ffn",
    )(x, g, w1, w3, w2)


def _lambda_init(layer):
    return 0.8 - 0.6 * math.exp(-0.3 * layer)


def kernel(x, mem, norm_mix, norm_mem, norm_ffn, w_in, w_gate, b_gate, da_q_norm, da_k_norm,
           da_lam_q1, da_lam_k1, da_lam_q2, da_lam_k2, da_out_norm, pool_w, pool_scale,
           mem_q_norm, mem_k_norm, w_mem_kv, w_br_da, w_br_pool, w_br_mem, w_out,
           ffn_w1, ffn_w3, ffn_w2, moe_router, moe_w1, moe_w3, moe_w2):
    b, s, d = x.shape
    depth = w_in.shape[0]
    xt = x.reshape(b * s, d)
    bf = lambda a: a.astype(BF16)
    row = lambda v: v.reshape(1, -1)
    for l in range(depth):
        lam0 = _lambda_init(l)
        proj = _in_proj(xt, row(norm_mix[l]), w_in, l)
        mem_k, mem_v = _mem_kv(mem, row(norm_mem[l]), bf(w_mem_kv[l]), row(mem_k_norm[l]))
        lam_vecs = jnp.stack([da_lam_q1[l], da_lam_k1[l], da_lam_q2[l], da_lam_k2[l]])
        o_da = _diff_attn(proj.reshape(b, s, -1), lam_vecs, row(da_q_norm[l]), row(da_k_norm[l]),
                          da_out_norm[l].reshape(-1, 1), lam0)
        xt = _mix(xt, proj, o_da.reshape(b * s, -1), mem_k, mem_v, row(norm_mix[l]), bf(w_gate[l]),
                  row(b_gate[l]), bf(pool_w[l]), row(pool_scale[l]), row(mem_q_norm[l]),
                  bf(w_br_da[l]), bf(w_br_pool[l]), bf(w_br_mem[l]), bf(w_out[l]), s)
        j = l // 2
        if l % 2 == 0:
            xt = _ffn(xt, row(norm_ffn[l]), bf(ffn_w1[j]), bf(ffn_w3[j]), bf(ffn_w2[j]))
        else:
            xt = _moe(xt, row(norm_ffn[l]), moe_router[j], moe_w1[j], moe_w3[j], moe_w2[j])
    return xt.reshape(b, s, d)
```

```python
import functools
import math

import jax
import jax.numpy as jnp
import numpy as np
from jax import lax
from jax.experimental import pallas as pl
from jax.experimental.pallas import tpu as pltpu

F32 = jnp.float32
BF16 = jnp.bfloat16

D_MODEL = 1024
CHUNK = 64
RMS_EPS = 1e-6
DA_HEADS = 8
DA_QK_DIM = 64
DA_V_DIM = 128
POOL_WINDOWS = (2, 4, 8, 16)
POOL_GROUP_DIM = 128
POOL_WIDTH = 512
MEM_HEADS = 4
MEM_HEAD_DIM = 128
MEM_WIDTH = 512
N_EXPERTS = 8
LOG2E = 1.4426950408889634
MASK_VALUE = -1e30

V7X_VMEM_BYTES = 64 * 1024 * 1024
VMEM_LIMIT_CAP = 56 * 1024 * 1024
LANES = 128

ROW_TILE = 1024
IN_PROJ_ROW_TILE = 512
ATTN_TILE = 256
POOL_HALO = 16
FFN_ROW_TILE = 512
FFN_HIDDEN_CHUNK = 1536


def _vmem_limit(estimate_bytes):
    return int(min(VMEM_LIMIT_CAP, max(32 * 1024 * 1024, 2 * estimate_bytes)))


def _rms(x):
    return x * lax.rsqrt(jnp.mean(x * x, axis=-1, keepdims=True) + RMS_EPS)


def _dot(a, b):
    return jnp.dot(a, b, preferred_element_type=F32)


def _dot_nt(a, b):
    return lax.dot_general(a, b, (((1,), (1,)), ((), ())), preferred_element_type=F32)


def _in_proj_kernel(x_ref, g_ref, w_ref, o_ref, wb_scr, *, col_tile):
    n = w_ref.shape[1]

    @pl.when(pl.program_id(0) == 0)
    def _():
        for c0 in range(0, n, col_tile):
            wb_scr[:, c0:c0 + col_tile] = w_ref[:, c0:c0 + col_tile].astype(BF16)

    hb = (_rms(x_ref[...]) * g_ref[...]).astype(BF16)
    for c0 in range(0, n, col_tile):
        o_ref[:, c0:c0 + col_tile] = _dot(hb, wb_scr[:, c0:c0 + col_tile]).astype(BF16)


def _in_proj(x, g, w_stack, layer):
    t, d = x.shape
    n = w_stack.shape[2]
    tm = IN_PROJ_ROW_TILE
    est = 2 * (tm * d * 4 + tm * n * 2) + d * n * 6 + 2 * tm * 1024 * 4
    return pl.pallas_call(
        functools.partial(_in_proj_kernel, col_tile=1024),
        grid=(t // tm,),
        in_specs=[
            pl.BlockSpec((tm, d), lambda i: (i, 0)),
            pl.BlockSpec((1, d), lambda i: (0, 0)),
            pl.BlockSpec((None, d, n), lambda i: (layer, 0, 0), pipeline_mode=pl.Buffered(1)),
        ],
        out_specs=pl.BlockSpec((tm, n), lambda i: (i, 0)),
        out_shape=jax.ShapeDtypeStruct((t, n), BF16),
        scratch_shapes=[pltpu.VMEM((d, n), BF16)],
        compiler_params=pltpu.CompilerParams(
            dimension_semantics=("arbitrary",), vmem_limit_bytes=_vmem_limit(est)),
        name="in_proj",
    )(x, g, w_stack)


def _mem_kv_kernel(mem_ref, g_ref, w_ref, gk_ref, k_ref, v_ref):
    mh = (_rms(mem_ref[...]) * g_ref[...]).astype(BF16)
    kv = _dot(mh, w_ref[...])
    for hh in range(MEM_HEADS):
        c0 = hh * MEM_HEAD_DIM
        k_ref[:, c0:c0 + MEM_HEAD_DIM] = (_rms(kv[:, c0:c0 + MEM_HEAD_DIM]) * gk_ref[...]).astype(BF16)
    v_ref[...] = kv[:, MEM_WIDTH:].astype(BF16)


def _mem_kv(mem, g, w_bf16, gk):
    b, m, d = mem.shape
    out = jax.ShapeDtypeStruct((b, m, MEM_WIDTH), BF16)
    return pl.pallas_call(
        _mem_kv_kernel,
        grid=(b,),
        in_specs=[
            pl.BlockSpec((None, m, d), lambda i: (i, 0, 0)),
            pl.BlockSpec((1, d), lambda i: (0, 0)),
            pl.BlockSpec((d, 2 * MEM_WIDTH), lambda i: (0, 0)),
            pl.BlockSpec((1, MEM_HEAD_DIM), lambda i: (0, 0)),
        ],
        out_specs=[pl.BlockSpec((None, m, MEM_WIDTH), lambda i: (i, 0, 0))] * 2,
        out_shape=[out, out],
        compiler_params=pltpu.CompilerParams(dimension_semantics=("parallel",)),
        name="mem_kv",
    )(mem, g, w_bf16, gk)


def _diff_attn_kernel(lam_ref, pos_ref, q_ref, k_ref, v_ref, gq_ref, gk_ref, go_ref, o_ref,
                      qa_scr, ka_scr, vt_scr, s_scr, *, lam_init, seq):
    tq = ATTN_TILE
    head = pl.program_id(1)
    slope = jnp.exp2(-(jnp.zeros((1, LANES), F32) + (head + 1).astype(F32)))[:, :1] * LOG2E

    lam_v = lam_ref[...]
    lam = (jnp.exp(jnp.sum(lam_v[0:1] * lam_v[1:2], axis=-1, keepdims=True))
           - jnp.exp(jnp.sum(lam_v[2:3] * lam_v[3:4], axis=-1, keepdims=True)) + lam_init)

    r_i = lax.broadcasted_iota(jnp.int32, (LANES, LANES), 0) // DA_QK_DIM
    c_i = lax.broadcasted_iota(jnp.int32, (LANES, LANES), 1) // DA_QK_DIM
    group_mean = jnp.where(r_i == c_i, 1.0 / DA_QK_DIM, 0.0).astype(BF16)

    def qk_norm(x_bf16, gain):
        xf = x_bf16.astype(F32)
        ms = _dot((xf * xf).astype(BF16), group_mean)
        return xf * (lax.rsqrt(ms + RMS_EPS) * gain)

    lane = lax.broadcasted_iota(jnp.int32, (tq, LANES), 1)
    in_map = (lane < DA_QK_DIM, lane >= DA_QK_DIM)
    lane_row = lax.broadcasted_iota(jnp.int32, (1, LANES), 1)
    ones_at_pieces = jnp.where(lane_row < ALIBI_PIECES, 1.0, 0.0)
    pow2 = jnp.exp2(-(jnp.zeros((1, LANES), F32) + (head + 1).astype(F32)))
    q_gain = gq_ref[...] * (DA_QK_DIM ** -0.5 * LOG2E)
    n_blocks = seq // tq

    def prepare(blk):
        rows = slice(blk * tq, (blk + 1) * tq)
        qn = qk_norm(q_ref[rows, :], q_gain)
        for m in range(2):
            q_rows = slice((2 * blk + m) * tq, (2 * blk + m + 1) * tq)
            qa_scr[q_rows, :LANES] = jnp.where(in_map[m], qn, 0.0).astype(BF16)
            qa_scr[q_rows, LANES:] = jnp.broadcast_to(ones_at_pieces, (tq, LANES)).astype(BF16)
        ka_scr[rows, :LANES] = qk_norm(k_ref[rows, :], gk_ref[...]).astype(BF16)
        ka_scr[rows, LANES:] = (pos_ref[rows, :] * pow2).astype(BF16)
        vt_scr[0:DA_V_DIM, rows] = v_ref[rows, :].astype(F32).T.astype(BF16)
        vt_scr[DA_V_DIM:, rows] = jnp.ones((ATTN_ONES_ROWS, tq), BF16)

    krel = lax.broadcasted_iota(jnp.int32, (tq, 2 * tq), 0)
    qrel = lax.broadcasted_iota(jnp.int32, (tq, 2 * tq), 1) % tq
    allowed = (krel // CHUNK) <= (qrel // CHUNK)
    diag_corr = slope * (-2.0 * jnp.maximum(krel - qrel, 0).astype(F32))

    def col_max_of(s):
        mx = s[0:8, :]
        for r0 in range(8, s.shape[0], 8):
            mx = jnp.maximum(mx, s[r0:r0 + 8, :])
        return jnp.max(mx, axis=0, keepdims=True)

    def scores(i):
        q_i = qa_scr[2 * i * tq:2 * (i + 1) * tq, :]
        diag = slice(i * tq, (i + 1) * tq)
        s = jnp.where(allowed, _dot_nt(ka_scr[diag, :], q_i) + diag_corr, MASK_VALUE)
        s_scr[i % ATTN_SCORE_BUFFERS, diag, :] = s
        cmax = col_max_of(s)
        if i > 0:
            s = _dot_nt(ka_scr[0:i * tq, :], q_i)
            s_scr[i % ATTN_SCORE_BUFFERS, 0:i * tq, :] = s
            cmax = jnp.maximum(cmax, col_max_of(s))
        return cmax

    def values(i, col_max):
        n_keys = (i + 1) * tq
        p = jnp.exp2(s_scr[i % ATTN_SCORE_BUFFERS, 0:n_keys, :] - col_max).astype(BF16)
        acc = _dot(vt_scr[:, 0:n_keys], p)
        ratio = acc[:DA_V_DIM] / acc[DA_V_DIM:DA_V_DIM + 1]
        o = ratio[:, :tq] - lam * ratio[:, tq:]
        o = o * lax.rsqrt(jnp.mean(o * o, axis=0, keepdims=True) + RMS_EPS) * go_ref[...] * (1.0 - lam_init)
        o_ref[i * tq:(i + 1) * tq, :] = o.T.astype(BF16)

    ahead = ATTN_SCORE_BUFFERS - 1
    col_max = {}
    prepared = [0]

    def prepare_through(blk):
        while prepared[0] <= min(blk, n_blocks - 1):
            prepare(prepared[0])
            prepared[0] += 1

    prepare_through(ATTN_PREP_LEAD - 1)
    for blk in range(min(ahead, n_blocks)):
        prepare_through(blk + ATTN_PREP_LEAD)
        col_max[blk] = scores(blk)
    for i in range(n_blocks):
        prepare_through(i + ahead + ATTN_PREP_LEAD)
        if i + ahead < n_blocks:
            col_max[i + ahead] = scores(i + ahead)
        values(i, col_max.pop(i))


ATTN_ONES_ROWS = 16
ATTN_SCORE_BUFFERS = 3
ATTN_PREP_LEAD = 2
ALIBI_PIECES = 4


def _alibi_position_table(seq):
    rest = np.arange(seq, dtype=np.float64) * LOG2E
    table = np.zeros((seq, LANES), np.float32)
    for p in range(ALIBI_PIECES):
        piece = rest.astype(np.float32).astype(BF16).astype(np.float32)
        table[:, p] = piece
        rest = rest - piece.astype(np.float64)
    return jnp.asarray(table)


def _diff_attn(proj3, lam_vecs, gq, gk, go_col, lam_init):
    b, s, _ = proj3.shape
    hd = 2 * DA_QK_DIM
    kern = functools.partial(_diff_attn_kernel, lam_init=lam_init, seq=s)
    est = (2 * 4 * s * hd * 2 + 4 * s * hd * 2 + (DA_V_DIM + ATTN_ONES_ROWS) * s * 2 + 2 * ATTN_SCORE_BUFFERS * ATTN_TILE * s * 4
           + 8 * s * hd * 4)
    return pl.pallas_call(
        kern,
        grid=(b, DA_HEADS),
        in_specs=[
            pl.BlockSpec((4, DA_QK_DIM), lambda bi, h: (0, 0)),
            pl.BlockSpec((s, LANES), lambda bi, h: (0, 0)),
            pl.BlockSpec((None, s, hd), lambda bi, h: (bi, 0, h)),
            pl.BlockSpec((None, s, hd), lambda bi, h: (bi, 0, DA_HEADS + h)),
            pl.BlockSpec((None, s, DA_V_DIM), lambda bi, h: (bi, 0, 2 * DA_HEADS + h)),
            pl.BlockSpec((1, hd), lambda bi, h: (0, 0)),
            pl.BlockSpec((1, hd), lambda bi, h: (0, 0)),
            pl.BlockSpec((DA_V_DIM, 1), lambda bi, h: (0, 0)),
        ],
        out_specs=pl.BlockSpec((None, s, DA_V_DIM), lambda bi, h: (bi, 0, h)),
        out_shape=jax.ShapeDtypeStruct((b, s, DA_HEADS * DA_V_DIM), BF16),
        scratch_shapes=[
            pltpu.VMEM((2 * s, 2 * hd), BF16),
            pltpu.VMEM((s, 2 * hd), BF16),
            pltpu.VMEM((DA_V_DIM + ATTN_ONES_ROWS, s), BF16),
            pltpu.VMEM((ATTN_SCORE_BUFFERS, s, 2 * ATTN_TILE), F32),
        ],
        compiler_params=pltpu.CompilerParams(
            dimension_semantics=("parallel", "parallel"), vmem_limit_bytes=_vmem_limit(est)),
        name="diff_attn",
    )(lam_vecs, _alibi_position_table(s), proj3, proj3, proj3, gq, gk, go_col)


def _mix_kernel(x_ref, pool_ref, halo_ref, mq_ref, oda_ref, mk_ref, mv_ref,
                gmix_ref, wg_ref, bg_ref, wpool_ref, pscale_ref, gmq_ref,
                wda_ref, wbp_ref, wbm_ref, wout_ref, o_ref, *, tiles_per_seq):
    tm = x_ref.shape[0]
    d = D_MODEL
    x = x_ref[...]
    hb = (_rms(x) * gmix_ref[...]).astype(BF16)

    tile_in_seq = pl.program_id(0) % tiles_per_seq
    p_cur = pool_ref[...].astype(F32)
    halo = jnp.where(tile_in_seq == 0, 0.0, halo_ref[...].astype(F32))
    ext = jnp.concatenate([halo, p_cur], axis=0)
    t_pos = tile_in_seq * tm + lax.broadcasted_iota(jnp.int32, (tm, 1), 0)
    pool_parts = []
    for g, w in enumerate(POOL_WINDOWS):
        c0 = g * POOL_GROUP_DIM
        s = ext[:, c0:c0 + POOL_GROUP_DIM]
        sh = 1
        while sh < w:
            s = s + pltpu.roll(s, sh, axis=0)
            sh *= 2
        cnt = jnp.minimum(t_pos + 1, w).astype(F32)
        dlt = (s[POOL_HALO:] / cnt - p_cur[:, c0:c0 + POOL_GROUP_DIM]).astype(BF16)
        pool_parts.append(_dot(dlt, wpool_ref[g]))
    o_pool = (jnp.concatenate(pool_parts, axis=1) * pscale_ref[...]).astype(BF16)

    mq = mq_ref[...].astype(F32)
    mem_parts = []
    for hh in range(MEM_HEADS):
        c0 = hh * MEM_HEAD_DIM
        qh = (_rms(mq[:, c0:c0 + MEM_HEAD_DIM]) * gmq_ref[...] * (MEM_HEAD_DIM ** -0.5)).astype(BF16)
        s = _dot_nt(qh, mk_ref[:, c0:c0 + MEM_HEAD_DIM])
        p = jnp.exp(s - jnp.max(s, axis=-1, keepdims=True))
        l = jnp.sum(p, axis=-1, keepdims=True)
        mem_parts.append(_dot(p.astype(BF16), mv_ref[:, c0:c0 + MEM_HEAD_DIM]) / l)
    o_mem = jnp.concatenate(mem_parts, axis=1).astype(BF16)

    def gate(k):
        return jax.nn.sigmoid(_dot(hb, wg_ref[:, k * d:(k + 1) * d]) + bg_ref[:, k * d:(k + 1) * d])

    merged = gate(0) * _dot(oda_ref[...], wda_ref[...])
    merged = merged + gate(1) * _dot(o_pool, wbp_ref[...])
    merged = merged + gate(2) * _dot(o_mem, wbm_ref[...])
    o_ref[...] = x + _dot(merged.astype(BF16), wout_ref[...])


def _mix(x, proj, o_da, mem_k, mem_v, gmix, wg, bg, wpool, pscale, gmq, wda, wbp, wbm, wout, seq):
    t, d = x.shape
    tm = ROW_TILE
    tiles_per_seq = seq // tm
    halo_blocks_per_tile = tm // POOL_HALO
    pool_col = (proj.shape[1] - POOL_WIDTH - MEM_WIDTH) // POOL_WIDTH
    mem_len = mem_k.shape[1]
    const = lambda i: (0, 0)
    weights_bytes = 2 * (wg.size + wda.size + wbp.size + wbm.size + wout.size + wpool.size)
    est = 2 * weights_bytes + 2 * (2 * tm * d * 4 + tm * (d + 2 * POOL_WIDTH) * 2) + 8 * tm * d * 4
    return pl.pallas_call(
        functools.partial(_mix_kernel, tiles_per_seq=tiles_per_seq),
        grid=(t // tm,),
        in_specs=[
            pl.BlockSpec((tm, d), lambda i: (i, 0)),
            pl.BlockSpec((tm, POOL_WIDTH), lambda i: (i, pool_col)),
            pl.BlockSpec((POOL_HALO, POOL_WIDTH),
                         lambda i: (jnp.maximum(i * halo_blocks_per_tile - 1, 0), pool_col)),
            pl.BlockSpec((tm, MEM_WIDTH), lambda i: (i, pool_col + 1)),
            pl.BlockSpec((tm, d), lambda i: (i, 0)),
            pl.BlockSpec((None, mem_len, MEM_WIDTH), lambda i: (i // tiles_per_seq, 0, 0)),
            pl.BlockSpec((None, mem_len, MEM_WIDTH), lambda i: (i // tiles_per_seq, 0, 0)),
            pl.BlockSpec((1, d), const),
            pl.BlockSpec(wg.shape, const, pipeline_mode=pl.Buffered(1)),
            pl.BlockSpec((1, bg.shape[1]), const),
            pl.BlockSpec(wpool.shape, lambda i: (0, 0, 0)),
            pl.BlockSpec((1, POOL_WIDTH), const),
            pl.BlockSpec((1, MEM_HEAD_DIM), const),
            pl.BlockSpec(wda.shape, const, pipeline_mode=pl.Buffered(1)),
            pl.BlockSpec(wbp.shape, const, pipeline_mode=pl.Buffered(1)),
            pl.BlockSpec(wbm.shape, const, pipeline_mode=pl.Buffered(1)),
            pl.BlockSpec(wout.shape, const, pipeline_mode=pl.Buffered(1)),
        ],
        out_specs=pl.BlockSpec((tm, d), lambda i: (i, 0)),
        out_shape=jax.ShapeDtypeStruct((t, d), F32),
        compiler_params=pltpu.CompilerParams(
            dimension_semantics=("parallel",), vmem_limit_bytes=_vmem_limit(est)),
        name="mix",
    )(x, proj, proj, proj, o_da, mem_k, mem_v, gmix, wg, bg, wpool, pscale, gmq, wda, wbp, wbm, wout)


def _split3(x):
    hi = x.astype(BF16)
    r = x - hi.astype(F32)
    mid = r.astype(BF16)
    lo = (r - mid.astype(F32)).astype(BF16)
    return hi, mid, lo


MOE_TILE = 512
SEG_MAIN = 192
SEG_OVER = MOE_TILE - SEG_MAIN
SEG_ALIGN = 16
NOT_ROUTED = -1e6
UP_COL_TILE = 1792
DOWN_COL_TILE = 1024
COPY_SIZES = (512, 256, 128, 64, 32, 16)


def _max_expert_tiles(t):
    padded_rows = 2 * t + (t // MOE_TILE) * N_EXPERTS * (SEG_ALIGN - 1)
    return padded_rows // MOE_TILE + N_EXPERTS


def _row_copies(src, dst, src_row0, dst_row0, n_rows, max_rows, sem):
    pairs = []
    pos = 0
    for size in COPY_SIZES:
        if size > max_rows:
            continue
        cond = jnp.bitwise_and(n_rows, size) != 0
        cp = pltpu.make_async_copy(
            src.at[pl.ds(pl.multiple_of(src_row0 + pos, SEG_ALIGN), size), :],
            dst.at[pl.ds(pl.multiple_of(dst_row0 + pos, SEG_ALIGN), size), :], sem)
        pairs.append((cond, cp))
        pos = pos + jnp.where(cond, size, 0)
    return pairs


def _start_all(pairs):
    for cond, cp in pairs:
        pl.when(cond)(cp.start)


def _wait_all(pairs):
    for cond, cp in pairs:
        pl.when(cond)(cp.wait)


def _onehot_rows(rank_row, n_rows, row0):
    r = lax.broadcasted_iota(jnp.int32, (n_rows, rank_row.shape[1]), 0).astype(F32) + float(row0)
    return jnp.where(r == rank_row, 1.0, 0.0).astype(BF16)


def _route_kernel(x_ref, g_ref, wr_ref, rank_ref, meta_ref, cnts_ref):
    i = pl.program_id(0)
    tm = x_ref.shape[0]
    h2 = _rms(x_ref[...]) * g_ref[...]
    h_hi, h_mid, _ = _split3(h2)
    w_hi, w_mid, _ = _split3(wr_ref[...])
    logits = _dot_nt(w_hi, h_hi) + (_dot_nt(w_hi, h_mid) + _dot_nt(w_mid, h_hi))
    eidx = lax.broadcasted_iota(jnp.int32, logits.shape, 0).astype(F32)
    m1 = jnp.max(logits, axis=0, keepdims=True)
    i1 = jnp.min(jnp.where(logits == m1, eidx, float(N_EXPERTS)), axis=0, keepdims=True)
    sel1 = eidx == i1
    rest = jnp.where(sel1, -jnp.inf, logits)
    m2 = jnp.max(rest, axis=0, keepdims=True)
    i2 = jnp.min(jnp.where(rest == m2, eidx, float(N_EXPERTS)), axis=0, keepdims=True)
    sel2 = eidx == i2
    e2 = jnp.exp(m2 - m1)
    g1 = 1.0 / (1.0 + e2)
    g2 = e2 / (1.0 + e2)

    mask = jnp.where(sel1 | sel2, 1.0, 0.0)
    before = (lax.broadcasted_iota(jnp.int32, (tm, tm), 0)
              < lax.broadcasted_iota(jnp.int32, (tm, tm), 1)).astype(BF16)
    rank = _dot(mask.astype(BF16), before)
    counts = jnp.sum(mask, axis=1, keepdims=True)
    rank_m = jnp.where(mask > 0.5, rank, NOT_ROUTED)

    rk1 = jnp.where(sel1, rank, NOT_ROUTED)
    rk2 = jnp.where(sel2, rank, NOT_ROUTED)
    gates = jnp.where(eidx == 0.0, g1, jnp.where(eidx == 1.0, g2, 0.0))
    packed = jnp.concatenate([rk1, rk2, gates], axis=0)
    ident = (lax.broadcasted_iota(jnp.int32, (tm, tm), 0)
             == lax.broadcasted_iota(jnp.int32, (tm, tm), 1)).astype(BF16)
    p_hi, p_mid, p_lo = _split3(packed)
    meta_ref[...] = _dot_nt(ident, p_hi) + (_dot_nt(ident, p_mid) + _dot_nt(ident, p_lo))

    rank_ref[...] = rank_m
    for e in range(N_EXPERTS):
        cnt = counts[e, 0].astype(jnp.int32)
        cnts_ref[i * N_EXPERTS + e] = jnp.bitwise_and(cnt + (SEG_ALIGN - 1), -SEG_ALIGN)


def _route(x, g, w_router_t):
    t, d = x.shape
    tm = MOE_TILE
    nt = t // tm
    est = 2 * tm * d * 4 + 6 * tm * tm * 4 + 4 * tm * d * 4
    return pl.pallas_call(
        _route_kernel,
        grid=(nt,),
        in_specs=[
            pl.BlockSpec((tm, d), lambda i: (i, 0)),
            pl.BlockSpec((1, d), lambda i: (0, 0)),
            pl.BlockSpec((N_EXPERTS, d), lambda i: (0, 0)),
        ],
        out_specs=[
            pl.BlockSpec((N_EXPERTS, tm), lambda i: (0, i)),
            pl.BlockSpec((tm, 3 * N_EXPERTS), lambda i: (i, 0)),
            pl.BlockSpec(memory_space=pltpu.SMEM),
        ],
        out_shape=[
            jax.ShapeDtypeStruct((N_EXPERTS, t), F32),
            jax.ShapeDtypeStruct((t, 3 * N_EXPERTS), F32),
            jax.ShapeDtypeStruct((nt * N_EXPERTS,), jnp.int32),
        ],
        compiler_params=pltpu.CompilerParams(
            dimension_semantics=("arbitrary",), vmem_limit_bytes=_vmem_limit(est)),
        name="route",
    )(x, g, w_router_t)


def _scatter_kernel(offs_ref, cnts_ref, fill_ref, nv_ref, x_ref, g_ref, rank_ref, xs_hbm,
                    stage, over_stage, zero_buf, sems, over_sem, *, n_tiles):
    i = pl.program_id(0)
    hb = (_rms(x_ref[...]) * g_ref[...]).astype(BF16)
    rank_m = rank_ref[...]

    def main_copies(step, e):
        n_main = jnp.minimum(cnts_ref[step * N_EXPERTS + e], SEG_MAIN)
        return _row_copies(stage.at[step % 2, e], xs_hbm, 0, offs_ref[step * N_EXPERTS + e], n_main,
                           SEG_MAIN, sems.at[step % 2, e])

    @pl.when(i > 1)
    def _():
        for e in range(N_EXPERTS):
            _wait_all(main_copies(i - 2, e))

    for e in range(N_EXPERTS):
        rank_e = rank_m[e:e + 1, :]
        stage[i % 2, e] = _dot(_onehot_rows(rank_e, SEG_MAIN, 0), hb).astype(BF16)
        _start_all(main_copies(i, e))
        n_over = cnts_ref[i * N_EXPERTS + e] - SEG_MAIN

        @pl.when(n_over > 0)
        def _():
            over_stage[...] = _dot(_onehot_rows(rank_e, SEG_OVER, SEG_MAIN), hb).astype(BF16)
            pairs = _row_copies(over_stage, xs_hbm, 0, offs_ref[i * N_EXPERTS + e] + SEG_MAIN, n_over,
                                SEG_OVER, over_sem)
            _start_all(pairs)
            _wait_all(pairs)

    @pl.when(i == pl.num_programs(0) - 1)
    def _():
        for e in range(N_EXPERTS):
            _wait_all(main_copies(i - 1, e))
            _wait_all(main_copies(i, e))
        zero_buf[...] = jnp.zeros_like(zero_buf)
        for e in range(N_EXPERTS):
            pairs = _row_copies(zero_buf, xs_hbm, 0, fill_ref[e], fill_ref[N_EXPERTS + e], MOE_TILE,
                                sems.at[0, e])
            _start_all(pairs)
            _wait_all(pairs)

        def zero_tile(tile, carry):
            dst = xs_hbm.at[pl.ds(pl.multiple_of(tile * MOE_TILE, MOE_TILE), MOE_TILE), :]
            cp = pltpu.make_async_copy(zero_buf, dst, over_sem)
            cp.start()
            cp.wait()
            return carry

        lax.fori_loop(nv_ref[0], n_tiles, zero_tile, 0)


def _scatter(x, g, rank, offs, cnts, fill, n_valid, n_tiles):
    t, d = x.shape
    tm = MOE_TILE
    assert t // tm >= 2, "the segment copies are waited two grid steps after they start"
    est = 2 * tm * d * 4 + (2 * N_EXPERTS * SEG_MAIN + SEG_OVER + MOE_TILE) * d * 2 + 6 * tm * d * 4
    grid_spec = pltpu.PrefetchScalarGridSpec(
        num_scalar_prefetch=4,
        grid=(t // tm,),
        in_specs=[
            pl.BlockSpec((tm, d), lambda i, *_: (i, 0)),
            pl.BlockSpec((1, d), lambda i, *_: (0, 0)),
            pl.BlockSpec((N_EXPERTS, tm), lambda i, *_: (0, i)),
        ],
        out_specs=pl.BlockSpec(memory_space=pl.ANY),
        scratch_shapes=[
            pltpu.VMEM((2, N_EXPERTS, SEG_MAIN, d), BF16),
            pltpu.VMEM((SEG_OVER, d), BF16),
            pltpu.VMEM((MOE_TILE, d), BF16),
            pltpu.SemaphoreType.DMA((2, N_EXPERTS)),
            pltpu.SemaphoreType.DMA(()),
        ],
    )
    return pl.pallas_call(
        functools.partial(_scatter_kernel, n_tiles=n_tiles),
        grid_spec=grid_spec,
        out_shape=jax.ShapeDtypeStruct((n_tiles * MOE_TILE, d), BF16),
        compiler_params=pltpu.CompilerParams(
            dimension_semantics=("arbitrary",), vmem_limit_bytes=_vmem_limit(est)),
        name="scatter",
    )(offs, cnts, fill, n_valid, x, g, rank)


def _expert_changed(te_ref, t):
    return (t == 0) | (te_ref[t] != te_ref[jnp.maximum(t - 1, 0)])


def _expert_up_kernel(te_ref, nv_ref, x_ref, w1_ref, w3_ref, h_ref, w1b, w3b):
    t = pl.program_id(1)

    @pl.when(_expert_changed(te_ref, t))
    def _():
        w1b[...] = w1_ref[...].astype(BF16)
        w3b[...] = w3_ref[...].astype(BF16)

    @pl.when(t < nv_ref[0])
    def _():
        x = x_ref[...]
        a = _dot(x, w1b[...])
        b = _dot(x, w3b[...])
        h_ref[...] = (a * jax.nn.sigmoid(a) * b).astype(BF16)

    @pl.when(t >= nv_ref[0])
    def _():
        h_ref[...] = jnp.zeros_like(h_ref)


def _expert_up(xs, w1, w3, tile_expert, n_valid):
    rows, d = xs.shape
    _, _, f_dim = w1.shape
    tf = UP_COL_TILE
    est = 2 * (MOE_TILE * d * 2 + 2 * d * tf * 4 + MOE_TILE * tf * 2) + 2 * d * tf * 2 + 3 * MOE_TILE * tf * 4
    grid_spec = pltpu.PrefetchScalarGridSpec(
        num_scalar_prefetch=2,
        grid=(f_dim // tf, rows // MOE_TILE),
        in_specs=[
            pl.BlockSpec((MOE_TILE, d), lambda f, t, te, nv: (t, 0)),
            pl.BlockSpec((None, d, tf), lambda f, t, te, nv: (te[t], 0, f)),
            pl.BlockSpec((None, d, tf), lambda f, t, te, nv: (te[t], 0, f)),
        ],
        out_specs=pl.BlockSpec((MOE_TILE, tf), lambda f, t, te, nv: (t, f)),
        scratch_shapes=[pltpu.VMEM((d, tf), BF16), pltpu.VMEM((d, tf), BF16)],
    )
    return pl.pallas_call(
        _expert_up_kernel,
        grid_spec=grid_spec,
        out_shape=jax.ShapeDtypeStruct((rows, f_dim), BF16),
        compiler_params=pltpu.CompilerParams(
            dimension_semantics=("arbitrary", "arbitrary"), vmem_limit_bytes=_vmem_limit(est)),
        name="expert_up",
    )(tile_expert, n_valid, xs, w1, w3)


def _expert_down_kernel(te_ref, nv_ref, h_ref, w2_ref, y_ref, w2b):
    t = pl.program_id(1)

    @pl.when(_expert_changed(te_ref, t))
    def _():
        w2b[...] = w2_ref[...].astype(BF16)

    @pl.when(t < nv_ref[0])
    def _():
        y_ref[...] = _dot(h_ref[...], w2b[...]).astype(BF16)

    @pl.when(t >= nv_ref[0])
    def _():
        y_ref[...] = jnp.zeros_like(y_ref)


def _expert_down(hs, w2, tile_expert, n_valid):
    rows, f_dim = hs.shape
    d = w2.shape[2]
    tn = DOWN_COL_TILE
    est = 2 * (MOE_TILE * f_dim * 2 + f_dim * tn * 4 + MOE_TILE * tn * 2) + f_dim * tn * 2 + MOE_TILE * tn * 4
    grid_spec = pltpu.PrefetchScalarGridSpec(
        num_scalar_prefetch=2,
        grid=(d // tn, rows // MOE_TILE),
        in_specs=[
            pl.BlockSpec((MOE_TILE, f_dim), lambda n, t, te, nv: (t, 0)),
            pl.BlockSpec((None, f_dim, tn), lambda n, t, te, nv: (te[t], 0, n)),
        ],
        out_specs=pl.BlockSpec((MOE_TILE, tn), lambda n, t, te, nv: (t, n)),
        scratch_shapes=[pltpu.VMEM((f_dim, tn), BF16)],
    )
    return pl.pallas_call(
        _expert_down_kernel,
        grid_spec=grid_spec,
        out_shape=jax.ShapeDtypeStruct((rows, d), BF16),
        compiler_params=pltpu.CompilerParams(
            dimension_semantics=("arbitrary", "arbitrary"), vmem_limit_bytes=_vmem_limit(est)),
        name="expert_down",
    )(tile_expert, n_valid, hs, w2)


def _combine_kernel(offs_ref, cnts_ref, x_ref, meta_ref, y_hbm, o_ref,
                    ybuf, over_buf, over_acc, sems, over_sem):
    i = pl.program_id(0)
    tm = x_ref.shape[0]
    total_rows = y_hbm.shape[0]
    meta = meta_ref[...]
    g1 = meta[:, 2 * N_EXPERTS:2 * N_EXPERTS + 1]
    g2 = meta[:, 2 * N_EXPERTS + 1:2 * N_EXPERTS + 2]

    def window(step, e, seg_row0, n_rows):
        off = offs_ref[step * N_EXPERTS + e] + seg_row0
        start = jnp.minimum(off, total_rows - n_rows)
        return start, off - start

    def main_window_copy(step, e):
        start, _ = window(step, e, 0, SEG_MAIN)
        src = y_hbm.at[pl.ds(pl.multiple_of(start, SEG_ALIGN), SEG_MAIN), :]
        return pltpu.make_async_copy(src, ybuf.at[step % 2, e], sems.at[step % 2, e])

    @pl.when(i == 0)
    def _():
        for e in range(N_EXPERTS):
            main_window_copy(i, e).start()

    @pl.when(i + 1 < pl.num_programs(0))
    def _():
        for e in range(N_EXPERTS):
            main_window_copy(i + 1, e).start()

    def weights(e, shift, n_cols):
        c = lax.broadcasted_iota(jnp.int32, (tm, n_cols), 1).astype(F32)
        first = meta[:, e:e + 1] + shift
        second = meta[:, N_EXPERTS + e:N_EXPERTS + e + 1] + shift
        return jnp.where(c == first, g1, jnp.where(c == second, g2, 0.0)).astype(BF16)

    out = x_ref[...]
    any_over = cnts_ref[i * N_EXPERTS] > SEG_MAIN
    for e in range(N_EXPERTS):
        _, shift = window(i, e, 0, SEG_MAIN)
        w_e = weights(e, shift.astype(F32), SEG_MAIN)
        main_window_copy(i, e).wait()
        out = out + _dot(w_e, ybuf[i % 2, e])
        if e > 0:
            any_over = any_over | (cnts_ref[i * N_EXPERTS + e] > SEG_MAIN)
    o_ref[...] = out

    @pl.when(any_over)
    def _():
        over_acc[...] = jnp.zeros_like(over_acc)
        for e in range(N_EXPERTS):
            @pl.when(cnts_ref[i * N_EXPERTS + e] > SEG_MAIN)
            def _():
                start, shift = window(i, e, SEG_MAIN, SEG_OVER)
                src = y_hbm.at[pl.ds(pl.multiple_of(start, SEG_ALIGN), SEG_OVER), :]
                cp = pltpu.make_async_copy(src, over_buf, over_sem)
                cp.start()
                cp.wait()
                over_acc[...] += _dot(weights(e, shift.astype(F32) - float(SEG_MAIN), SEG_OVER), over_buf[...])
        o_ref[...] += over_acc[...]


def _combine(x, meta, ys, offs, cnts):
    t, d = x.shape
    tm = MOE_TILE
    est = (4 * tm * d * 4 + (2 * N_EXPERTS * SEG_MAIN + SEG_OVER) * d * 2 + 2 * tm * d * 4
           + 2 * tm * LANES * 4 + 6 * tm * d * 4)
    grid_spec = pltpu.PrefetchScalarGridSpec(
        num_scalar_prefetch=2,
        grid=(t // tm,),
        in_specs=[
            pl.BlockSpec((tm, d), lambda i, *_: (i, 0)),
            pl.BlockSpec((tm, 3 * N_EXPERTS), lambda i, *_: (i, 0)),
            pl.BlockSpec(memory_space=pl.ANY),
        ],
        out_specs=pl.BlockSpec((tm, d), lambda i, *_: (i, 0)),
        scratch_shapes=[
            pltpu.VMEM((2, N_EXPERTS, SEG_MAIN, d), BF16),
            pltpu.VMEM((SEG_OVER, d), BF16),
            pltpu.VMEM((tm, d), F32),
            pltpu.SemaphoreType.DMA((2, N_EXPERTS)),
            pltpu.SemaphoreType.DMA(()),
        ],
    )
    return pl.pallas_call(
        _combine_kernel,
        grid_spec=grid_spec,
        out_shape=jax.ShapeDtypeStruct((t, d), F32),
        compiler_params=pltpu.CompilerParams(
            dimension_semantics=("arbitrary",), vmem_limit_bytes=_vmem_limit(est)),
        name="combine",
    )(offs, cnts, x, meta, ys)


def _moe(x, g, w_router, w1, w3, w2):
    t, _ = x.shape
    nt = t // MOE_TILE
    n_tiles = _max_expert_tiles(t)
    rank, meta, cnts = _route(x, g, w_router.T)
    cnt2 = cnts.reshape(nt, N_EXPERTS)
    rows_e = jnp.sum(cnt2, axis=0)
    tiles_e = jnp.maximum((rows_e + MOE_TILE - 1) // MOE_TILE, 1)
    ends = jnp.cumsum(tiles_e)
    base_e = (ends - tiles_e) * MOE_TILE
    offs = (base_e[None, :] + jnp.cumsum(cnt2, axis=0) - cnt2).reshape(-1).astype(jnp.int32)
    fill = jnp.concatenate([base_e + rows_e, tiles_e * MOE_TILE - rows_e]).astype(jnp.int32)
    n_valid = ends[-1].reshape(1).astype(jnp.int32)
    tid = jnp.arange(n_tiles, dtype=jnp.int32)
    tile_expert = jnp.minimum(jnp.sum(tid[:, None] >= ends[None, :], axis=1), N_EXPERTS - 1).astype(jnp.int32)
    xs = _scatter(x, g, rank, offs, cnts, fill, n_valid, n_tiles)
    hs = _expert_up(xs, w1, w3, tile_expert, n_valid)
    ys = _expert_down(hs, w2, tile_expert, n_valid)
    return _combine(x, meta, ys, offs, cnts)


def _ffn_kernel(x_ref, g_ref, w1_ref, w3_ref, w2_ref, o_ref):
    x = x_ref[...]
    hb = (_rms(x) * g_ref[...]).astype(BF16)
    f_dim = w1_ref.shape[1]
    out = x
    for c0 in range(0, f_dim, FFN_HIDDEN_CHUNK):
        c1 = min(c0 + FFN_HIDDEN_CHUNK, f_dim)
        a = _dot(hb, w1_ref[:, c0:c1])
        b = _dot(hb, w3_ref[:, c0:c1])
        out = out + _dot((a * jax.nn.sigmoid(a) * b).astype(BF16), w2_ref[c0:c1, :])
    o_ref[...] = out


def _ffn(x, g, w1, w3, w2):
    t, d = x.shape
    f_dim = w1.shape[1]
    tm = FFN_ROW_TILE
    resident = pl.Buffered(1)
    est = 3 * d * f_dim * 2 + 4 * tm * d * 4 + 4 * tm * FFN_HIDDEN_CHUNK * 4
    return pl.pallas_call(
        _ffn_kernel,
        grid=(t // tm,),
        in_specs=[
            pl.BlockSpec((tm, d), lambda i: (i, 0)),
            pl.BlockSpec((1, d), lambda i: (0, 0)),
            pl.BlockSpec((d, f_dim), lambda i: (0, 0), pipeline_mode=resident),
            pl.BlockSpec((d, f_dim), lambda i: (0, 0), pipeline_mode=resident),
            pl.BlockSpec((f_dim, d), lambda i: (0, 0), pipeline_mode=resident),
        ],
        out_specs=pl.BlockSpec((tm, d), lambda i: (i, 0)),
        out_shape=jax.ShapeDtypeStruct((t, d), F32),
        compiler_params=pltpu.CompilerParams(
            dimension_semantics=("parallel",), vmem_limit_bytes=_vmem_limit(est)),
        name="ffn",
    )(x, g, w1, w3, w2)


def _lambda_init(layer):
    return 0.8 - 0.6 * math.exp(-0.3 * layer)


def kernel(x, mem, norm_mix, norm_mem, norm_ffn, w_in, w_gate, b_gate, da_q_norm, da_k_norm,
           da_lam_q1, da_lam_k1, da_lam_q2, da_lam_k2, da_out_norm, pool_w, pool_scale,
           mem_q_norm, mem_k_norm, w_mem_kv, w_br_da, w_br_pool, w_br_mem, w_out,
           ffn_w1, ffn_w3, ffn_w2, moe_router, moe_w1, moe_w3, moe_w2):
    b, s, d = x.shape
    depth = w_in.shape[0]
    xt = x.reshape(b * s, d)
    bf = lambda a: a.astype(BF16)
    row = lambda v: v.reshape(1, -1)
    for l in range(depth):
        lam0 = _lambda_init(l)
        proj = _in_proj(xt, row(norm_mix[l]), w_in, l)
        mem_k, mem_v = _mem_kv(mem, row(norm_mem[l]), bf(w_mem_kv[l]), row(mem_k_norm[l]))
        lam_vecs = jnp.stack([da_lam_q1[l], da_lam_k1[l], da_lam_q2[l], da_lam_k2[l]])
        o_da = _diff_attn(proj.reshape(b, s, -1), lam_vecs, row(da_q_norm[l]), row(da_k_norm[l]),
                          da_out_norm[l].reshape(-1, 1), lam0)
        xt = _mix(xt, proj, o_da.reshape(b * s, -1), mem_k, mem_v, row(norm_mix[l]), bf(w_gate[l]),
                  row(b_gate[l]), bf(pool_w[l]), row(pool_scale[l]), row(mem_q_norm[l]),
                  bf(w_br_da[l]), bf(w_br_pool[l]), bf(w_br_mem[l]), bf(w_out[l]), s)
        j = l // 2
        if l % 2 == 0:
            xt = _ffn(xt, row(norm_ffn[l]), bf(ffn_w1[j]), bf(ffn_w3[j]), bf(ffn_w2[j]))
        else:
            xt = _moe(xt, row(norm_ffn[l]), moe_router[j], moe_w1[j], moe_w3[j], moe_w2[j])
    return xt.reshape(b, s, d)
```

```python
import functools
import math

import jax
import jax.numpy as jnp
import numpy as np
from jax import lax
from jax.experimental import pallas as pl
from jax.experimental.pallas import tpu as pltpu

F32 = jnp.float32
BF16 = jnp.bfloat16

D_MODEL = 1024
CHUNK = 64
RMS_EPS = 1e-6
DA_HEADS = 8
DA_QK_DIM = 64
DA_V_DIM = 128
POOL_WINDOWS = (2, 4, 8, 16)
POOL_GROUP_DIM = 128
POOL_WIDTH = 512
MEM_HEADS = 4
MEM_HEAD_DIM = 128
MEM_WIDTH = 512
N_EXPERTS = 8
LOG2E = 1.4426950408889634
MASK_VALUE = -1e30

V7X_VMEM_BYTES = 64 * 1024 * 1024
VMEM_LIMIT_CAP = 56 * 1024 * 1024
LANES = 128

ROW_TILE = 1024
IN_PROJ_ROW_TILE = 512
N_ATTN_SLABS = 3 * DA_HEADS
ATTN_TILE = 256
POOL_HALO = 16
FFN_ROW_TILE = 512
FFN_HIDDEN_CHUNK = 1536


def _vmem_limit(estimate_bytes):
    return int(min(VMEM_LIMIT_CAP, max(32 * 1024 * 1024, 2 * estimate_bytes)))


def _rms(x):
    return x * lax.rsqrt(jnp.mean(x * x, axis=-1, keepdims=True) + RMS_EPS)


def _dot(a, b):
    return jnp.dot(a, b, preferred_element_type=F32)


def _dot_nt(a, b):
    return lax.dot_general(a, b, (((1,), (1,)), ((), ())), preferred_element_type=F32)


def _in_proj_kernel(x_ref, g_ref, w_ref, heads_ref, rest_ref, wb_scr, *, col_tile):
    n = w_ref.shape[1]
    n_head_cols = heads_ref.shape[0] * LANES

    @pl.when(pl.program_id(0) == 0)
    def _():
        for c0 in range(0, n, col_tile):
            wb_scr[:, c0:c0 + col_tile] = w_ref[:, c0:c0 + col_tile].astype(BF16)

    hb = (_rms(x_ref[...]) * g_ref[...]).astype(BF16)
    for c0 in range(0, n, col_tile):
        res = _dot(hb, wb_scr[:, c0:c0 + col_tile]).astype(BF16)
        if c0 < n_head_cols:
            for j in range(col_tile // LANES):
                heads_ref[c0 // LANES + j] = res[:, j * LANES:(j + 1) * LANES]
        else:
            rest_ref[:, c0 - n_head_cols:c0 - n_head_cols + col_tile] = res


def _in_proj(x, g, w_stack, layer):
    t, d = x.shape
    n = w_stack.shape[2]
    tm = IN_PROJ_ROW_TILE
    est = 2 * (tm * d * 4 + tm * n * 2) + d * n * 6 + 2 * tm * 1024 * 4
    return pl.pallas_call(
        functools.partial(_in_proj_kernel, col_tile=1024),
        grid=(t // tm,),
        in_specs=[
            pl.BlockSpec((tm, d), lambda i: (i, 0)),
            pl.BlockSpec((1, d), lambda i: (0, 0)),
            pl.BlockSpec((None, d, n), lambda i: (layer, 0, 0), pipeline_mode=pl.Buffered(1)),
        ],
        out_specs=[
            pl.BlockSpec((N_ATTN_SLABS, tm, LANES), lambda i: (0, i, 0)),
            pl.BlockSpec((tm, n - N_ATTN_SLABS * LANES), lambda i: (i, 0)),
        ],
        out_shape=[
            jax.ShapeDtypeStruct((N_ATTN_SLABS, t, LANES), BF16),
            jax.ShapeDtypeStruct((t, n - N_ATTN_SLABS * LANES), BF16),
        ],
        scratch_shapes=[pltpu.VMEM((d, n), BF16)],
        compiler_params=pltpu.CompilerParams(
            dimension_semantics=("arbitrary",), vmem_limit_bytes=_vmem_limit(est)),
        name="in_proj",
    )(x, g, w_stack)


def _mem_kv_kernel(mem_ref, g_ref, w_ref, gk_ref, k_ref, v_ref):
    mh = (_rms(mem_ref[...]) * g_ref[...]).astype(BF16)
    kv = _dot(mh, w_ref[...])
    for hh in range(MEM_HEADS):
        c0 = hh * MEM_HEAD_DIM
        k_ref[:, c0:c0 + MEM_HEAD_DIM] = (_rms(kv[:, c0:c0 + MEM_HEAD_DIM]) * gk_ref[...]).astype(BF16)
    v_ref[...] = kv[:, MEM_WIDTH:].astype(BF16)


def _mem_kv(mem, g, w_bf16, gk):
    b, m, d = mem.shape
    out = jax.ShapeDtypeStruct((b, m, MEM_WIDTH), BF16)
    return pl.pallas_call(
        _mem_kv_kernel,
        grid=(b,),
        in_specs=[
            pl.BlockSpec((None, m, d), lambda i: (i, 0, 0)),
            pl.BlockSpec((1, d), lambda i: (0, 0)),
            pl.BlockSpec((d, 2 * MEM_WIDTH), lambda i: (0, 0)),
            pl.BlockSpec((1, MEM_HEAD_DIM), lambda i: (0, 0)),
        ],
        out_specs=[pl.BlockSpec((None, m, MEM_WIDTH), lambda i: (i, 0, 0))] * 2,
        out_shape=[out, out],
        compiler_params=pltpu.CompilerParams(dimension_semantics=("parallel",)),
        name="mem_kv",
    )(mem, g, w_bf16, gk)


def _diff_attn_kernel(lam_ref, pos_ref, q_ref, k_ref, v_ref, gq_ref, gk_ref, go_ref, o_ref,
                      qa_scr, ka_scr, vt_scr, s_scr, *, lam_init, seq):
    tq = ATTN_TILE
    head = pl.program_id(1)
    slope = jnp.exp2(-(jnp.zeros((1, LANES), F32) + (head + 1).astype(F32)))[:, :1] * LOG2E

    lam_v = lam_ref[...]
    lam = (jnp.exp(jnp.sum(lam_v[0:1] * lam_v[1:2], axis=-1, keepdims=True))
           - jnp.exp(jnp.sum(lam_v[2:3] * lam_v[3:4], axis=-1, keepdims=True)) + lam_init)

    r_i = lax.broadcasted_iota(jnp.int32, (LANES, LANES), 0) // DA_QK_DIM
    c_i = lax.broadcasted_iota(jnp.int32, (LANES, LANES), 1) // DA_QK_DIM
    group_mean = jnp.where(r_i == c_i, 1.0 / DA_QK_DIM, 0.0).astype(BF16)

    def qk_norm(x_bf16, gain):
        xf = x_bf16.astype(F32)
        ms = _dot((xf * xf).astype(BF16), group_mean)
        return xf * (lax.rsqrt(ms + RMS_EPS) * gain)

    lane = lax.broadcasted_iota(jnp.int32, (tq, LANES), 1)
    in_map = (lane < DA_QK_DIM, lane >= DA_QK_DIM)
    lane_row = lax.broadcasted_iota(jnp.int32, (1, LANES), 1)
    ones_at_pieces = jnp.where(lane_row < ALIBI_PIECES, 1.0, 0.0)
    pow2 = jnp.exp2(-(jnp.zeros((1, LANES), F32) + (head + 1).astype(F32)))
    q_gain = gq_ref[...] * (DA_QK_DIM ** -0.5 * LOG2E)
    n_blocks = seq // tq

    def prepare(blk):
        rows = slice(blk * tq, (blk + 1) * tq)
        qn = qk_norm(q_ref[rows, :], q_gain)
        for m in range(2):
            q_rows = slice((2 * blk + m) * tq, (2 * blk + m + 1) * tq)
            qa_scr[q_rows, :LANES] = jnp.where(in_map[m], qn, 0.0).astype(BF16)
            qa_scr[q_rows, LANES:] = jnp.broadcast_to(ones_at_pieces, (tq, LANES)).astype(BF16)
        ka_scr[rows, :LANES] = qk_norm(k_ref[rows, :], gk_ref[...]).astype(BF16)
        ka_scr[rows, LANES:] = (pos_ref[rows, :] * pow2).astype(BF16)
        vt_scr[0:DA_V_DIM, rows] = v_ref[rows, :].astype(F32).T.astype(BF16)
        vt_scr[DA_V_DIM:, rows] = jnp.ones((ATTN_ONES_ROWS, tq), BF16)

    krel = lax.broadcasted_iota(jnp.int32, (tq, 2 * tq), 0)
    qrel = lax.broadcasted_iota(jnp.int32, (tq, 2 * tq), 1) % tq
    allowed = (krel // CHUNK) <= (qrel // CHUNK)
    diag_corr = slope * (-2.0 * jnp.maximum(krel - qrel, 0).astype(F32))

    def col_max_of(s):
        mx = s[0:8, :]
        for r0 in range(8, s.shape[0], 8):
            mx = jnp.maximum(mx, s[r0:r0 + 8, :])
        return jnp.max(mx, axis=0, keepdims=True)

    def scores(i):
        q_i = qa_scr[2 * i * tq:2 * (i + 1) * tq, :]
        diag = slice(i * tq, (i + 1) * tq)
        s = jnp.where(allowed, _dot_nt(ka_scr[diag, :], q_i) + diag_corr, MASK_VALUE)
        s_scr[i % ATTN_SCORE_BUFFERS, diag, :] = s
        cmax = col_max_of(s)
        if i > 0:
            s = _dot_nt(ka_scr[0:i * tq, :], q_i)
            s_scr[i % ATTN_SCORE_BUFFERS, 0:i * tq, :] = s
            cmax = jnp.maximum(cmax, col_max_of(s))
        return cmax

    def values(i, col_max):
        n_keys = (i + 1) * tq
        p = jnp.exp2(s_scr[i % ATTN_SCORE_BUFFERS, 0:n_keys, :] - col_max).astype(BF16)
        acc = _dot(vt_scr[:, 0:n_keys], p)
        ratio = acc[:DA_V_DIM] / acc[DA_V_DIM:DA_V_DIM + 1]
        o = ratio[:, :tq] - lam * ratio[:, tq:]
        o = o * lax.rsqrt(jnp.mean(o * o, axis=0, keepdims=True) + RMS_EPS) * go_ref[...] * (1.0 - lam_init)
        o_ref[i * tq:(i + 1) * tq, :] = o.T.astype(BF16)

    ahead = ATTN_SCORE_BUFFERS - 1
    col_max = {}
    prepared = [0]

    def prepare_through(blk):
        while prepared[0] <= min(blk, n_blocks - 1):
            prepare(prepared[0])
            prepared[0] += 1

    prepare_through(ATTN_PREP_LEAD - 1)
    for blk in range(min(ahead, n_blocks)):
        prepare_through(blk + ATTN_PREP_LEAD)
        col_max[blk] = scores(blk)
    for i in range(n_blocks):
        prepare_through(i + ahead + ATTN_PREP_LEAD)
        if i + ahead < n_blocks:
            col_max[i + ahead] = scores(i + ahead)
        values(i, col_max.pop(i))


ATTN_ONES_ROWS = 16
ATTN_SCORE_BUFFERS = 3
ATTN_PREP_LEAD = 2
ALIBI_PIECES = 4


def _alibi_position_table(seq):
    rest = np.arange(seq, dtype=np.float64) * LOG2E
    table = np.zeros((seq, LANES), np.float32)
    for p in range(ALIBI_PIECES):
        piece = rest.astype(np.float32).astype(BF16).astype(np.float32)
        table[:, p] = piece
        rest = rest - piece.astype(np.float64)
    return jnp.asarray(table)


def _diff_attn(heads4, lam_vecs, gq, gk, go_col, lam_init):
    _, b, s, _ = heads4.shape
    hd = 2 * DA_QK_DIM
    kern = functools.partial(_diff_attn_kernel, lam_init=lam_init, seq=s)
    est = (2 * 4 * s * hd * 2 + 4 * s * hd * 2 + (DA_V_DIM + ATTN_ONES_ROWS) * s * 2 + 2 * ATTN_SCORE_BUFFERS * ATTN_TILE * s * 4
           + 8 * s * hd * 4)
    return pl.pallas_call(
        kern,
        grid=(b, DA_HEADS),
        in_specs=[
            pl.BlockSpec((4, DA_QK_DIM), lambda bi, h: (0, 0)),
            pl.BlockSpec((s, LANES), lambda bi, h: (0, 0)),
            pl.BlockSpec((None, None, s, hd), lambda bi, h: (h, bi, 0, 0)),
            pl.BlockSpec((None, None, s, hd), lambda bi, h: (DA_HEADS + h, bi, 0, 0)),
            pl.BlockSpec((None, None, s, DA_V_DIM), lambda bi, h: (2 * DA_HEADS + h, bi, 0, 0)),
            pl.BlockSpec((1, hd), lambda bi, h: (0, 0)),
            pl.BlockSpec((1, hd), lambda bi, h: (0, 0)),
            pl.BlockSpec((DA_V_DIM, 1), lambda bi, h: (0, 0)),
        ],
        out_specs=pl.BlockSpec((None, s, DA_V_DIM), lambda bi, h: (bi, 0, h)),
        out_shape=jax.ShapeDtypeStruct((b, s, DA_HEADS * DA_V_DIM), BF16),
        scratch_shapes=[
            pltpu.VMEM((2 * s, 2 * hd), BF16),
            pltpu.VMEM((s, 2 * hd), BF16),
            pltpu.VMEM((DA_V_DIM + ATTN_ONES_ROWS, s), BF16),
            pltpu.VMEM((ATTN_SCORE_BUFFERS, s, 2 * ATTN_TILE), F32),
        ],
        compiler_params=pltpu.CompilerParams(
            dimension_semantics=("parallel", "parallel"), vmem_limit_bytes=_vmem_limit(est)),
        name="diff_attn",
    )(lam_vecs, _alibi_position_table(s), heads4, heads4, heads4, gq, gk, go_col)


def _mix_kernel(x_ref, pool_ref, halo_ref, mq_ref, oda_ref, mk_ref, mv_ref,
                gmix_ref, wg_ref, bg_ref, wpool_ref, pscale_ref, gmq_ref,
                wda_ref, wbp_ref, wbm_ref, wout_ref, o_ref, *, tiles_per_seq):
    tm = x_ref.shape[0]
    d = D_MODEL
    x = x_ref[...]
    hb = (_rms(x) * gmix_ref[...]).astype(BF16)

    tile_in_seq = pl.program_id(0) % tiles_per_seq
    p_cur = pool_ref[...].astype(F32)
    halo = jnp.where(tile_in_seq == 0, 0.0, halo_ref[...].astype(F32))
    ext = jnp.concatenate([halo, p_cur], axis=0)
    t_pos = tile_in_seq * tm + lax.broadcasted_iota(jnp.int32, (tm, 1), 0)
    pool_parts = []
    for g, w in enumerate(POOL_WINDOWS):
        c0 = g * POOL_GROUP_DIM
        s = ext[:, c0:c0 + POOL_GROUP_DIM]
        sh = 1
        while sh < w:
            s = s + pltpu.roll(s, sh, axis=0)
            sh *= 2
        cnt = jnp.minimum(t_pos + 1, w).astype(F32)
        dlt = (s[POOL_HALO:] / cnt - p_cur[:, c0:c0 + POOL_GROUP_DIM]).astype(BF16)
        pool_parts.append(_dot(dlt, wpool_ref[g]))
    o_pool = (jnp.concatenate(pool_parts, axis=1) * pscale_ref[...]).astype(BF16)

    mq = mq_ref[...].astype(F32)
    mem_parts = []
    for hh in range(MEM_HEADS):
        c0 = hh * MEM_HEAD_DIM
        qh = (_rms(mq[:, c0:c0 + MEM_HEAD_DIM]) * gmq_ref[...] * (MEM_HEAD_DIM ** -0.5)).astype(BF16)
        s = _dot_nt(qh, mk_ref[:, c0:c0 + MEM_HEAD_DIM])
        p = jnp.exp(s - jnp.max(s, axis=-1, keepdims=True))
        l = jnp.sum(p, axis=-1, keepdims=True)
        mem_parts.append(_dot(p.astype(BF16), mv_ref[:, c0:c0 + MEM_HEAD_DIM]) / l)
    o_mem = jnp.concatenate(mem_parts, axis=1).astype(BF16)

    def gate(k):
        return jax.nn.sigmoid(_dot(hb, wg_ref[:, k * d:(k + 1) * d]) + bg_ref[:, k * d:(k + 1) * d])

    merged = gate(0) * _dot(oda_ref[...], wda_ref[...])
    merged = merged + gate(1) * _dot(o_pool, wbp_ref[...])
    merged = merged + gate(2) * _dot(o_mem, wbm_ref[...])
    o_ref[...] = x + _dot(merged.astype(BF16), wout_ref[...])


def _mix(x, proj, o_da, mem_k, mem_v, gmix, wg, bg, wpool, pscale, gmq, wda, wbp, wbm, wout, seq):
    t, d = x.shape
    tm = ROW_TILE
    tiles_per_seq = seq // tm
    halo_blocks_per_tile = tm // POOL_HALO
    pool_col = (proj.shape[1] - POOL_WIDTH - MEM_WIDTH) // POOL_WIDTH
    mem_len = mem_k.shape[1]
    const = lambda i: (0, 0)
    weights_bytes = 2 * (wg.size + wda.size + wbp.size + wbm.size + wout.size + wpool.size)
    est = 2 * weights_bytes + 2 * (2 * tm * d * 4 + tm * (d + 2 * POOL_WIDTH) * 2) + 8 * tm * d * 4
    return pl.pallas_call(
        functools.partial(_mix_kernel, tiles_per_seq=tiles_per_seq),
        grid=(t // tm,),
        in_specs=[
            pl.BlockSpec((tm, d), lambda i: (i, 0)),
            pl.BlockSpec((tm, POOL_WIDTH), lambda i: (i, pool_col)),
            pl.BlockSpec((POOL_HALO, POOL_WIDTH),
                         lambda i: (jnp.maximum(i * halo_blocks_per_tile - 1, 0), pool_col)),
            pl.BlockSpec((tm, MEM_WIDTH), lambda i: (i, pool_col + 1)),
            pl.BlockSpec((tm, d), lambda i: (i, 0)),
            pl.BlockSpec((None, mem_len, MEM_WIDTH), lambda i: (i // tiles_per_seq, 0, 0)),
            pl.BlockSpec((None, mem_len, MEM_WIDTH), lambda i: (i // tiles_per_seq, 0, 0)),
            pl.BlockSpec((1, d), const),
            pl.BlockSpec(wg.shape, const, pipeline_mode=pl.Buffered(1)),
            pl.BlockSpec((1, bg.shape[1]), const),
            pl.BlockSpec(wpool.shape, lambda i: (0, 0, 0)),
            pl.BlockSpec((1, POOL_WIDTH), const),
            pl.BlockSpec((1, MEM_HEAD_DIM), const),
            pl.BlockSpec(wda.shape, const, pipeline_mode=pl.Buffered(1)),
            pl.BlockSpec(wbp.shape, const, pipeline_mode=pl.Buffered(1)),
            pl.BlockSpec(wbm.shape, const, pipeline_mode=pl.Buffered(1)),
            pl.BlockSpec(wout.shape, const, pipeline_mode=pl.Buffered(1)),
        ],
        out_specs=pl.BlockSpec((tm, d), lambda i: (i, 0)),
        out_shape=jax.ShapeDtypeStruct((t, d), F32),
        compiler_params=pltpu.CompilerParams(
            dimension_semantics=("parallel",), vmem_limit_bytes=_vmem_limit(est)),
        name="mix",
    )(x, proj, proj, proj, o_da, mem_k, mem_v, gmix, wg, bg, wpool, pscale, gmq, wda, wbp, wbm, wout)


def _split3(x):
    hi = x.astype(BF16)
    r = x - hi.astype(F32)
    mid = r.astype(BF16)
    lo = (r - mid.astype(F32)).astype(BF16)
    return hi, mid, lo


MOE_TILE = 512
SEG_MAIN = 192
SEG_OVER = MOE_TILE - SEG_MAIN
SEG_ALIGN = 16
NOT_ROUTED = -1e6
UP_COL_TILE = 1792
DOWN_COL_TILE = 1024
COPY_SIZES = (512, 256, 128, 64, 32, 16)


def _max_expert_tiles(t):
    padded_rows = 2 * t + (t // MOE_TILE) * N_EXPERTS * (SEG_ALIGN - 1)
    return padded_rows // MOE_TILE + N_EXPERTS


def _row_copies(src, dst, src_row0, dst_row0, n_rows, max_rows, sem):
    pairs = []
    pos = 0
    for size in COPY_SIZES:
        if size > max_rows:
            continue
        cond = jnp.bitwise_and(n_rows, size) != 0
        cp = pltpu.make_async_copy(
            src.at[pl.ds(pl.multiple_of(src_row0 + pos, SEG_ALIGN), size), :],
            dst.at[pl.ds(pl.multiple_of(dst_row0 + pos, SEG_ALIGN), size), :], sem)
        pairs.append((cond, cp))
        pos = pos + jnp.where(cond, size, 0)
    return pairs


def _start_all(pairs):
    for cond, cp in pairs:
        pl.when(cond)(cp.start)


def _wait_all(pairs):
    for cond, cp in pairs:
        pl.when(cond)(cp.wait)


def _onehot_rows(rank_row, n_rows, row0):
    r = lax.broadcasted_iota(jnp.int32, (n_rows, rank_row.shape[1]), 0).astype(F32) + float(row0)
    return jnp.where(r == rank_row, 1.0, 0.0).astype(BF16)


def _route_kernel(x_ref, g_ref, wr_ref, rank_ref, meta_ref, cnts_ref):
    i = pl.program_id(0)
    tm = x_ref.shape[0]
    h2 = _rms(x_ref[...]) * g_ref[...]
    h_hi, h_mid, _ = _split3(h2)
    w_hi, w_mid, _ = _split3(wr_ref[...])
    logits = _dot_nt(w_hi, h_hi) + (_dot_nt(w_hi, h_mid) + _dot_nt(w_mid, h_hi))
    eidx = lax.broadcasted_iota(jnp.int32, logits.shape, 0).astype(F32)
    m1 = jnp.max(logits, axis=0, keepdims=True)
    i1 = jnp.min(jnp.where(logits == m1, eidx, float(N_EXPERTS)), axis=0, keepdims=True)
    sel1 = eidx == i1
    rest = jnp.where(sel1, -jnp.inf, logits)
    m2 = jnp.max(rest, axis=0, keepdims=True)
    i2 = jnp.min(jnp.where(rest == m2, eidx, float(N_EXPERTS)), axis=0, keepdims=True)
    sel2 = eidx == i2
    e2 = jnp.exp(m2 - m1)
    g1 = 1.0 / (1.0 + e2)
    g2 = e2 / (1.0 + e2)

    mask = jnp.where(sel1 | sel2, 1.0, 0.0)
    before = (lax.broadcasted_iota(jnp.int32, (tm, tm), 0)
              < lax.broadcasted_iota(jnp.int32, (tm, tm), 1)).astype(BF16)
    rank = _dot(mask.astype(BF16), before)
    counts = jnp.sum(mask, axis=1, keepdims=True)
    rank_m = jnp.where(mask > 0.5, rank, NOT_ROUTED)

    rk1 = jnp.where(sel1, rank, NOT_ROUTED)
    rk2 = jnp.where(sel2, rank, NOT_ROUTED)
    gates = jnp.where(eidx == 0.0, g1, jnp.where(eidx == 1.0, g2, 0.0))
    packed = jnp.concatenate([rk1, rk2, gates], axis=0)
    ident = (lax.broadcasted_iota(jnp.int32, (tm, tm), 0)
             == lax.broadcasted_iota(jnp.int32, (tm, tm), 1)).astype(BF16)
    p_hi, p_mid, p_lo = _split3(packed)
    meta_ref[...] = _dot_nt(ident, p_hi) + (_dot_nt(ident, p_mid) + _dot_nt(ident, p_lo))

    rank_ref[...] = rank_m
    for e in range(N_EXPERTS):
        cnt = counts[e, 0].astype(jnp.int32)
        cnts_ref[i * N_EXPERTS + e] = jnp.bitwise_and(cnt + (SEG_ALIGN - 1), -SEG_ALIGN)


def _route(x, g, w_router_t):
    t, d = x.shape
    tm = MOE_TILE
    nt = t // tm
    est = 2 * tm * d * 4 + 6 * tm * tm * 4 + 4 * tm * d * 4
    return pl.pallas_call(
        _route_kernel,
        grid=(nt,),
        in_specs=[
            pl.BlockSpec((tm, d), lambda i: (i, 0)),
            pl.BlockSpec((1, d), lambda i: (0, 0)),
            pl.BlockSpec((N_EXPERTS, d), lambda i: (0, 0)),
        ],
        out_specs=[
            pl.BlockSpec((N_EXPERTS, tm), lambda i: (0, i)),
            pl.BlockSpec((tm, 3 * N_EXPERTS), lambda i: (i, 0)),
            pl.BlockSpec(memory_space=pltpu.SMEM),
        ],
        out_shape=[
            jax.ShapeDtypeStruct((N_EXPERTS, t), F32),
            jax.ShapeDtypeStruct((t, 3 * N_EXPERTS), F32),
            jax.ShapeDtypeStruct((nt * N_EXPERTS,), jnp.int32),
        ],
        compiler_params=pltpu.CompilerParams(
            dimension_semantics=("arbitrary",), vmem_limit_bytes=_vmem_limit(est)),
        name="route",
    )(x, g, w_router_t)


def _scatter_kernel(offs_ref, cnts_ref, fill_ref, nv_ref, x_ref, g_ref, rank_ref, xs_hbm,
                    stage, over_stage, zero_buf, sems, over_sem, *, n_tiles):
    i = pl.program_id(0)
    hb = (_rms(x_ref[...]) * g_ref[...]).astype(BF16)
    rank_m = rank_ref[...]

    def main_copies(step, e):
        n_main = jnp.minimum(cnts_ref[step * N_EXPERTS + e], SEG_MAIN)
        return _row_copies(stage.at[step % 2, e], xs_hbm, 0, offs_ref[step * N_EXPERTS + e], n_main,
                           SEG_MAIN, sems.at[step % 2, e])

    @pl.when(i > 1)
    def _():
        for e in range(N_EXPERTS):
            _wait_all(main_copies(i - 2, e))

    for e in range(N_EXPERTS):
        rank_e = rank_m[e:e + 1, :]
        stage[i % 2, e] = _dot(_onehot_rows(rank_e, SEG_MAIN, 0), hb).astype(BF16)
        _start_all(main_copies(i, e))
        n_over = cnts_ref[i * N_EXPERTS + e] - SEG_MAIN

        @pl.when(n_over > 0)
        def _():
            over_stage[...] = _dot(_onehot_rows(rank_e, SEG_OVER, SEG_MAIN), hb).astype(BF16)
            pairs = _row_copies(over_stage, xs_hbm, 0, offs_ref[i * N_EXPERTS + e] + SEG_MAIN, n_over,
                                SEG_OVER, over_sem)
            _start_all(pairs)
            _wait_all(pairs)

    @pl.when(i == pl.num_programs(0) - 1)
    def _():
        for e in range(N_EXPERTS):
            _wait_all(main_copies(i - 1, e))
            _wait_all(main_copies(i, e))
        zero_buf[...] = jnp.zeros_like(zero_buf)
        for e in range(N_EXPERTS):
            pairs = _row_copies(zero_buf, xs_hbm, 0, fill_ref[e], fill_ref[N_EXPERTS + e], MOE_TILE,
                                sems.at[0, e])
            _start_all(pairs)
            _wait_all(pairs)

        def zero_tile(tile, carry):
            dst = xs_hbm.at[pl.ds(pl.multiple_of(tile * MOE_TILE, MOE_TILE), MOE_TILE), :]
            cp = pltpu.make_async_copy(zero_buf, dst, over_sem)
            cp.start()
            cp.wait()
            return carry

        lax.fori_loop(nv_ref[0], n_tiles, zero_tile, 0)


def _scatter(x, g, rank, offs, cnts, fill, n_valid, n_tiles):
    t, d = x.shape
    tm = MOE_TILE
    assert t // tm >= 2, "the segment copies are waited two grid steps after they start"
    est = 2 * tm * d * 4 + (2 * N_EXPERTS * SEG_MAIN + SEG_OVER + MOE_TILE) * d * 2 + 6 * tm * d * 4
    grid_spec = pltpu.PrefetchScalarGridSpec(
        num_scalar_prefetch=4,
        grid=(t // tm,),
        in_specs=[
            pl.BlockSpec((tm, d), lambda i, *_: (i, 0)),
            pl.BlockSpec((1, d), lambda i, *_: (0, 0)),
            pl.BlockSpec((N_EXPERTS, tm), lambda i, *_: (0, i)),
        ],
        out_specs=pl.BlockSpec(memory_space=pl.ANY),
        scratch_shapes=[
            pltpu.VMEM((2, N_EXPERTS, SEG_MAIN, d), BF16),
            pltpu.VMEM((SEG_OVER, d), BF16),
            pltpu.VMEM((MOE_TILE, d), BF16),
            pltpu.SemaphoreType.DMA((2, N_EXPERTS)),
            pltpu.SemaphoreType.DMA(()),
        ],
    )
    return pl.pallas_call(
        functools.partial(_scatter_kernel, n_tiles=n_tiles),
        grid_spec=grid_spec,
        out_shape=jax.ShapeDtypeStruct((n_tiles * MOE_TILE, d), BF16),
        compiler_params=pltpu.CompilerParams(
            dimension_semantics=("arbitrary",), vmem_limit_bytes=_vmem_limit(est)),
        name="scatter",
    )(offs, cnts, fill, n_valid, x, g, rank)


def _expert_changed(te_ref, t):
    return (t == 0) | (te_ref[t] != te_ref[jnp.maximum(t - 1, 0)])


def _expert_up_kernel(te_ref, nv_ref, x_ref, w1_ref, w3_ref, h_ref, w1b, w3b):
    t = pl.program_id(1)

    @pl.when(_expert_changed(te_ref, t))
    def _():
        w1b[...] = w1_ref[...].astype(BF16)
        w3b[...] = w3_ref[...].astype(BF16)

    @pl.when(t < nv_ref[0])
    def _():
        x = x_ref[...]
        a = _dot(x, w1b[...])
        b = _dot(x, w3b[...])
        h_ref[...] = (a * jax.nn.sigmoid(a) * b).astype(BF16)

    @pl.when(t >= nv_ref[0])
    def _():
        h_ref[...] = jnp.zeros_like(h_ref)


def _expert_up(xs, w1, w3, tile_expert, n_valid):
    rows, d = xs.shape
    _, _, f_dim = w1.shape
    tf = UP_COL_TILE
    est = 2 * (MOE_TILE * d * 2 + 2 * d * tf * 4 + MOE_TILE * tf * 2) + 2 * d * tf * 2 + 3 * MOE_TILE * tf * 4
    grid_spec = pltpu.PrefetchScalarGridSpec(
        num_scalar_prefetch=2,
        grid=(f_dim // tf, rows // MOE_TILE),
        in_specs=[
            pl.BlockSpec((MOE_TILE, d), lambda f, t, te, nv: (t, 0)),
            pl.BlockSpec((None, d, tf), lambda f, t, te, nv: (te[t], 0, f)),
            pl.BlockSpec((None, d, tf), lambda f, t, te, nv: (te[t], 0, f)),
        ],
        out_specs=pl.BlockSpec((MOE_TILE, tf), lambda f, t, te, nv: (t, f)),
        scratch_shapes=[pltpu.VMEM((d, tf), BF16), pltpu.VMEM((d, tf), BF16)],
    )
    return pl.pallas_call(
        _expert_up_kernel,
        grid_spec=grid_spec,
        out_shape=jax.ShapeDtypeStruct((rows, f_dim), BF16),
        compiler_params=pltpu.CompilerParams(
            dimension_semantics=("arbitrary", "arbitrary"), vmem_limit_bytes=_vmem_limit(est)),
        name="expert_up",
    )(tile_expert, n_valid, xs, w1, w3)


def _expert_down_kernel(te_ref, nv_ref, h_ref, w2_ref, y_ref, w2b):
    t = pl.program_id(1)

    @pl.when(_expert_changed(te_ref, t))
    def _():
        w2b[...] = w2_ref[...].astype(BF16)

    @pl.when(t < nv_ref[0])
    def _():
        y_ref[...] = _dot(h_ref[...], w2b[...]).astype(BF16)

    @pl.when(t >= nv_ref[0])
    def _():
        y_ref[...] = jnp.zeros_like(y_ref)


def _expert_down(hs, w2, tile_expert, n_valid):
    rows, f_dim = hs.shape
    d = w2.shape[2]
    tn = DOWN_COL_TILE
    est = 2 * (MOE_TILE * f_dim * 2 + f_dim * tn * 4 + MOE_TILE * tn * 2) + f_dim * tn * 2 + MOE_TILE * tn * 4
    grid_spec = pltpu.PrefetchScalarGridSpec(
        num_scalar_prefetch=2,
        grid=(d // tn, rows // MOE_TILE),
        in_specs=[
            pl.BlockSpec((MOE_TILE, f_dim), lambda n, t, te, nv: (t, 0)),
            pl.BlockSpec((None, f_dim, tn), lambda n, t, te, nv: (te[t], 0, n)),
        ],
        out_specs=pl.BlockSpec((MOE_TILE, tn), lambda n, t, te, nv: (t, n)),
        scratch_shapes=[pltpu.VMEM((f_dim, tn), BF16)],
    )
    return pl.pallas_call(
        _expert_down_kernel,
        grid_spec=grid_spec,
        out_shape=jax.ShapeDtypeStruct((rows, d), BF16),
        compiler_params=pltpu.CompilerParams(
            dimension_semantics=("arbitrary", "arbitrary"), vmem_limit_bytes=_vmem_limit(est)),
        name="expert_down",
    )(tile_expert, n_valid, hs, w2)


def _combine_kernel(offs_ref, cnts_ref, x_ref, meta_ref, y_hbm, o_ref,
                    ybuf, over_buf, over_acc, sems, over_sem):
    i = pl.program_id(0)
    tm = x_ref.shape[0]
    total_rows = y_hbm.shape[0]
    meta = meta_ref[...]
    g1 = meta[:, 2 * N_EXPERTS:2 * N_EXPERTS + 1]
    g2 = meta[:, 2 * N_EXPERTS + 1:2 * N_EXPERTS + 2]

    def window(step, e, seg_row0, n_rows):
        off = offs_ref[step * N_EXPERTS + e] + seg_row0
        start = jnp.minimum(off, total_rows - n_rows)
        return start, off - start

    def main_window_copy(step, e):
        start, _ = window(step, e, 0, SEG_MAIN)
        src = y_hbm.at[pl.ds(pl.multiple_of(start, SEG_ALIGN), SEG_MAIN), :]
        return pltpu.make_async_copy(src, ybuf.at[step % 2, e], sems.at[step % 2, e])

    @pl.when(i == 0)
    def _():
        for e in range(N_EXPERTS):
            main_window_copy(i, e).start()

    @pl.when(i + 1 < pl.num_programs(0))
    def _():
        for e in range(N_EXPERTS):
            main_window_copy(i + 1, e).start()

    def weights(e, shift, n_cols):
        c = lax.broadcasted_iota(jnp.int32, (tm, n_cols), 1).astype(F32)
        first = meta[:, e:e + 1] + shift
        second = meta[:, N_EXPERTS + e:N_EXPERTS + e + 1] + shift
        return jnp.where(c == first, g1, jnp.where(c == second, g2, 0.0)).astype(BF16)

    out = x_ref[...]
    any_over = cnts_ref[i * N_EXPERTS] > SEG_MAIN
    for e in range(N_EXPERTS):
        _, shift = window(i, e, 0, SEG_MAIN)
        w_e = weights(e, shift.astype(F32), SEG_MAIN)
        main_window_copy(i, e).wait()
        out = out + _dot(w_e, ybuf[i % 2, e])
        if e > 0:
            any_over = any_over | (cnts_ref[i * N_EXPERTS + e] > SEG_MAIN)
    o_ref[...] = out

    @pl.when(any_over)
    def _():
        over_acc[...] = jnp.zeros_like(over_acc)
        for e in range(N_EXPERTS):
            @pl.when(cnts_ref[i * N_EXPERTS + e] > SEG_MAIN)
            def _():
                start, shift = window(i, e, SEG_MAIN, SEG_OVER)
                src = y_hbm.at[pl.ds(pl.multiple_of(start, SEG_ALIGN), SEG_OVER), :]
                cp = pltpu.make_async_copy(src, over_buf, over_sem)
                cp.start()
                cp.wait()
                over_acc[...] += _dot(weights(e, shift.astype(F32) - float(SEG_MAIN), SEG_OVER), over_buf[...])
        o_ref[...] += over_acc[...]


def _combine(x, meta, ys, offs, cnts):
    t, d = x.shape
    tm = MOE_TILE
    est = (4 * tm * d * 4 + (2 * N_EXPERTS * SEG_MAIN + SEG_OVER) * d * 2 + 2 * tm * d * 4
           + 2 * tm * LANES * 4 + 6 * tm * d * 4)
    grid_spec = pltpu.PrefetchScalarGridSpec(
        num_scalar_prefetch=2,
        grid=(t // tm,),
        in_specs=[
            pl.BlockSpec((tm, d), lambda i, *_: (i, 0)),
            pl.BlockSpec((tm, 3 * N_EXPERTS), lambda i, *_: (i, 0)),
            pl.BlockSpec(memory_space=pl.ANY),
        ],
        out_specs=pl.BlockSpec((tm, d), lambda i, *_: (i, 0)),
        scratch_shapes=[
            pltpu.VMEM((2, N_EXPERTS, SEG_MAIN, d), BF16),
            pltpu.VMEM((SEG_OVER, d), BF16),
            pltpu.VMEM((tm, d), F32),
            pltpu.SemaphoreType.DMA((2, N_EXPERTS)),
            pltpu.SemaphoreType.DMA(()),
        ],
    )
    return pl.pallas_call(
        _combine_kernel,
        grid_spec=grid_spec,
        out_shape=jax.ShapeDtypeStruct((t, d), F32),
        compiler_params=pltpu.CompilerParams(
            dimension_semantics=("arbitrary",), vmem_limit_bytes=_vmem_limit(est)),
        name="combine",
    )(offs, cnts, x, meta, ys)


def _moe(x, g, w_router, w1, w3, w2):
    t, _ = x.shape
    nt = t // MOE_TILE
    n_tiles = _max_expert_tiles(t)
    rank, meta, cnts = _route(x, g, w_router.T)
    cnt2 = cnts.reshape(nt, N_EXPERTS)
    rows_e = jnp.sum(cnt2, axis=0)
    tiles_e = jnp.maximum((rows_e + MOE_TILE - 1) // MOE_TILE, 1)
    ends = jnp.cumsum(tiles_e)
    base_e = (ends - tiles_e) * MOE_TILE
    offs = (base_e[None, :] + jnp.cumsum(cnt2, axis=0) - cnt2).reshape(-1).astype(jnp.int32)
    fill = jnp.concatenate([base_e + rows_e, tiles_e * MOE_TILE - rows_e]).astype(jnp.int32)
    n_valid = ends[-1].reshape(1).astype(jnp.int32)
    tid = jnp.arange(n_tiles, dtype=jnp.int32)
    tile_expert = jnp.minimum(jnp.sum(tid[:, None] >= ends[None, :], axis=1), N_EXPERTS - 1).astype(jnp.int32)
    xs = _scatter(x, g, rank, offs, cnts, fill, n_valid, n_tiles)
    hs = _expert_up(xs, w1, w3, tile_expert, n_valid)
    ys = _expert_down(hs, w2, tile_expert, n_valid)
    return _combine(x, meta, ys, offs, cnts)


def _ffn_kernel(x_ref, g_ref, w1_ref, w3_ref, w2_ref, o_ref):
    x = x_ref[...]
    hb = (_rms(x) * g_ref[...]).astype(BF16)
    f_dim = w1_ref.shape[1]
    out = x
    for c0 in range(0, f_dim, FFN_HIDDEN_CHUNK):
        c1 = min(c0 + FFN_HIDDEN_CHUNK, f_dim)
        a = _dot(hb, w1_ref[:, c0:c1])
        b = _dot(hb, w3_ref[:, c0:c1])
        out = out + _dot((a * jax.nn.sigmoid(a) * b).astype(BF16), w2_ref[c0:c1, :])
    o_ref[...] = out


def _ffn(x, g, w1, w3, w2):
    t, d = x.shape
    f_dim = w1.shape[1]
    tm = FFN_ROW_TILE
    resident = pl.Buffered(1)
    est = 3 * d * f_dim * 2 + 4 * tm * d * 4 + 4 * tm * FFN_HIDDEN_CHUNK * 4
    return pl.pallas_call(
        _ffn_kernel,
        grid=(t // tm,),
        in_specs=[
            pl.BlockSpec((tm, d), lambda i: (i, 0)),
            pl.BlockSpec((1, d), lambda i: (0, 0)),
            pl.BlockSpec((d, f_dim), lambda i: (0, 0), pipeline_mode=resident),
            pl.BlockSpec((d, f_dim), lambda i: (0, 0), pipeline_mode=resident),
            pl.BlockSpec((f_dim, d), lambda i: (0, 0), pipeline_mode=resident),
        ],
        out_specs=pl.BlockSpec((tm, d), lambda i: (i, 0)),
        out_shape=jax.ShapeDtypeStruct((t, d), F32),
        compiler_params=pltpu.CompilerParams(
            dimension_semantics=("parallel",), vmem_limit_bytes=_vmem_limit(est)),
        name="ffn",
    )(x, g, w1, w3, w2)


def _lambda_init(layer):
    return 0.8 - 0.6 * math.exp(-0.3 * layer)


def kernel(x, mem, norm_mix, norm_mem, norm_ffn, w_in, w_gate, b_gate, da_q_norm, da_k_norm,
           da_lam_q1, da_lam_k1, da_lam_q2, da_lam_k2, da_out_norm, pool_w, pool_scale,
           mem_q_norm, mem_k_norm, w_mem_kv, w_br_da, w_br_pool, w_br_mem, w_out,
           ffn_w1, ffn_w3, ffn_w2, moe_router, moe_w1, moe_w3, moe_w2):
    b, s, d = x.shape
    depth = w_in.shape[0]
    xt = x.reshape(b * s, d)
    bf = lambda a: a.astype(BF16)
    row = lambda v: v.reshape(1, -1)
    for l in range(depth):
        lam0 = _lambda_init(l)
        heads, proj = _in_proj(xt, row(norm_mix[l]), w_in, l)
        mem_k, mem_v = _mem_kv(mem, row(norm_mem[l]), bf(w_mem_kv[l]), row(mem_k_norm[l]))
        lam_vecs = jnp.stack([da_lam_q1[l], da_lam_k1[l], da_lam_q2[l], da_lam_k2[l]])
        o_da = _diff_attn(heads.reshape(-1, b, s, LANES), lam_vecs, row(da_q_norm[l]), row(da_k_norm[l]),
                          da_out_norm[l].reshape(-1, 1), lam0)
        xt = _mix(xt, proj, o_da.reshape(b * s, -1), mem_k, mem_v, row(norm_mix[l]), bf(w_gate[l]),
                  row(b_gate[l]), bf(pool_w[l]), row(pool_scale[l]), row(mem_q_norm[l]),
                  bf(w_br_da[l]), bf(w_br_pool[l]), bf(w_br_mem[l]), bf(w_out[l]), s)
        j = l // 2
        if l % 2 == 0:
            xt = _ffn(xt, row(norm_ffn[l]), bf(ffn_w1[j]), bf(ffn_w3[j]), bf(ffn_w2[j]))
        else:
            xt = _moe(xt, row(norm_ffn[l]), moe_router[j], moe_w1[j], moe_w3[j], moe_w2[j])
    return xt.reshape(b, s, d)
```

```python
import functools
import math

import jax
import jax.numpy as jnp
import numpy as np
from jax import lax
from jax.experimental import pallas as pl
from jax.experimental.pallas import tpu as pltpu

F32 = jnp.float32
BF16 = jnp.bfloat16

D_MODEL = 1024
CHUNK = 64
RMS_EPS = 1e-6
DA_HEADS = 8
DA_QK_DIM = 64
DA_V_DIM = 128
POOL_WINDOWS = (2, 4, 8, 16)
POOL_GROUP_DIM = 128
POOL_WIDTH = 512
MEM_HEADS = 4
MEM_HEAD_DIM = 128
MEM_WIDTH = 512
N_EXPERTS = 8
LOG2E = 1.4426950408889634
MASK_VALUE = -1e30

V7X_VMEM_BYTES = 64 * 1024 * 1024
VMEM_LIMIT_CAP = 56 * 1024 * 1024
LANES = 128

ROW_TILE = 1024
IN_PROJ_ROW_TILE = 512
N_ATTN_SLABS = 3 * DA_HEADS
ATTN_TILE = 256
POOL_HALO = 16
FFN_ROW_TILE = 512
FFN_HIDDEN_CHUNK = 1536


def _vmem_limit(estimate_bytes):
    return int(min(VMEM_LIMIT_CAP, max(32 * 1024 * 1024, 2 * estimate_bytes)))


def _rms(x):
    return x * lax.rsqrt(jnp.mean(x * x, axis=-1, keepdims=True) + RMS_EPS)


def _dot(a, b):
    return jnp.dot(a, b, preferred_element_type=F32)


def _dot_nt(a, b):
    return lax.dot_general(a, b, (((1,), (1,)), ((), ())), preferred_element_type=F32)


def _in_proj_kernel(x_ref, g_ref, w_ref, heads_ref, rest_ref, wb_scr, *, col_tile):
    n = w_ref.shape[1]
    n_head_cols = heads_ref.shape[0] * LANES

    @pl.when(pl.program_id(0) == 0)
    def _():
        for c0 in range(0, n, col_tile):
            wb_scr[:, c0:c0 + col_tile] = w_ref[:, c0:c0 + col_tile].astype(BF16)

    hb = (_rms(x_ref[...]) * g_ref[...]).astype(BF16)
    for c0 in range(0, n, col_tile):
        res = _dot(hb, wb_scr[:, c0:c0 + col_tile]).astype(BF16)
        if c0 < n_head_cols:
            for j in range(col_tile // LANES):
                heads_ref[c0 // LANES + j] = res[:, j * LANES:(j + 1) * LANES]
        else:
            rest_ref[:, c0 - n_head_cols:c0 - n_head_cols + col_tile] = res


def _in_proj(x, g, w_stack, layer):
    t, d = x.shape
    n = w_stack.shape[2]
    tm = IN_PROJ_ROW_TILE
    est = 2 * (tm * d * 4 + tm * n * 2) + d * n * 6 + 2 * tm * 1024 * 4
    return pl.pallas_call(
        functools.partial(_in_proj_kernel, col_tile=1024),
        grid=(t // tm,),
        in_specs=[
            pl.BlockSpec((tm, d), lambda i: (i, 0)),
            pl.BlockSpec((1, d), lambda i: (0, 0)),
            pl.BlockSpec((None, d, n), lambda i: (layer, 0, 0), pipeline_mode=pl.Buffered(1)),
        ],
        out_specs=[
            pl.BlockSpec((N_ATTN_SLABS, tm, LANES), lambda i: (0, i, 0)),
            pl.BlockSpec((tm, n - N_ATTN_SLABS * LANES), lambda i: (i, 0)),
        ],
        out_shape=[
            jax.ShapeDtypeStruct((N_ATTN_SLABS, t, LANES), BF16),
            jax.ShapeDtypeStruct((t, n - N_ATTN_SLABS * LANES), BF16),
        ],
        scratch_shapes=[pltpu.VMEM((d, n), BF16)],
        compiler_params=pltpu.CompilerParams(
            dimension_semantics=("arbitrary",), vmem_limit_bytes=_vmem_limit(est)),
        name="in_proj",
    )(x, g, w_stack)


def _mem_kv_kernel(mem_ref, g_ref, w_ref, gk_ref, k_ref, v_ref):
    mh = (_rms(mem_ref[...]) * g_ref[...]).astype(BF16)
    kv = _dot(mh, w_ref[...])
    for hh in range(MEM_HEADS):
        c0 = hh * MEM_HEAD_DIM
        k_ref[:, c0:c0 + MEM_HEAD_DIM] = (_rms(kv[:, c0:c0 + MEM_HEAD_DIM]) * gk_ref[...]).astype(BF16)
    v_ref[...] = kv[:, MEM_WIDTH:].astype(BF16)


def _mem_kv(mem, g, w_bf16, gk):
    b, m, d = mem.shape
    out = jax.ShapeDtypeStruct((b, m, MEM_WIDTH), BF16)
    return pl.pallas_call(
        _mem_kv_kernel,
        grid=(b,),
        in_specs=[
            pl.BlockSpec((None, m, d), lambda i: (i, 0, 0)),
            pl.BlockSpec((1, d), lambda i: (0, 0)),
            pl.BlockSpec((d, 2 * MEM_WIDTH), lambda i: (0, 0)),
            pl.BlockSpec((1, MEM_HEAD_DIM), lambda i: (0, 0)),
        ],
        out_specs=[pl.BlockSpec((None, m, MEM_WIDTH), lambda i: (i, 0, 0))] * 2,
        out_shape=[out, out],
        compiler_params=pltpu.CompilerParams(dimension_semantics=("parallel",)),
        name="mem_kv",
    )(mem, g, w_bf16, gk)


def _diff_attn_kernel(lam_ref, pos_ref, q_ref, k_ref, v_ref, gq_ref, gk_ref, go_ref, o_ref,
                      qa_scr, ka_scr, vt_scr, s_scr, *, lam_init, seq):
    tq = ATTN_TILE
    head = pl.program_id(1)
    slope = jnp.exp2(-(jnp.zeros((1, LANES), F32) + (head + 1).astype(F32)))[:, :1] * LOG2E

    lam_v = lam_ref[...]
    lam = (jnp.exp(jnp.sum(lam_v[0:1] * lam_v[1:2], axis=-1, keepdims=True))
           - jnp.exp(jnp.sum(lam_v[2:3] * lam_v[3:4], axis=-1, keepdims=True)) + lam_init)

    r_i = lax.broadcasted_iota(jnp.int32, (LANES, LANES), 0) // DA_QK_DIM
    c_i = lax.broadcasted_iota(jnp.int32, (LANES, LANES), 1) // DA_QK_DIM
    group_mean = jnp.where(r_i == c_i, 1.0 / DA_QK_DIM, 0.0).astype(BF16)

    def qk_norm(x_bf16, gain):
        xf = x_bf16.astype(F32)
        ms = _dot((xf * xf).astype(BF16), group_mean)
        return xf * (lax.rsqrt(ms + RMS_EPS) * gain)

    lane = lax.broadcasted_iota(jnp.int32, (tq, LANES), 1)
    in_map = (lane < DA_QK_DIM, lane >= DA_QK_DIM)
    lane_row = lax.broadcasted_iota(jnp.int32, (1, LANES), 1)
    ones_at_pieces = jnp.where(lane_row < ALIBI_PIECES, 1.0, 0.0)
    pow2 = jnp.exp2(-(jnp.zeros((1, LANES), F32) + (head + 1).astype(F32)))
    q_gain = gq_ref[...] * (DA_QK_DIM ** -0.5 * LOG2E)
    n_blocks = seq // tq

    def prepare(blk):
        rows = slice(blk * tq, (blk + 1) * tq)
        qn = qk_norm(q_ref[rows, :], q_gain)
        for m in range(2):
            q_rows = slice((2 * blk + m) * tq, (2 * blk + m + 1) * tq)
            qa_scr[q_rows, :LANES] = jnp.where(in_map[m], qn, 0.0).astype(BF16)
            qa_scr[q_rows, LANES:] = jnp.broadcast_to(ones_at_pieces, (tq, LANES)).astype(BF16)
        ka_scr[rows, :LANES] = qk_norm(k_ref[rows, :], gk_ref[...]).astype(BF16)
        ka_scr[rows, LANES:] = (pos_ref[rows, :] * pow2).astype(BF16)
        vt_scr[0:DA_V_DIM, rows] = v_ref[rows, :].astype(F32).T.astype(BF16)
        vt_scr[DA_V_DIM:, rows] = jnp.ones((ATTN_ONES_ROWS, tq), BF16)

    krel = lax.broadcasted_iota(jnp.int32, (tq, 2 * tq), 0)
    qrel = lax.broadcasted_iota(jnp.int32, (tq, 2 * tq), 1) % tq
    allowed = (krel // CHUNK) <= (qrel // CHUNK)
    diag_corr = slope * (-2.0 * jnp.maximum(krel - qrel, 0).astype(F32))

    def col_max_of(s):
        mx = s[0:8, :]
        for r0 in range(8, s.shape[0], 8):
            mx = jnp.maximum(mx, s[r0:r0 + 8, :])
        return jnp.max(mx, axis=0, keepdims=True)

    def scores(i):
        q_i = qa_scr[2 * i * tq:2 * (i + 1) * tq, :]
        diag = slice(i * tq, (i + 1) * tq)
        s = jnp.where(allowed, _dot_nt(ka_scr[diag, :], q_i) + diag_corr, MASK_VALUE)
        s_scr[i % ATTN_SCORE_BUFFERS, diag, :] = s
        cmax = col_max_of(s)
        if i > 0:
            s = _dot_nt(ka_scr[0:i * tq, :], q_i)
            s_scr[i % ATTN_SCORE_BUFFERS, 0:i * tq, :] = s
            cmax = jnp.maximum(cmax, col_max_of(s))
        return cmax

    def values(i, col_max):
        n_keys = (i + 1) * tq
        p = jnp.exp2(s_scr[i % ATTN_SCORE_BUFFERS, 0:n_keys, :] - col_max).astype(BF16)
        acc = _dot(vt_scr[:, 0:n_keys], p)
        ratio = acc[:DA_V_DIM] / acc[DA_V_DIM:DA_V_DIM + 1]
        o = ratio[:, :tq] - lam * ratio[:, tq:]
        o = o * lax.rsqrt(jnp.mean(o * o, axis=0, keepdims=True) + RMS_EPS) * go_ref[...] * (1.0 - lam_init)
        o_ref[i * tq:(i + 1) * tq, :] = o.T.astype(BF16)

    ahead = ATTN_SCORE_BUFFERS - 1
    col_max = {}
    prepared = [0]

    def prepare_through(blk):
        while prepared[0] <= min(blk, n_blocks - 1):
            prepare(prepared[0])
            prepared[0] += 1

    prepare_through(ATTN_PREP_LEAD - 1)
    for blk in range(min(ahead, n_blocks)):
        prepare_through(blk + ATTN_PREP_LEAD)
        col_max[blk] = scores(blk)
    for i in range(n_blocks):
        prepare_through(i + ahead + ATTN_PREP_LEAD)
        if i + ahead < n_blocks:
            col_max[i + ahead] = scores(i + ahead)
        values(i, col_max.pop(i))


ATTN_ONES_ROWS = 16
ATTN_SCORE_BUFFERS = 3
ATTN_PREP_LEAD = 2
ALIBI_PIECES = 4


def _alibi_position_table(seq):
    rest = np.arange(seq, dtype=np.float64) * LOG2E
    table = np.zeros((seq, LANES), np.float32)
    for p in range(ALIBI_PIECES):
        piece = rest.astype(np.float32).astype(BF16).astype(np.float32)
        table[:, p] = piece
        rest = rest - piece.astype(np.float64)
    return jnp.asarray(table)


def _diff_attn(heads4, lam_vecs, gq, gk, go_col, lam_init):
    _, b, s, _ = heads4.shape
    hd = 2 * DA_QK_DIM
    kern = functools.partial(_diff_attn_kernel, lam_init=lam_init, seq=s)
    est = (2 * 4 * s * hd * 2 + 4 * s * hd * 2 + (DA_V_DIM + ATTN_ONES_ROWS) * s * 2 + 2 * ATTN_SCORE_BUFFERS * ATTN_TILE * s * 4
           + 8 * s * hd * 4)
    return pl.pallas_call(
        kern,
        grid=(b, DA_HEADS),
        in_specs=[
            pl.BlockSpec((4, DA_QK_DIM), lambda bi, h: (0, 0)),
            pl.BlockSpec((s, LANES), lambda bi, h: (0, 0)),
            pl.BlockSpec((None, None, s, hd), lambda bi, h: (h, bi, 0, 0)),
            pl.BlockSpec((None, None, s, hd), lambda bi, h: (DA_HEADS + h, bi, 0, 0)),
            pl.BlockSpec((None, None, s, DA_V_DIM), lambda bi, h: (2 * DA_HEADS + h, bi, 0, 0)),
            pl.BlockSpec((1, hd), lambda bi, h: (0, 0)),
            pl.BlockSpec((1, hd), lambda bi, h: (0, 0)),
            pl.BlockSpec((DA_V_DIM, 1), lambda bi, h: (0, 0)),
        ],
        out_specs=pl.BlockSpec((None, s, DA_V_DIM), lambda bi, h: (bi, 0, h)),
        out_shape=jax.ShapeDtypeStruct((b, s, DA_HEADS * DA_V_DIM), BF16),
        scratch_shapes=[
            pltpu.VMEM((2 * s, 2 * hd), BF16),
            pltpu.VMEM((s, 2 * hd), BF16),
            pltpu.VMEM((DA_V_DIM + ATTN_ONES_ROWS, s), BF16),
            pltpu.VMEM((ATTN_SCORE_BUFFERS, s, 2 * ATTN_TILE), F32),
        ],
        compiler_params=pltpu.CompilerParams(
            dimension_semantics=("parallel", "parallel"), vmem_limit_bytes=_vmem_limit(est)),
        name="diff_attn",
    )(lam_vecs, _alibi_position_table(s), heads4, heads4, heads4, gq, gk, go_col)


def _mix_kernel(x_ref, pool_ref, halo_ref, mq_ref, oda_ref, mk_ref, mv_ref,
                gmix_ref, wg_ref, bg_ref, wpool_ref, pscale_ref, gmq_ref,
                wda_ref, wbp_ref, wbm_ref, wout_ref, o_ref, *, tiles_per_seq):
    tm = x_ref.shape[0]
    d = D_MODEL
    x = x_ref[...]
    hb = (_rms(x) * gmix_ref[...]).astype(BF16)

    tile_in_seq = pl.program_id(0) % tiles_per_seq
    p_cur = pool_ref[...].astype(F32)
    halo = jnp.where(tile_in_seq == 0, 0.0, halo_ref[...].astype(F32))
    ext = jnp.concatenate([halo, p_cur], axis=0)
    t_pos = tile_in_seq * tm + lax.broadcasted_iota(jnp.int32, (tm, 1), 0)
    pool_parts = []
    for g, w in enumerate(POOL_WINDOWS):
        c0 = g * POOL_GROUP_DIM
        s = ext[:, c0:c0 + POOL_GROUP_DIM]
        sh = 1
        while sh < w:
            s = s + pltpu.roll(s, sh, axis=0)
            sh *= 2
        cnt = jnp.minimum(t_pos + 1, w).astype(F32)
        dlt = (s[POOL_HALO:] / cnt - p_cur[:, c0:c0 + POOL_GROUP_DIM]).astype(BF16)
        pool_parts.append(_dot(dlt, wpool_ref[g]))
    o_pool = (jnp.concatenate(pool_parts, axis=1) * pscale_ref[...]).astype(BF16)

    mq = mq_ref[...].astype(F32)
    mem_parts = []
    for hh in range(MEM_HEADS):
        c0 = hh * MEM_HEAD_DIM
        qh = (_rms(mq[:, c0:c0 + MEM_HEAD_DIM]) * gmq_ref[...] * (MEM_HEAD_DIM ** -0.5)).astype(BF16)
        s = _dot_nt(qh, mk_ref[:, c0:c0 + MEM_HEAD_DIM])
        p = jnp.exp(s - jnp.max(s, axis=-1, keepdims=True))
        l = jnp.sum(p, axis=-1, keepdims=True)
        mem_parts.append(_dot(p.astype(BF16), mv_ref[:, c0:c0 + MEM_HEAD_DIM]) / l)
    o_mem = jnp.concatenate(mem_parts, axis=1).astype(BF16)

    def gate(k):
        return jax.nn.sigmoid(_dot(hb, wg_ref[:, k * d:(k + 1) * d]) + bg_ref[:, k * d:(k + 1) * d])

    merged = gate(0) * _dot(oda_ref[...], wda_ref[...])
    merged = merged + gate(1) * _dot(o_pool, wbp_ref[...])
    merged = merged + gate(2) * _dot(o_mem, wbm_ref[...])
    o_ref[...] = x + _dot(merged.astype(BF16), wout_ref[...])


def _mix(x, proj, o_da, mem_k, mem_v, gmix, wg, bg, wpool, pscale, gmq, wda, wbp, wbm, wout, seq):
    t, d = x.shape
    tm = ROW_TILE
    tiles_per_seq = seq // tm
    halo_blocks_per_tile = tm // POOL_HALO
    pool_col = (proj.shape[1] - POOL_WIDTH - MEM_WIDTH) // POOL_WIDTH
    mem_len = mem_k.shape[1]
    const = lambda i: (0, 0)
    weights_bytes = 2 * (wg.size + wda.size + wbp.size + wbm.size + wout.size + wpool.size)
    est = 2 * weights_bytes + 2 * (2 * tm * d * 4 + tm * (d + 2 * POOL_WIDTH) * 2) + 8 * tm * d * 4
    return pl.pallas_call(
        functools.partial(_mix_kernel, tiles_per_seq=tiles_per_seq),
        grid=(t // tm,),
        in_specs=[
            pl.BlockSpec((tm, d), lambda i: (i, 0)),
            pl.BlockSpec((tm, POOL_WIDTH), lambda i: (i, pool_col)),
            pl.BlockSpec((POOL_HALO, POOL_WIDTH),
                         lambda i: (jnp.maximum(i * halo_blocks_per_tile - 1, 0), pool_col)),
            pl.BlockSpec((tm, MEM_WIDTH), lambda i: (i, pool_col + 1)),
            pl.BlockSpec((tm, d), lambda i: (i, 0)),
            pl.BlockSpec((None, mem_len, MEM_WIDTH), lambda i: (i // tiles_per_seq, 0, 0)),
            pl.BlockSpec((None, mem_len, MEM_WIDTH), lambda i: (i // tiles_per_seq, 0, 0)),
            pl.BlockSpec((1, d), const),
            pl.BlockSpec(wg.shape, const, pipeline_mode=pl.Buffered(1)),
            pl.BlockSpec((1, bg.shape[1]), const),
            pl.BlockSpec(wpool.shape, lambda i: (0, 0, 0)),
            pl.BlockSpec((1, POOL_WIDTH), const),
            pl.BlockSpec((1, MEM_HEAD_DIM), const),
            pl.BlockSpec(wda.shape, const, pipeline_mode=pl.Buffered(1)),
            pl.BlockSpec(wbp.shape, const, pipeline_mode=pl.Buffered(1)),
            pl.BlockSpec(wbm.shape, const, pipeline_mode=pl.Buffered(1)),
            pl.BlockSpec(wout.shape, const, pipeline_mode=pl.Buffered(1)),
        ],
        out_specs=pl.BlockSpec((tm, d), lambda i: (i, 0)),
        out_shape=jax.ShapeDtypeStruct((t, d), F32),
        compiler_params=pltpu.CompilerParams(
            dimension_semantics=("parallel",), vmem_limit_bytes=_vmem_limit(est)),
        name="mix",
    )(x, proj, proj, proj, o_da, mem_k, mem_v, gmix, wg, bg, wpool, pscale, gmq, wda, wbp, wbm, wout)


def _split3(x):
    hi = x.astype(BF16)
    r = x - hi.astype(F32)
    mid = r.astype(BF16)
    lo = (r - mid.astype(F32)).astype(BF16)
    return hi, mid, lo


MOE_TILE = 512
SEG_MAIN = 192
SEG_OVER = MOE_TILE - SEG_MAIN
SEG_ALIGN = 16
NOT_ROUTED = -1e6
UP_COL_TILE = 1792
DOWN_COL_TILE = 1024
COPY_SIZES = (512, 256, 128, 64, 32, 16)


def _max_expert_tiles(t):
    padded_rows = 2 * t + (t // MOE_TILE) * N_EXPERTS * (SEG_ALIGN - 1)
    return padded_rows // MOE_TILE + N_EXPERTS


def _row_copies(src, dst, src_row0, dst_row0, n_rows, max_rows, sem):
    pairs = []
    pos = 0
    for size in COPY_SIZES:
        if size > max_rows:
            continue
        cond = jnp.bitwise_and(n_rows, size) != 0
        cp = pltpu.make_async_copy(
            src.at[pl.ds(pl.multiple_of(src_row0 + pos, SEG_ALIGN), size), :],
            dst.at[pl.ds(pl.multiple_of(dst_row0 + pos, SEG_ALIGN), size), :], sem)
        pairs.append((cond, cp))
        pos = pos + jnp.where(cond, size, 0)
    return pairs


def _start_all(pairs):
    for cond, cp in pairs:
        pl.when(cond)(cp.start)


def _wait_all(pairs):
    for cond, cp in pairs:
        pl.when(cond)(cp.wait)


def _onehot_rows(rank_row, n_rows, row0):
    r = lax.broadcasted_iota(jnp.int32, (n_rows, rank_row.shape[1]), 0).astype(F32) + float(row0)
    return jnp.where(r == rank_row, 1.0, 0.0).astype(BF16)


def _route_kernel(x_ref, g_ref, wr_ref, rank_ref, meta_ref, cnts_ref):
    i = pl.program_id(0)
    tm = x_ref.shape[0]
    h2 = _rms(x_ref[...]) * g_ref[...]
    h_hi, h_mid, _ = _split3(h2)
    w_hi, w_mid, _ = _split3(wr_ref[...])
    logits = _dot_nt(w_hi, h_hi) + (_dot_nt(w_hi, h_mid) + _dot_nt(w_mid, h_hi))
    eidx = lax.broadcasted_iota(jnp.int32, logits.shape, 0).astype(F32)
    m1 = jnp.max(logits, axis=0, keepdims=True)
    i1 = jnp.min(jnp.where(logits == m1, eidx, float(N_EXPERTS)), axis=0, keepdims=True)
    sel1 = eidx == i1
    rest = jnp.where(sel1, -jnp.inf, logits)
    m2 = jnp.max(rest, axis=0, keepdims=True)
    i2 = jnp.min(jnp.where(rest == m2, eidx, float(N_EXPERTS)), axis=0, keepdims=True)
    sel2 = eidx == i2
    e2 = jnp.exp(m2 - m1)
    g1 = 1.0 / (1.0 + e2)
    g2 = e2 / (1.0 + e2)

    mask = jnp.where(sel1 | sel2, 1.0, 0.0)
    before = (lax.broadcasted_iota(jnp.int32, (tm, tm), 0)
              < lax.broadcasted_iota(jnp.int32, (tm, tm), 1)).astype(BF16)
    rank = _dot(mask.astype(BF16), before)
    counts = jnp.sum(mask, axis=1, keepdims=True)
    rank_m = jnp.where(mask > 0.5, rank, NOT_ROUTED)

    rk1 = jnp.where(sel1, rank, NOT_ROUTED)
    rk2 = jnp.where(sel2, rank, NOT_ROUTED)
    gates = jnp.where(eidx == 0.0, g1, jnp.where(eidx == 1.0, g2, 0.0))
    packed = jnp.concatenate([rk1, rk2, gates], axis=0)
    ident = (lax.broadcasted_iota(jnp.int32, (tm, tm), 0)
             == lax.broadcasted_iota(jnp.int32, (tm, tm), 1)).astype(BF16)
    p_hi, p_mid, p_lo = _split3(packed)
    meta_ref[...] = _dot_nt(ident, p_hi) + (_dot_nt(ident, p_mid) + _dot_nt(ident, p_lo))

    rank_ref[...] = rank_m
    for e in range(N_EXPERTS):
        cnt = counts[e, 0].astype(jnp.int32)
        cnts_ref[i * N_EXPERTS + e] = jnp.bitwise_and(cnt + (SEG_ALIGN - 1), -SEG_ALIGN)


def _route(x, g, w_router_t):
    t, d = x.shape
    tm = MOE_TILE
    nt = t // tm
    est = 2 * tm * d * 4 + 6 * tm * tm * 4 + 4 * tm * d * 4
    return pl.pallas_call(
        _route_kernel,
        grid=(nt,),
        in_specs=[
            pl.BlockSpec((tm, d), lambda i: (i, 0)),
            pl.BlockSpec((1, d), lambda i: (0, 0)),
            pl.BlockSpec((N_EXPERTS, d), lambda i: (0, 0)),
        ],
        out_specs=[
            pl.BlockSpec((N_EXPERTS, tm), lambda i: (0, i)),
            pl.BlockSpec((tm, 3 * N_EXPERTS), lambda i: (i, 0)),
            pl.BlockSpec(memory_space=pltpu.SMEM),
        ],
        out_shape=[
            jax.ShapeDtypeStruct((N_EXPERTS, t), F32),
            jax.ShapeDtypeStruct((t, 3 * N_EXPERTS), F32),
            jax.ShapeDtypeStruct((nt * N_EXPERTS,), jnp.int32),
        ],
        compiler_params=pltpu.CompilerParams(
            dimension_semantics=("arbitrary",), vmem_limit_bytes=_vmem_limit(est)),
        name="route",
    )(x, g, w_router_t)


def _scatter_kernel(offs_ref, cnts_ref, fill_ref, nv_ref, x_ref, g_ref, rank_ref, xs_hbm,
                    stage, over_stage, zero_buf, sems, over_sem, *, n_tiles):
    i = pl.program_id(0)
    hb = (_rms(x_ref[...]) * g_ref[...]).astype(BF16)
    rank_m = rank_ref[...]

    def main_copies(step, e):
        n_main = jnp.minimum(cnts_ref[step * N_EXPERTS + e], SEG_MAIN)
        return _row_copies(stage.at[step % 2, e], xs_hbm, 0, offs_ref[step * N_EXPERTS + e], n_main,
                           SEG_MAIN, sems.at[step % 2, e])

    @pl.when(i > 1)
    def _():
        for e in range(N_EXPERTS):
            _wait_all(main_copies(i - 2, e))

    onehot_all = jnp.concatenate(
        [_onehot_rows(rank_m[e:e + 1, :], SEG_MAIN, 0) for e in range(N_EXPERTS)], axis=0)
    compact = _dot(onehot_all, hb).astype(BF16)
    for e in range(N_EXPERTS):
        rank_e = rank_m[e:e + 1, :]
        stage[i % 2, e] = compact[e * SEG_MAIN:(e + 1) * SEG_MAIN, :]
        _start_all(main_copies(i, e))
        n_over = cnts_ref[i * N_EXPERTS + e] - SEG_MAIN

        @pl.when(n_over > 0)
        def _():
            over_stage[...] = _dot(_onehot_rows(rank_e, SEG_OVER, SEG_MAIN), hb).astype(BF16)
            pairs = _row_copies(over_stage, xs_hbm, 0, offs_ref[i * N_EXPERTS + e] + SEG_MAIN, n_over,
                                SEG_OVER, over_sem)
            _start_all(pairs)
            _wait_all(pairs)

    @pl.when(i == pl.num_programs(0) - 1)
    def _():
        for e in range(N_EXPERTS):
            _wait_all(main_copies(i - 1, e))
            _wait_all(main_copies(i, e))
        zero_buf[...] = jnp.zeros_like(zero_buf)
        for e in range(N_EXPERTS):
            pairs = _row_copies(zero_buf, xs_hbm, 0, fill_ref[e], fill_ref[N_EXPERTS + e], MOE_TILE,
                                sems.at[0, e])
            _start_all(pairs)
            _wait_all(pairs)

        def zero_tile(tile, carry):
            dst = xs_hbm.at[pl.ds(pl.multiple_of(tile * MOE_TILE, MOE_TILE), MOE_TILE), :]
            cp = pltpu.make_async_copy(zero_buf, dst, over_sem)
            cp.start()
            cp.wait()
            return carry

        lax.fori_loop(nv_ref[0], n_tiles, zero_tile, 0)


def _scatter(x, g, rank, offs, cnts, fill, n_valid, n_tiles):
    t, d = x.shape
    tm = MOE_TILE
    assert t // tm >= 2, "the segment copies are waited two grid steps after they start"
    est = 2 * tm * d * 4 + (2 * N_EXPERTS * SEG_MAIN + SEG_OVER + MOE_TILE) * d * 2 + 6 * tm * d * 4
    grid_spec = pltpu.PrefetchScalarGridSpec(
        num_scalar_prefetch=4,
        grid=(t // tm,),
        in_specs=[
            pl.BlockSpec((tm, d), lambda i, *_: (i, 0)),
            pl.BlockSpec((1, d), lambda i, *_: (0, 0)),
            pl.BlockSpec((N_EXPERTS, tm), lambda i, *_: (0, i)),
        ],
        out_specs=pl.BlockSpec(memory_space=pl.ANY),
        scratch_shapes=[
            pltpu.VMEM((2, N_EXPERTS, SEG_MAIN, d), BF16),
            pltpu.VMEM((SEG_OVER, d), BF16),
            pltpu.VMEM((MOE_TILE, d), BF16),
            pltpu.SemaphoreType.DMA((2, N_EXPERTS)),
            pltpu.SemaphoreType.DMA(()),
        ],
    )
    return pl.pallas_call(
        functools.partial(_scatter_kernel, n_tiles=n_tiles),
        grid_spec=grid_spec,
        out_shape=jax.ShapeDtypeStruct((n_tiles * MOE_TILE, d), BF16),
        compiler_params=pltpu.CompilerParams(
            dimension_semantics=("arbitrary",), vmem_limit_bytes=_vmem_limit(est)),
        name="scatter",
    )(offs, cnts, fill, n_valid, x, g, rank)


def _expert_changed(te_ref, t):
    return (t == 0) | (te_ref[t] != te_ref[jnp.maximum(t - 1, 0)])


def _expert_up_kernel(te_ref, nv_ref, x_ref, w1_ref, w3_ref, h_ref, w1b, w3b):
    t = pl.program_id(1)

    @pl.when(_expert_changed(te_ref, t))
    def _():
        w1b[...] = w1_ref[...].astype(BF16)
        w3b[...] = w3_ref[...].astype(BF16)

    @pl.when(t < nv_ref[0])
    def _():
        x = x_ref[...]
        a = _dot(x, w1b[...])
        b = _dot(x, w3b[...])
        h_ref[...] = (a * jax.nn.sigmoid(a) * b).astype(BF16)

    @pl.when(t >= nv_ref[0])
    def _():
        h_ref[...] = jnp.zeros_like(h_ref)


def _expert_up(xs, w1, w3, tile_expert, n_valid):
    rows, d = xs.shape
    _, _, f_dim = w1.shape
    tf = UP_COL_TILE
    est = 2 * (MOE_TILE * d * 2 + 2 * d * tf * 4 + MOE_TILE * tf * 2) + 2 * d * tf * 2 + 3 * MOE_TILE * tf * 4
    grid_spec = pltpu.PrefetchScalarGridSpec(
        num_scalar_prefetch=2,
        grid=(f_dim // tf, rows // MOE_TILE),
        in_specs=[
            pl.BlockSpec((MOE_TILE, d), lambda f, t, te, nv: (t, 0)),
            pl.BlockSpec((None, d, tf), lambda f, t, te, nv: (te[t], 0, f)),
            pl.BlockSpec((None, d, tf), lambda f, t, te, nv: (te[t], 0, f)),
        ],
        out_specs=pl.BlockSpec((MOE_TILE, tf), lambda f, t, te, nv: (t, f)),
        scratch_shapes=[pltpu.VMEM((d, tf), BF16), pltpu.VMEM((d, tf), BF16)],
    )
    return pl.pallas_call(
        _expert_up_kernel,
        grid_spec=grid_spec,
        out_shape=jax.ShapeDtypeStruct((rows, f_dim), BF16),
        compiler_params=pltpu.CompilerParams(
            dimension_semantics=("arbitrary", "arbitrary"), vmem_limit_bytes=_vmem_limit(est)),
        name="expert_up",
    )(tile_expert, n_valid, xs, w1, w3)


def _expert_down_kernel(te_ref, nv_ref, h_ref, w2_ref, y_ref, w2b):
    t = pl.program_id(1)

    @pl.when(_expert_changed(te_ref, t))
    def _():
        w2b[...] = w2_ref[...].astype(BF16)

    @pl.when(t < nv_ref[0])
    def _():
        y_ref[...] = _dot(h_ref[...], w2b[...]).astype(BF16)

    @pl.when(t >= nv_ref[0])
    def _():
        y_ref[...] = jnp.zeros_like(y_ref)


def _expert_down(hs, w2, tile_expert, n_valid):
    rows, f_dim = hs.shape
    d = w2.shape[2]
    tn = DOWN_COL_TILE
    est = 2 * (MOE_TILE * f_dim * 2 + f_dim * tn * 4 + MOE_TILE * tn * 2) + f_dim * tn * 2 + MOE_TILE * tn * 4
    grid_spec = pltpu.PrefetchScalarGridSpec(
        num_scalar_prefetch=2,
        grid=(d // tn, rows // MOE_TILE),
        in_specs=[
            pl.BlockSpec((MOE_TILE, f_dim), lambda n, t, te, nv: (t, 0)),
            pl.BlockSpec((None, f_dim, tn), lambda n, t, te, nv: (te[t], 0, n)),
        ],
        out_specs=pl.BlockSpec((MOE_TILE, tn), lambda n, t, te, nv: (t, n)),
        scratch_shapes=[pltpu.VMEM((f_dim, tn), BF16)],
    )
    return pl.pallas_call(
        _expert_down_kernel,
        grid_spec=grid_spec,
        out_shape=jax.ShapeDtypeStruct((rows, d), BF16),
        compiler_params=pltpu.CompilerParams(
            dimension_semantics=("arbitrary", "arbitrary"), vmem_limit_bytes=_vmem_limit(est)),
        name="expert_down",
    )(tile_expert, n_valid, hs, w2)


def _combine_kernel(offs_ref, cnts_ref, x_ref, meta_ref, y_hbm, o_ref,
                    ybuf, over_buf, over_acc, sems, over_sem):
    i = pl.program_id(0)
    tm = x_ref.shape[0]
    total_rows = y_hbm.shape[0]
    meta = meta_ref[...]
    g1 = meta[:, 2 * N_EXPERTS:2 * N_EXPERTS + 1]
    g2 = meta[:, 2 * N_EXPERTS + 1:2 * N_EXPERTS + 2]

    def window(step, e, seg_row0, n_rows):
        off = offs_ref[step * N_EXPERTS + e] + seg_row0
        start = jnp.minimum(off, total_rows - n_rows)
        return start, off - start

    def main_window_copy(step, e):
        start, _ = window(step, e, 0, SEG_MAIN)
        src = y_hbm.at[pl.ds(pl.multiple_of(start, SEG_ALIGN), SEG_MAIN), :]
        return pltpu.make_async_copy(src, ybuf.at[step % 2, e], sems.at[step % 2, e])

    @pl.when(i == 0)
    def _():
        for e in range(N_EXPERTS):
            main_window_copy(i, e).start()

    @pl.when(i + 1 < pl.num_programs(0))
    def _():
        for e in range(N_EXPERTS):
            main_window_copy(i + 1, e).start()

    def weights(e, shift, n_cols):
        c = lax.broadcasted_iota(jnp.int32, (tm, n_cols), 1).astype(F32)
        first = meta[:, e:e + 1] + shift
        second = meta[:, N_EXPERTS + e:N_EXPERTS + e + 1] + shift
        return jnp.where(c == first, g1, jnp.where(c == second, g2, 0.0)).astype(BF16)

    out = x_ref[...]
    any_over = cnts_ref[i * N_EXPERTS] > SEG_MAIN
    for e in range(N_EXPERTS):
        _, shift = window(i, e, 0, SEG_MAIN)
        w_e = weights(e, shift.astype(F32), SEG_MAIN)
        main_window_copy(i, e).wait()
        out = out + _dot(w_e, ybuf[i % 2, e])
        if e > 0:
            any_over = any_over | (cnts_ref[i * N_EXPERTS + e] > SEG_MAIN)
    o_ref[...] = out

    @pl.when(any_over)
    def _():
        over_acc[...] = jnp.zeros_like(over_acc)
        for e in range(N_EXPERTS):
            @pl.when(cnts_ref[i * N_EXPERTS + e] > SEG_MAIN)
            def _():
                start, shift = window(i, e, SEG_MAIN, SEG_OVER)
                src = y_hbm.at[pl.ds(pl.multiple_of(start, SEG_ALIGN), SEG_OVER), :]
                cp = pltpu.make_async_copy(src, over_buf, over_sem)
                cp.start()
                cp.wait()
                over_acc[...] += _dot(weights(e, shift.astype(F32) - float(SEG_MAIN), SEG_OVER), over_buf[...])
        o_ref[...] += over_acc[...]


def _combine(x, meta, ys, offs, cnts):
    t, d = x.shape
    tm = MOE_TILE
    est = (4 * tm * d * 4 + (2 * N_EXPERTS * SEG_MAIN + SEG_OVER) * d * 2 + 2 * tm * d * 4
           + 2 * tm * LANES * 4 + 6 * tm * d * 4)
    grid_spec = pltpu.PrefetchScalarGridSpec(
        num_scalar_prefetch=2,
        grid=(t // tm,),
        in_specs=[
            pl.BlockSpec((tm, d), lambda i, *_: (i, 0)),
            pl.BlockSpec((tm, 3 * N_EXPERTS), lambda i, *_: (i, 0)),
            pl.BlockSpec(memory_space=pl.ANY),
        ],
        out_specs=pl.BlockSpec((tm, d), lambda i, *_: (i, 0)),
        scratch_shapes=[
            pltpu.VMEM((2, N_EXPERTS, SEG_MAIN, d), BF16),
            pltpu.VMEM((SEG_OVER, d), BF16),
            pltpu.VMEM((tm, d), F32),
            pltpu.SemaphoreType.DMA((2, N_EXPERTS)),
            pltpu.SemaphoreType.DMA(()),
        ],
    )
    return pl.pallas_call(
        _combine_kernel,
        grid_spec=grid_spec,
        out_shape=jax.ShapeDtypeStruct((t, d), F32),
        compiler_params=pltpu.CompilerParams(
            dimension_semantics=("arbitrary",), vmem_limit_bytes=_vmem_limit(est)),
        name="combine",
    )(offs, cnts, x, meta, ys)


def _moe(x, g, w_router, w1, w3, w2):
    t, _ = x.shape
    nt = t // MOE_TILE
    n_tiles = _max_expert_tiles(t)
    rank, meta, cnts = _route(x, g, w_router.T)
    cnt2 = cnts.reshape(nt, N_EXPERTS)
    rows_e = jnp.sum(cnt2, axis=0)
    tiles_e = jnp.maximum((rows_e + MOE_TILE - 1) // MOE_TILE, 1)
    ends = jnp.cumsum(tiles_e)
    base_e = (ends - tiles_e) * MOE_TILE
    offs = (base_e[None, :] + jnp.cumsum(cnt2, axis=0) - cnt2).reshape(-1).astype(jnp.int32)
    fill = jnp.concatenate([base_e + rows_e, tiles_e * MOE_TILE - rows_e]).astype(jnp.int32)
    n_valid = ends[-1].reshape(1).astype(jnp.int32)
    tid = jnp.arange(n_tiles, dtype=jnp.int32)
    tile_expert = jnp.minimum(jnp.sum(tid[:, None] >= ends[None, :], axis=1), N_EXPERTS - 1).astype(jnp.int32)
    xs = _scatter(x, g, rank, offs, cnts, fill, n_valid, n_tiles)
    hs = _expert_up(xs, w1, w3, tile_expert, n_valid)
    ys = _expert_down(hs, w2, tile_expert, n_valid)
    return _combine(x, meta, ys, offs, cnts)


def _ffn_kernel(x_ref, g_ref, w1_ref, w3_ref, w2_ref, o_ref):
    x = x_ref[...]
    hb = (_rms(x) * g_ref[...]).astype(BF16)
    f_dim = w1_ref.shape[1]
    out = x
    for c0 in range(0, f_dim, FFN_HIDDEN_CHUNK):
        c1 = min(c0 + FFN_HIDDEN_CHUNK, f_dim)
        a = _dot(hb, w1_ref[:, c0:c1])
        b = _dot(hb, w3_ref[:, c0:c1])
        out = out + _dot((a * jax.nn.sigmoid(a) * b).astype(BF16), w2_ref[c0:c1, :])
    o_ref[...] = out


def _ffn(x, g, w1, w3, w2):
    t, d = x.shape
    f_dim = w1.shape[1]
    tm = FFN_ROW_TILE
    resident = pl.Buffered(1)
    est = 3 * d * f_dim * 2 + 4 * tm * d * 4 + 4 * tm * FFN_HIDDEN_CHUNK * 4
    return pl.pallas_call(
        _ffn_kernel,
        grid=(t // tm,),
        in_specs=[
            pl.BlockSpec((tm, d), lambda i: (i, 0)),
            pl.BlockSpec((1, d), lambda i: (0, 0)),
            pl.BlockSpec((d, f_dim), lambda i: (0, 0), pipeline_mode=resident),
            pl.BlockSpec((d, f_dim), lambda i: (0, 0), pipeline_mode=resident),
            pl.BlockSpec((f_dim, d), lambda i: (0, 0), pipeline_mode=resident),
        ],
        out_specs=pl.BlockSpec((tm, d), lambda i: (i, 0)),
        out_shape=jax.ShapeDtypeStruct((t, d), F32),
        compiler_params=pltpu.CompilerParams(
            dimension_semantics=("parallel",), vmem_limit_bytes=_vmem_limit(est)),
        name="ffn",
    )(x, g, w1, w3, w2)


def _lambda_init(layer):
    return 0.8 - 0.6 * math.exp(-0.3 * layer)


def kernel(x, mem, norm_mix, norm_mem, norm_ffn, w_in, w_gate, b_gate, da_q_norm, da_k_norm,
           da_lam_q1, da_lam_k1, da_lam_q2, da_lam_k2, da_out_norm, pool_w, pool_scale,
           mem_q_norm, mem_k_norm, w_mem_kv, w_br_da, w_br_pool, w_br_mem, w_out,
           ffn_w1, ffn_w3, ffn_w2, moe_router, moe_w1, moe_w3, moe_w2):
    b, s, d = x.shape
    depth = w_in.shape[0]
    xt = x.reshape(b * s, d)
    bf = lambda a: a.astype(BF16)
    row = lambda v: v.reshape(1, -1)
    for l in range(depth):
        lam0 = _lambda_init(l)
        heads, proj = _in_proj(xt, row(norm_mix[l]), w_in, l)
        mem_k, mem_v = _mem_kv(mem, row(norm_mem[l]), bf(w_mem_kv[l]), row(mem_k_norm[l]))
        lam_vecs = jnp.stack([da_lam_q1[l], da_lam_k1[l], da_lam_q2[l], da_lam_k2[l]])
        o_da = _diff_attn(heads.reshape(-1, b, s, LANES), lam_vecs, row(da_q_norm[l]), row(da_k_norm[l]),
                          da_out_norm[l].reshape(-1, 1), lam0)
        xt = _mix(xt, proj, o_da.reshape(b * s, -1), mem_k, mem_v, row(norm_mix[l]), bf(w_gate[l]),
                  row(b_gate[l]), bf(pool_w[l]), row(pool_scale[l]), row(mem_q_norm[l]),
                  bf(w_br_da[l]), bf(w_br_pool[l]), bf(w_br_mem[l]), bf(w_out[l]), s)
        j = l // 2
        if l % 2 == 0:
            xt = _ffn(xt, row(norm_ffn[l]), bf(ffn_w1[j]), bf(ffn_w3[j]), bf(ffn_w2[j]))
        else:
            xt = _moe(xt, row(norm_ffn[l]), moe_router[j], moe_w1[j], moe_w3[j], moe_w2[j])
    return xt.reshape(b, s, d)
```

```python
import functools
import math

import jax
import jax.numpy as jnp
import numpy as np
from jax import lax
from jax.experimental import pallas as pl
from jax.experimental.pallas import tpu as pltpu

F32 = jnp.float32
BF16 = jnp.bfloat16

D_MODEL = 1024
CHUNK = 64
RMS_EPS = 1e-6
DA_HEADS = 8
DA_QK_DIM = 64
DA_V_DIM = 128
POOL_WINDOWS = (2, 4, 8, 16)
POOL_GROUP_DIM = 128
POOL_WIDTH = 512
MEM_HEADS = 4
MEM_HEAD_DIM = 128
MEM_WIDTH = 512
N_EXPERTS = 8
LOG2E = 1.4426950408889634
MASK_VALUE = -1e30

V7X_VMEM_BYTES = 64 * 1024 * 1024
VMEM_LIMIT_CAP = 56 * 1024 * 1024
LANES = 128

ROW_TILE = 1024
IN_PROJ_ROW_TILE = 512
N_ATTN_SLABS = 3 * DA_HEADS
ATTN_TILE = 256
POOL_HALO = 16
FFN_ROW_TILE = 512
FFN_HIDDEN_CHUNK = 1536


def _vmem_limit(estimate_bytes):
    return int(min(VMEM_LIMIT_CAP, max(32 * 1024 * 1024, 2 * estimate_bytes)))


def _rms(x):
    return x * lax.rsqrt(jnp.mean(x * x, axis=-1, keepdims=True) + RMS_EPS)


def _dot(a, b):
    return jnp.dot(a, b, preferred_element_type=F32)


def _dot_nt(a, b):
    return lax.dot_general(a, b, (((1,), (1,)), ((), ())), preferred_element_type=F32)


def _in_proj_kernel(x_ref, g_ref, w_ref, heads_ref, rest_ref, wb_scr, *, col_tile):
    n = w_ref.shape[1]
    n_head_cols = heads_ref.shape[0] * LANES

    @pl.when(pl.program_id(0) == 0)
    def _():
        for c0 in range(0, n, col_tile):
            wb_scr[:, c0:c0 + col_tile] = w_ref[:, c0:c0 + col_tile].astype(BF16)

    hb = (_rms(x_ref[...]) * g_ref[...]).astype(BF16)
    for c0 in range(0, n, col_tile):
        res = _dot(hb, wb_scr[:, c0:c0 + col_tile]).astype(BF16)
        if c0 < n_head_cols:
            for j in range(col_tile // LANES):
                heads_ref[c0 // LANES + j] = res[:, j * LANES:(j + 1) * LANES]
        else:
            rest_ref[:, c0 - n_head_cols:c0 - n_head_cols + col_tile] = res


def _in_proj(x, g, w_stack, layer):
    t, d = x.shape
    n = w_stack.shape[2]
    tm = IN_PROJ_ROW_TILE
    est = 2 * (tm * d * 4 + tm * n * 2) + d * n * 6 + 2 * tm * 1024 * 4
    return pl.pallas_call(
        functools.partial(_in_proj_kernel, col_tile=1024),
        grid=(t // tm,),
        in_specs=[
            pl.BlockSpec((tm, d), lambda i: (i, 0)),
            pl.BlockSpec((1, d), lambda i: (0, 0)),
            pl.BlockSpec((None, d, n), lambda i: (layer, 0, 0), pipeline_mode=pl.Buffered(1)),
        ],
        out_specs=[
            pl.BlockSpec((N_ATTN_SLABS, tm, LANES), lambda i: (0, i, 0)),
            pl.BlockSpec((tm, n - N_ATTN_SLABS * LANES), lambda i: (i, 0)),
        ],
        out_shape=[
            jax.ShapeDtypeStruct((N_ATTN_SLABS, t, LANES), BF16),
            jax.ShapeDtypeStruct((t, n - N_ATTN_SLABS * LANES), BF16),
        ],
        scratch_shapes=[pltpu.VMEM((d, n), BF16)],
        compiler_params=pltpu.CompilerParams(
            dimension_semantics=("arbitrary",), vmem_limit_bytes=_vmem_limit(est)),
        name="in_proj",
    )(x, g, w_stack)


def _mem_kv_kernel(mem_ref, g_ref, w_ref, gk_ref, k_ref, v_ref):
    mh = (_rms(mem_ref[...]) * g_ref[...]).astype(BF16)
    kv = _dot(mh, w_ref[...])
    for hh in range(MEM_HEADS):
        c0 = hh * MEM_HEAD_DIM
        k_ref[:, c0:c0 + MEM_HEAD_DIM] = (_rms(kv[:, c0:c0 + MEM_HEAD_DIM]) * gk_ref[...]).astype(BF16)
    v_ref[...] = kv[:, MEM_WIDTH:].astype(BF16)


def _mem_kv(mem, g, w_bf16, gk):
    b, m, d = mem.shape
    out = jax.ShapeDtypeStruct((b, m, MEM_WIDTH), BF16)
    return pl.pallas_call(
        _mem_kv_kernel,
        grid=(b,),
        in_specs=[
            pl.BlockSpec((None, m, d), lambda i: (i, 0, 0)),
            pl.BlockSpec((1, d), lambda i: (0, 0)),
            pl.BlockSpec((d, 2 * MEM_WIDTH), lambda i: (0, 0)),
            pl.BlockSpec((1, MEM_HEAD_DIM), lambda i: (0, 0)),
        ],
        out_specs=[pl.BlockSpec((None, m, MEM_WIDTH), lambda i: (i, 0, 0))] * 2,
        out_shape=[out, out],
        compiler_params=pltpu.CompilerParams(dimension_semantics=("parallel",)),
        name="mem_kv",
    )(mem, g, w_bf16, gk)


def _diff_attn_kernel(lam_ref, pos_ref, q_ref, k_ref, v_ref, gq_ref, gk_ref, go_ref, o_ref,
                      qa_scr, ka_scr, vt_scr, s_scr, *, lam_init, seq):
    tq = ATTN_TILE
    head = pl.program_id(1)
    slope = jnp.exp2(-(jnp.zeros((1, LANES), F32) + (head + 1).astype(F32)))[:, :1] * LOG2E

    lam_v = lam_ref[...]
    lam = (jnp.exp(jnp.sum(lam_v[0:1] * lam_v[1:2], axis=-1, keepdims=True))
           - jnp.exp(jnp.sum(lam_v[2:3] * lam_v[3:4], axis=-1, keepdims=True)) + lam_init)

    r_i = lax.broadcasted_iota(jnp.int32, (LANES, LANES), 0) // DA_QK_DIM
    c_i = lax.broadcasted_iota(jnp.int32, (LANES, LANES), 1) // DA_QK_DIM
    group_mean = jnp.where(r_i == c_i, 1.0 / DA_QK_DIM, 0.0).astype(BF16)

    def qk_norm(x_bf16, gain):
        xf = x_bf16.astype(F32)
        ms = _dot((xf * xf).astype(BF16), group_mean)
        return xf * (lax.rsqrt(ms + RMS_EPS) * gain)

    lane = lax.broadcasted_iota(jnp.int32, (tq, LANES), 1)
    in_map = (lane < DA_QK_DIM, lane >= DA_QK_DIM)
    lane_row = lax.broadcasted_iota(jnp.int32, (1, LANES), 1)
    ones_at_pieces = jnp.where(lane_row < ALIBI_PIECES, 1.0, 0.0)
    pow2 = jnp.exp2(-(jnp.zeros((1, LANES), F32) + (head + 1).astype(F32)))
    q_gain = gq_ref[...] * (DA_QK_DIM ** -0.5 * LOG2E)
    n_blocks = seq // tq

    def prepare(blk):
        rows = slice(blk * tq, (blk + 1) * tq)
        qn = qk_norm(q_ref[rows, :], q_gain)
        for m in range(2):
            q_rows = slice((2 * blk + m) * tq, (2 * blk + m + 1) * tq)
            qa_scr[q_rows, :LANES] = jnp.where(in_map[m], qn, 0.0).astype(BF16)
            qa_scr[q_rows, LANES:] = jnp.broadcast_to(ones_at_pieces, (tq, LANES)).astype(BF16)
        ka_scr[rows, :LANES] = qk_norm(k_ref[rows, :], gk_ref[...]).astype(BF16)
        ka_scr[rows, LANES:] = (pos_ref[rows, :] * pow2).astype(BF16)
        vt_scr[0:DA_V_DIM, rows] = v_ref[rows, :].astype(F32).T.astype(BF16)
        vt_scr[DA_V_DIM:, rows] = jnp.ones((ATTN_ONES_ROWS, tq), BF16)

    krel = lax.broadcasted_iota(jnp.int32, (tq, 2 * tq), 0)
    qrel = lax.broadcasted_iota(jnp.int32, (tq, 2 * tq), 1) % tq
    allowed = (krel // CHUNK) <= (qrel // CHUNK)
    diag_corr = slope * (-2.0 * jnp.maximum(krel - qrel, 0).astype(F32))

    def col_max_of(s):
        mx = s[0:8, :]
        for r0 in range(8, s.shape[0], 8):
            mx = jnp.maximum(mx, s[r0:r0 + 8, :])
        return jnp.max(mx, axis=0, keepdims=True)

    def scores(i):
        q_i = qa_scr[2 * i * tq:2 * (i + 1) * tq, :]
        diag = slice(i * tq, (i + 1) * tq)
        s = jnp.where(allowed, _dot_nt(ka_scr[diag, :], q_i) + diag_corr, MASK_VALUE)
        s_scr[i % ATTN_SCORE_BUFFERS, diag, :] = s
        cmax = col_max_of(s)
        if i > 0:
            s = _dot_nt(ka_scr[0:i * tq, :], q_i)
            s_scr[i % ATTN_SCORE_BUFFERS, 0:i * tq, :] = s
            cmax = jnp.maximum(cmax, col_max_of(s))
        return cmax

    def values(i, col_max):
        n_keys = (i + 1) * tq
        p = jnp.exp2(s_scr[i % ATTN_SCORE_BUFFERS, 0:n_keys, :] - col_max).astype(BF16)
        acc = _dot(vt_scr[:, 0:n_keys], p)
        ratio = acc[:DA_V_DIM] / acc[DA_V_DIM:DA_V_DIM + 1]
        o = ratio[:, :tq] - lam * ratio[:, tq:]
        o = o * lax.rsqrt(jnp.mean(o * o, axis=0, keepdims=True) + RMS_EPS) * go_ref[...] * (1.0 - lam_init)
        o_ref[i * tq:(i + 1) * tq, :] = o.T.astype(BF16)

    ahead = ATTN_SCORE_BUFFERS - 1
    col_max = {}
    prepared = [0]

    def prepare_through(blk):
        while prepared[0] <= min(blk, n_blocks - 1):
            prepare(prepared[0])
            prepared[0] += 1

    prepare_through(ATTN_PREP_LEAD - 1)
    for blk in range(min(ahead, n_blocks)):
        prepare_through(blk + ATTN_PREP_LEAD)
        col_max[blk] = scores(blk)
    for i in range(n_blocks):
        prepare_through(i + ahead + ATTN_PREP_LEAD)
        if i + ahead < n_blocks:
            col_max[i + ahead] = scores(i + ahead)
        values(i, col_max.pop(i))


ATTN_ONES_ROWS = 16
ATTN_SCORE_BUFFERS = 3
ATTN_PREP_LEAD = 2
ALIBI_PIECES = 4


def _alibi_position_table(seq):
    rest = np.arange(seq, dtype=np.float64) * LOG2E
    table = np.zeros((seq, LANES), np.float32)
    for p in range(ALIBI_PIECES):
        piece = rest.astype(np.float32).astype(BF16).astype(np.float32)
        table[:, p] = piece
        rest = rest - piece.astype(np.float64)
    return jnp.asarray(table)


def _diff_attn(heads4, lam_vecs, gq, gk, go_col, lam_init):
    _, b, s, _ = heads4.shape
    hd = 2 * DA_QK_DIM
    kern = functools.partial(_diff_attn_kernel, lam_init=lam_init, seq=s)
    est = (2 * 4 * s * hd * 2 + 4 * s * hd * 2 + (DA_V_DIM + ATTN_ONES_ROWS) * s * 2 + 2 * ATTN_SCORE_BUFFERS * ATTN_TILE * s * 4
           + 8 * s * hd * 4)
    return pl.pallas_call(
        kern,
        grid=(b, DA_HEADS),
        in_specs=[
            pl.BlockSpec((4, DA_QK_DIM), lambda bi, h: (0, 0)),
            pl.BlockSpec((s, LANES), lambda bi, h: (0, 0)),
            pl.BlockSpec((None, None, s, hd), lambda bi, h: (h, bi, 0, 0)),
            pl.BlockSpec((None, None, s, hd), lambda bi, h: (DA_HEADS + h, bi, 0, 0)),
            pl.BlockSpec((None, None, s, DA_V_DIM), lambda bi, h: (2 * DA_HEADS + h, bi, 0, 0)),
            pl.BlockSpec((1, hd), lambda bi, h: (0, 0)),
            pl.BlockSpec((1, hd), lambda bi, h: (0, 0)),
            pl.BlockSpec((DA_V_DIM, 1), lambda bi, h: (0, 0)),
        ],
        out_specs=pl.BlockSpec((None, s, DA_V_DIM), lambda bi, h: (bi, 0, h)),
        out_shape=jax.ShapeDtypeStruct((b, s, DA_HEADS * DA_V_DIM), BF16),
        scratch_shapes=[
            pltpu.VMEM((2 * s, 2 * hd), BF16),
            pltpu.VMEM((s, 2 * hd), BF16),
            pltpu.VMEM((DA_V_DIM + ATTN_ONES_ROWS, s), BF16),
            pltpu.VMEM((ATTN_SCORE_BUFFERS, s, 2 * ATTN_TILE), F32),
        ],
        compiler_params=pltpu.CompilerParams(
            dimension_semantics=("parallel", "parallel"), vmem_limit_bytes=_vmem_limit(est)),
        name="diff_attn",
    )(lam_vecs, _alibi_position_table(s), heads4, heads4, heads4, gq, gk, go_col)


def _mix_kernel(x_ref, pool_ref, halo_ref, mq_ref, oda_ref, mk_ref, mv_ref,
                gmix_ref, wg_ref, bg_ref, wpool_ref, pscale_ref, gmq_ref,
                wda_ref, wbp_ref, wbm_ref, wout_ref, o_ref, *, tiles_per_seq):
    tm = x_ref.shape[0]
    d = D_MODEL
    x = x_ref[...]
    hb = (_rms(x) * gmix_ref[...]).astype(BF16)

    tile_in_seq = pl.program_id(0) % tiles_per_seq
    p_cur = pool_ref[...].astype(F32)
    halo = jnp.where(tile_in_seq == 0, 0.0, halo_ref[...].astype(F32))
    ext = jnp.concatenate([halo, p_cur], axis=0)
    t_pos = tile_in_seq * tm + lax.broadcasted_iota(jnp.int32, (tm, 1), 0)
    pool_parts = []
    for g, w in enumerate(POOL_WINDOWS):
        c0 = g * POOL_GROUP_DIM
        s = ext[:, c0:c0 + POOL_GROUP_DIM]
        sh = 1
        while sh < w:
            s = s + pltpu.roll(s, sh, axis=0)
            sh *= 2
        cnt = jnp.minimum(t_pos + 1, w).astype(F32)
        dlt = (s[POOL_HALO:] / cnt - p_cur[:, c0:c0 + POOL_GROUP_DIM]).astype(BF16)
        pool_parts.append(_dot(dlt, wpool_ref[g]))
    o_pool = (jnp.concatenate(pool_parts, axis=1) * pscale_ref[...]).astype(BF16)

    mq = mq_ref[...].astype(F32)
    mem_parts = []
    for hh in range(MEM_HEADS):
        c0 = hh * MEM_HEAD_DIM
        qh = (_rms(mq[:, c0:c0 + MEM_HEAD_DIM]) * gmq_ref[...] * (MEM_HEAD_DIM ** -0.5)).astype(BF16)
        s = _dot_nt(qh, mk_ref[:, c0:c0 + MEM_HEAD_DIM])
        p = jnp.exp(s - jnp.max(s, axis=-1, keepdims=True))
        l = jnp.sum(p, axis=-1, keepdims=True)
        mem_parts.append(_dot(p.astype(BF16), mv_ref[:, c0:c0 + MEM_HEAD_DIM]) / l)
    o_mem = jnp.concatenate(mem_parts, axis=1).astype(BF16)

    def gate(k):
        return jax.nn.sigmoid(_dot(hb, wg_ref[:, k * d:(k + 1) * d]) + bg_ref[:, k * d:(k + 1) * d])

    merged = gate(0) * _dot(oda_ref[...], wda_ref[...])
    merged = merged + gate(1) * _dot(o_pool, wbp_ref[...])
    merged = merged + gate(2) * _dot(o_mem, wbm_ref[...])
    o_ref[...] = x + _dot(merged.astype(BF16), wout_ref[...])


def _mix(x, proj, o_da, mem_k, mem_v, gmix, wg, bg, wpool, pscale, gmq, wda, wbp, wbm, wout, seq):
    t, d = x.shape
    tm = ROW_TILE
    tiles_per_seq = seq // tm
    halo_blocks_per_tile = tm // POOL_HALO
    pool_col = (proj.shape[1] - POOL_WIDTH - MEM_WIDTH) // POOL_WIDTH
    mem_len = mem_k.shape[1]
    const = lambda i: (0, 0)
    weights_bytes = 2 * (wg.size + wda.size + wbp.size + wbm.size + wout.size + wpool.size)
    est = 2 * weights_bytes + 2 * (2 * tm * d * 4 + tm * (d + 2 * POOL_WIDTH) * 2) + 8 * tm * d * 4
    return pl.pallas_call(
        functools.partial(_mix_kernel, tiles_per_seq=tiles_per_seq),
        grid=(t // tm,),
        in_specs=[
            pl.BlockSpec((tm, d), lambda i: (i, 0)),
            pl.BlockSpec((tm, POOL_WIDTH), lambda i: (i, pool_col)),
            pl.BlockSpec((POOL_HALO, POOL_WIDTH),
                         lambda i: (jnp.maximum(i * halo_blocks_per_tile - 1, 0), pool_col)),
            pl.BlockSpec((tm, MEM_WIDTH), lambda i: (i, pool_col + 1)),
            pl.BlockSpec((tm, d), lambda i: (i, 0)),
            pl.BlockSpec((None, mem_len, MEM_WIDTH), lambda i: (i // tiles_per_seq, 0, 0)),
            pl.BlockSpec((None, mem_len, MEM_WIDTH), lambda i: (i // tiles_per_seq, 0, 0)),
            pl.BlockSpec((1, d), const),
            pl.BlockSpec(wg.shape, const, pipeline_mode=pl.Buffered(1)),
            pl.BlockSpec((1, bg.shape[1]), const),
            pl.BlockSpec(wpool.shape, lambda i: (0, 0, 0)),
            pl.BlockSpec((1, POOL_WIDTH), const),
            pl.BlockSpec((1, MEM_HEAD_DIM), const),
            pl.BlockSpec(wda.shape, const, pipeline_mode=pl.Buffered(1)),
            pl.BlockSpec(wbp.shape, const, pipeline_mode=pl.Buffered(1)),
            pl.BlockSpec(wbm.shape, const, pipeline_mode=pl.Buffered(1)),
            pl.BlockSpec(wout.shape, const, pipeline_mode=pl.Buffered(1)),
        ],
        out_specs=pl.BlockSpec((tm, d), lambda i: (i, 0)),
        out_shape=jax.ShapeDtypeStruct((t, d), F32),
        compiler_params=pltpu.CompilerParams(
            dimension_semantics=("parallel",), vmem_limit_bytes=_vmem_limit(est)),
        name="mix",
    )(x, proj, proj, proj, o_da, mem_k, mem_v, gmix, wg, bg, wpool, pscale, gmq, wda, wbp, wbm, wout)


def _split3(x):
    hi = x.astype(BF16)
    r = x - hi.astype(F32)
    mid = r.astype(BF16)
    lo = (r - mid.astype(F32)).astype(BF16)
    return hi, mid, lo


MOE_TILE = 512
SEG_MAIN = 192
SEG_OVER = MOE_TILE - SEG_MAIN
SEG_ALIGN = 16
NOT_ROUTED = -1e6
ROUTE_TILES_PER_STEP = 2
UP_COL_TILE = 1792
DOWN_COL_TILE = 1024
COPY_SIZES = (512, 256, 128, 64, 32, 16)


def _max_expert_tiles(t):
    padded_rows = 2 * t + (t // MOE_TILE) * N_EXPERTS * (SEG_ALIGN - 1)
    return padded_rows // MOE_TILE + N_EXPERTS


def _row_copies(src, dst, src_row0, dst_row0, n_rows, max_rows, sem):
    pairs = []
    pos = 0
    for size in COPY_SIZES:
        if size > max_rows:
            continue
        cond = jnp.bitwise_and(n_rows, size) != 0
        cp = pltpu.make_async_copy(
            src.at[pl.ds(pl.multiple_of(src_row0 + pos, SEG_ALIGN), size), :],
            dst.at[pl.ds(pl.multiple_of(dst_row0 + pos, SEG_ALIGN), size), :], sem)
        pairs.append((cond, cp))
        pos = pos + jnp.where(cond, size, 0)
    return pairs


def _start_all(pairs):
    for cond, cp in pairs:
        pl.when(cond)(cp.start)


def _wait_all(pairs):
    for cond, cp in pairs:
        pl.when(cond)(cp.wait)


def _onehot_rows(rank_row, n_rows, row0):
    r = lax.broadcasted_iota(jnp.int32, (n_rows, rank_row.shape[1]), 0).astype(F32) + float(row0)
    return jnp.where(r == rank_row, 1.0, 0.0).astype(BF16)


def _route_kernel(x_ref, g_ref, wr_ref, rank_ref, meta_ref, cnts_ref):
    for sub in range(ROUTE_TILES_PER_STEP):
        rows = slice(sub * MOE_TILE, (sub + 1) * MOE_TILE)
        _route_tile(x_ref.at[rows, :], g_ref, wr_ref, rank_ref.at[:, rows], meta_ref.at[rows, :], cnts_ref,
                    pl.program_id(0) * ROUTE_TILES_PER_STEP + sub)


def _route_tile(x_ref, g_ref, wr_ref, rank_ref, meta_ref, cnts_ref, i):
    tm = x_ref.shape[0]
    h2 = _rms(x_ref[...]) * g_ref[...]
    h_hi, h_mid, _ = _split3(h2)
    w_hi, w_mid, _ = _split3(wr_ref[...])
    by_h_hi = _dot_nt(jnp.concatenate([w_hi, w_mid], axis=0), h_hi)
    logits = by_h_hi[:N_EXPERTS] + (_dot_nt(w_hi, h_mid) + by_h_hi[N_EXPERTS:])
    eidx = lax.broadcasted_iota(jnp.int32, logits.shape, 0).astype(F32)
    m1 = jnp.max(logits, axis=0, keepdims=True)
    i1 = jnp.min(jnp.where(logits == m1, eidx, float(N_EXPERTS)), axis=0, keepdims=True)
    sel1 = eidx == i1
    rest = jnp.where(sel1, -jnp.inf, logits)
    m2 = jnp.max(rest, axis=0, keepdims=True)
    i2 = jnp.min(jnp.where(rest == m2, eidx, float(N_EXPERTS)), axis=0, keepdims=True)
    sel2 = eidx == i2
    e2 = jnp.exp(m2 - m1)
    g1 = 1.0 / (1.0 + e2)
    g2 = e2 / (1.0 + e2)

    mask = jnp.where(sel1 | sel2, 1.0, 0.0)
    before = (lax.broadcasted_iota(jnp.int32, (tm, tm), 0)
              < lax.broadcasted_iota(jnp.int32, (tm, tm), 1)).astype(BF16)
    rank = _dot(mask.astype(BF16), before)
    counts = jnp.sum(mask, axis=1, keepdims=True)
    rank_m = jnp.where(mask > 0.5, rank, NOT_ROUTED)

    rk1 = jnp.where(sel1, rank, NOT_ROUTED)
    rk2 = jnp.where(sel2, rank, NOT_ROUTED)
    gates = jnp.where(eidx == 0.0, g1, jnp.where(eidx == 1.0, g2, 0.0))
    packed = jnp.concatenate([rk1, rk2, gates], axis=0)
    ident = (lax.broadcasted_iota(jnp.int32, (tm, tm), 0)
             == lax.broadcasted_iota(jnp.int32, (tm, tm), 1)).astype(BF16)
    p_hi, p_mid, p_lo = _split3(packed)
    meta_ref[...] = _dot_nt(ident, p_hi) + (_dot_nt(ident, p_mid) + _dot_nt(ident, p_lo))

    rank_ref[...] = rank_m
    for e in range(N_EXPERTS):
        cnt = counts[e, 0].astype(jnp.int32)
        cnts_ref[i * N_EXPERTS + e] = jnp.bitwise_and(cnt + (SEG_ALIGN - 1), -SEG_ALIGN)


def _route(x, g, w_router_t):
    t, d = x.shape
    tm = MOE_TILE * ROUTE_TILES_PER_STEP
    nt = t // MOE_TILE
    est = 2 * tm * d * 4 + ROUTE_TILES_PER_STEP * (6 * MOE_TILE * MOE_TILE * 4 + 4 * MOE_TILE * d * 4)
    return pl.pallas_call(
        _route_kernel,
        grid=(t // tm,),
        in_specs=[
            pl.BlockSpec((tm, d), lambda i: (i, 0)),
            pl.BlockSpec((1, d), lambda i: (0, 0)),
            pl.BlockSpec((N_EXPERTS, d), lambda i: (0, 0)),
        ],
        out_specs=[
            pl.BlockSpec((N_EXPERTS, tm), lambda i: (0, i)),
            pl.BlockSpec((tm, 3 * N_EXPERTS), lambda i: (i, 0)),
            pl.BlockSpec(memory_space=pltpu.SMEM),
        ],
        out_shape=[
            jax.ShapeDtypeStruct((N_EXPERTS, t), F32),
            jax.ShapeDtypeStruct((t, 3 * N_EXPERTS), F32),
            jax.ShapeDtypeStruct((nt * N_EXPERTS,), jnp.int32),
        ],
        compiler_params=pltpu.CompilerParams(
            dimension_semantics=("arbitrary",), vmem_limit_bytes=_vmem_limit(est)),
        name="route",
    )(x, g, w_router_t)


def _scatter_kernel(offs_ref, cnts_ref, fill_ref, nv_ref, x_ref, g_ref, rank_ref, xs_hbm,
                    stage, over_stage, zero_buf, sems, over_sem, *, n_tiles):
    i = pl.program_id(0)
    hb = (_rms(x_ref[...]) * g_ref[...]).astype(BF16)
    rank_m = rank_ref[...]

    def main_copies(step, e):
        n_main = jnp.minimum(cnts_ref[step * N_EXPERTS + e], SEG_MAIN)
        return _row_copies(stage.at[step % 2, e], xs_hbm, 0, offs_ref[step * N_EXPERTS + e], n_main,
                           SEG_MAIN, sems.at[step % 2, e])

    @pl.when(i > 1)
    def _():
        for e in range(N_EXPERTS):
            _wait_all(main_copies(i - 2, e))

    onehot_all = jnp.concatenate(
        [_onehot_rows(rank_m[e:e + 1, :], SEG_MAIN, 0) for e in range(N_EXPERTS)], axis=0)
    compact = _dot(onehot_all, hb).astype(BF16)
    for e in range(N_EXPERTS):
        rank_e = rank_m[e:e + 1, :]
        stage[i % 2, e] = compact[e * SEG_MAIN:(e + 1) * SEG_MAIN, :]
        _start_all(main_copies(i, e))
        n_over = cnts_ref[i * N_EXPERTS + e] - SEG_MAIN

        @pl.when(n_over > 0)
        def _():
            over_stage[...] = _dot(_onehot_rows(rank_e, SEG_OVER, SEG_MAIN), hb).astype(BF16)
            pairs = _row_copies(over_stage, xs_hbm, 0, offs_ref[i * N_EXPERTS + e] + SEG_MAIN, n_over,
                                SEG_OVER, over_sem)
            _start_all(pairs)
            _wait_all(pairs)

    @pl.when(i == pl.num_programs(0) - 1)
    def _():
        for e in range(N_EXPERTS):
            _wait_all(main_copies(i - 1, e))
            _wait_all(main_copies(i, e))
        zero_buf[...] = jnp.zeros_like(zero_buf)
        for e in range(N_EXPERTS):
            pairs = _row_copies(zero_buf, xs_hbm, 0, fill_ref[e], fill_ref[N_EXPERTS + e], MOE_TILE,
                                sems.at[0, e])
            _start_all(pairs)
            _wait_all(pairs)

        def zero_tile(tile, carry):
            dst = xs_hbm.at[pl.ds(pl.multiple_of(tile * MOE_TILE, MOE_TILE), MOE_TILE), :]
            cp = pltpu.make_async_copy(zero_buf, dst, over_sem)
            cp.start()
            cp.wait()
            return carry

        lax.fori_loop(nv_ref[0], n_tiles, zero_tile, 0)


def _scatter(x, g, rank, offs, cnts, fill, n_valid, n_tiles):
    t, d = x.shape
    tm = MOE_TILE
    assert t // tm >= 2, "the segment copies are waited two grid steps after they start"
    est = 2 * tm * d * 4 + (2 * N_EXPERTS * SEG_MAIN + SEG_OVER + MOE_TILE) * d * 2 + 6 * tm * d * 4
    grid_spec = pltpu.PrefetchScalarGridSpec(
        num_scalar_prefetch=4,
        grid=(t // tm,),
        in_specs=[
            pl.BlockSpec((tm, d), lambda i, *_: (i, 0)),
            pl.BlockSpec((1, d), lambda i, *_: (0, 0)),
            pl.BlockSpec((N_EXPERTS, tm), lambda i, *_: (0, i)),
        ],
        out_specs=pl.BlockSpec(memory_space=pl.ANY),
        scratch_shapes=[
            pltpu.VMEM((2, N_EXPERTS, SEG_MAIN, d), BF16),
            pltpu.VMEM((SEG_OVER, d), BF16),
            pltpu.VMEM((MOE_TILE, d), BF16),
            pltpu.SemaphoreType.DMA((2, N_EXPERTS)),
            pltpu.SemaphoreType.DMA(()),
        ],
    )
    return pl.pallas_call(
        functools.partial(_scatter_kernel, n_tiles=n_tiles),
        grid_spec=grid_spec,
        out_shape=jax.ShapeDtypeStruct((n_tiles * MOE_TILE, d), BF16),
        compiler_params=pltpu.CompilerParams(
            dimension_semantics=("arbitrary",), vmem_limit_bytes=_vmem_limit(est)),
        name="scatter",
    )(offs, cnts, fill, n_valid, x, g, rank)


def _expert_changed(te_ref, t):
    return (t == 0) | (te_ref[t] != te_ref[jnp.maximum(t - 1, 0)])


def _expert_up_kernel(te_ref, nv_ref, x_ref, w1_ref, w3_ref, h_ref, w1b, w3b):
    t = pl.program_id(1)

    @pl.when(_expert_changed(te_ref, t))
    def _():
        w1b[...] = w1_ref[...].astype(BF16)
        w3b[...] = w3_ref[...].astype(BF16)

    @pl.when(t < nv_ref[0])
    def _():
        x = x_ref[...]
        a = _dot(x, w1b[...])
        b = _dot(x, w3b[...])
        h_ref[...] = (a * jax.nn.sigmoid(a) * b).astype(BF16)

    @pl.when(t >= nv_ref[0])
    def _():
        h_ref[...] = jnp.zeros_like(h_ref)


def _expert_up(xs, w1, w3, tile_expert, n_valid):
    rows, d = xs.shape
    _, _, f_dim = w1.shape
    tf = UP_COL_TILE
    est = 2 * (MOE_TILE * d * 2 + 2 * d * tf * 4 + MOE_TILE * tf * 2) + 2 * d * tf * 2 + 3 * MOE_TILE * tf * 4
    grid_spec = pltpu.PrefetchScalarGridSpec(
        num_scalar_prefetch=2,
        grid=(f_dim // tf, rows // MOE_TILE),
        in_specs=[
            pl.BlockSpec((MOE_TILE, d), lambda f, t, te, nv: (t, 0)),
            pl.BlockSpec((None, d, tf), lambda f, t, te, nv: (te[t], 0, f)),
            pl.BlockSpec((None, d, tf), lambda f, t, te, nv: (te[t], 0, f)),
        ],
        out_specs=pl.BlockSpec((MOE_TILE, tf), lambda f, t, te, nv: (t, f)),
        scratch_shapes=[pltpu.VMEM((d, tf), BF16), pltpu.VMEM((d, tf), BF16)],
    )
    return pl.pallas_call(
        _expert_up_kernel,
        grid_spec=grid_spec,
        out_shape=jax.ShapeDtypeStruct((rows, f_dim), BF16),
        compiler_params=pltpu.CompilerParams(
            dimension_semantics=("arbitrary", "arbitrary"), vmem_limit_bytes=_vmem_limit(est)),
        name="expert_up",
    )(tile_expert, n_valid, xs, w1, w3)


def _expert_down_kernel(te_ref, nv_ref, h_ref, w2_ref, y_ref, w2b):
    t = pl.program_id(1)

    @pl.when(_expert_changed(te_ref, t))
    def _():
        w2b[...] = w2_ref[...].astype(BF16)

    @pl.when(t < nv_ref[0])
    def _():
        y_ref[...] = _dot(h_ref[...], w2b[...]).astype(BF16)

    @pl.when(t >= nv_ref[0])
    def _():
        y_ref[...] = jnp.zeros_like(y_ref)


def _expert_down(hs, w2, tile_expert, n_valid):
    rows, f_dim = hs.shape
    d = w2.shape[2]
    tn = DOWN_COL_TILE
    est = 2 * (MOE_TILE * f_dim * 2 + f_dim * tn * 4 + MOE_TILE * tn * 2) + f_dim * tn * 2 + MOE_TILE * tn * 4
    grid_spec = pltpu.PrefetchScalarGridSpec(
        num_scalar_prefetch=2,
        grid=(d // tn, rows // MOE_TILE),
        in_specs=[
            pl.BlockSpec((MOE_TILE, f_dim), lambda n, t, te, nv: (t, 0)),
            pl.BlockSpec((None, f_dim, tn), lambda n, t, te, nv: (te[t], 0, n)),
        ],
        out_specs=pl.BlockSpec((MOE_TILE, tn), lambda n, t, te, nv: (t, n)),
        scratch_shapes=[pltpu.VMEM((f_dim, tn), BF16)],
    )
    return pl.pallas_call(
        _expert_down_kernel,
        grid_spec=grid_spec,
        out_shape=jax.ShapeDtypeStruct((rows, d), BF16),
        compiler_params=pltpu.CompilerParams(
            dimension_semantics=("arbitrary", "arbitrary"), vmem_limit_bytes=_vmem_limit(est)),
        name="expert_down",
    )(tile_expert, n_valid, hs, w2)


def _combine_kernel(offs_ref, cnts_ref, x_ref, meta_ref, y_hbm, o_ref,
                    ybuf, over_buf, over_acc, sems, over_sem):
    i = pl.program_id(0)
    tm = x_ref.shape[0]
    total_rows = y_hbm.shape[0]
    meta = meta_ref[...]
    g1 = meta[:, 2 * N_EXPERTS:2 * N_EXPERTS + 1]
    g2 = meta[:, 2 * N_EXPERTS + 1:2 * N_EXPERTS + 2]

    def window(step, e, seg_row0, n_rows):
        off = offs_ref[step * N_EXPERTS + e] + seg_row0
        start = jnp.minimum(off, total_rows - n_rows)
        return start, off - start

    def main_window_copy(step, e):
        start, _ = window(step, e, 0, SEG_MAIN)
        src = y_hbm.at[pl.ds(pl.multiple_of(start, SEG_ALIGN), SEG_MAIN), :]
        return pltpu.make_async_copy(src, ybuf.at[step % 2, e], sems.at[step % 2, e])

    @pl.when(i == 0)
    def _():
        for e in range(N_EXPERTS):
            main_window_copy(i, e).start()

    @pl.when(i + 1 < pl.num_programs(0))
    def _():
        for e in range(N_EXPERTS):
            main_window_copy(i + 1, e).start()

    def weights(e, shift, n_cols):
        c = lax.broadcasted_iota(jnp.int32, (tm, n_cols), 1).astype(F32)
        first = meta[:, e:e + 1] + shift
        second = meta[:, N_EXPERTS + e:N_EXPERTS + e + 1] + shift
        return jnp.where(c == first, g1, jnp.where(c == second, g2, 0.0)).astype(BF16)

    out = x_ref[...]
    any_over = cnts_ref[i * N_EXPERTS] > SEG_MAIN
    for e in range(N_EXPERTS):
        _, shift = window(i, e, 0, SEG_MAIN)
        w_e = weights(e, shift.astype(F32), SEG_MAIN)
        main_window_copy(i, e).wait()
        out = out + _dot(w_e, ybuf[i % 2, e])
        if e > 0:
            any_over = any_over | (cnts_ref[i * N_EXPERTS + e] > SEG_MAIN)
    o_ref[...] = out

    @pl.when(any_over)
    def _():
        over_acc[...] = jnp.zeros_like(over_acc)
        for e in range(N_EXPERTS):
            @pl.when(cnts_ref[i * N_EXPERTS + e] > SEG_MAIN)
            def _():
                start, shift = window(i, e, SEG_MAIN, SEG_OVER)
                src = y_hbm.at[pl.ds(pl.multiple_of(start, SEG_ALIGN), SEG_OVER), :]
                cp = pltpu.make_async_copy(src, over_buf, over_sem)
                cp.start()
                cp.wait()
                over_acc[...] += _dot(weights(e, shift.astype(F32) - float(SEG_MAIN), SEG_OVER), over_buf[...])
        o_ref[...] += over_acc[...]


def _combine(x, meta, ys, offs, cnts):
    t, d = x.shape
    tm = MOE_TILE
    est = (4 * tm * d * 4 + (2 * N_EXPERTS * SEG_MAIN + SEG_OVER) * d * 2 + 2 * tm * d * 4
           + 2 * tm * LANES * 4 + 6 * tm * d * 4)
    grid_spec = pltpu.PrefetchScalarGridSpec(
        num_scalar_prefetch=2,
        grid=(t // tm,),
        in_specs=[
            pl.BlockSpec((tm, d), lambda i, *_: (i, 0)),
            pl.BlockSpec((tm, 3 * N_EXPERTS), lambda i, *_: (i, 0)),
            pl.BlockSpec(memory_space=pl.ANY),
        ],
        out_specs=pl.BlockSpec((tm, d), lambda i, *_: (i, 0)),
        scratch_shapes=[
            pltpu.VMEM((2, N_EXPERTS, SEG_MAIN, d), BF16),
            pltpu.VMEM((SEG_OVER, d), BF16),
            pltpu.VMEM((tm, d), F32),
            pltpu.SemaphoreType.DMA((2, N_EXPERTS)),
            pltpu.SemaphoreType.DMA(()),
        ],
    )
    return pl.pallas_call(
        _combine_kernel,
        grid_spec=grid_spec,
        out_shape=jax.ShapeDtypeStruct((t, d), F32),
        compiler_params=pltpu.CompilerParams(
            dimension_semantics=("arbitrary",), vmem_limit_bytes=_vmem_limit(est)),
        name="combine",
    )(offs, cnts, x, meta, ys)


def _moe(x, g, w_router, w1, w3, w2):
    t, _ = x.shape
    nt = t // MOE_TILE
    n_tiles = _max_expert_tiles(t)
    rank, meta, cnts = _route(x, g, w_router.T)
    cnt2 = cnts.reshape(nt, N_EXPERTS)
    rows_e = jnp.sum(cnt2, axis=0)
    tiles_e = jnp.maximum((rows_e + MOE_TILE - 1) // MOE_TILE, 1)
    ends = jnp.cumsum(tiles_e)
    base_e = (ends - tiles_e) * MOE_TILE
    offs = (base_e[None, :] + jnp.cumsum(cnt2, axis=0) - cnt2).reshape(-1).astype(jnp.int32)
    fill = jnp.concatenate([base_e + rows_e, tiles_e * MOE_TILE - rows_e]).astype(jnp.int32)
    n_valid = ends[-1].reshape(1).astype(jnp.int32)
    tid = jnp.arange(n_tiles, dtype=jnp.int32)
    tile_expert = jnp.minimum(jnp.sum(tid[:, None] >= ends[None, :], axis=1), N_EXPERTS - 1).astype(jnp.int32)
    xs = _scatter(x, g, rank, offs, cnts, fill, n_valid, n_tiles)
    hs = _expert_up(xs, w1, w3, tile_expert, n_valid)
    ys = _expert_down(hs, w2, tile_expert, n_valid)
    return _combine(x, meta, ys, offs, cnts)


def _ffn_kernel(x_ref, g_ref, w1_ref, w3_ref, w2_ref, o_ref):
    x = x_ref[...]
    hb = (_rms(x) * g_ref[...]).astype(BF16)
    f_dim = w1_ref.shape[1]
    out = x
    for c0 in range(0, f_dim, FFN_HIDDEN_CHUNK):
        c1 = min(c0 + FFN_HIDDEN_CHUNK, f_dim)
        a = _dot(hb, w1_ref[:, c0:c1])
        b = _dot(hb, w3_ref[:, c0:c1])
        out = out + _dot((a * jax.nn.sigmoid(a) * b).astype(BF16), w2_ref[c0:c1, :])
    o_ref[...] = out


def _ffn(x, g, w1, w3, w2):
    t, d = x.shape
    f_dim = w1.shape[1]
    tm = FFN_ROW_TILE
    resident = pl.Buffered(1)
    est = 3 * d * f_dim * 2 + 4 * tm * d * 4 + 4 * tm * FFN_HIDDEN_CHUNK * 4
    return pl.pallas_call(
        _ffn_kernel,
        grid=(t // tm,),
        in_specs=[
            pl.BlockSpec((tm, d), lambda i: (i, 0)),
            pl.BlockSpec((1, d), lambda i: (0, 0)),
            pl.BlockSpec((d, f_dim), lambda i: (0, 0), pipeline_mode=resident),
            pl.BlockSpec((d, f_dim), lambda i: (0, 0), pipeline_mode=resident),
            pl.BlockSpec((f_dim, d), lambda i: (0, 0), pipeline_mode=resident),
        ],
        out_specs=pl.BlockSpec((tm, d), lambda i: (i, 0)),
        out_shape=jax.ShapeDtypeStruct((t, d), F32),
        compiler_params=pltpu.CompilerParams(
            dimension_semantics=("parallel",), vmem_limit_bytes=_vmem_limit(est)),
        name="ffn",
    )(x, g, w1, w3, w2)


def _lambda_init(layer):
    return 0.8 - 0.6 * math.exp(-0.3 * layer)


def kernel(x, mem, norm_mix, norm_mem, norm_ffn, w_in, w_gate, b_gate, da_q_norm, da_k_norm,
           da_lam_q1, da_lam_k1, da_lam_q2, da_lam_k2, da_out_norm, pool_w, pool_scale,
           mem_q_norm, mem_k_norm, w_mem_kv, w_br_da, w_br_pool, w_br_mem, w_out,
           ffn_w1, ffn_w3, ffn_w2, moe_router, moe_w1, moe_w3, moe_w2):
    b, s, d = x.shape
    depth = w_in.shape[0]
    xt = x.reshape(b * s, d)
    bf = lambda a: a.astype(BF16)
    row = lambda v: v.reshape(1, -1)
    for l in range(depth):
        lam0 = _lambda_init(l)
        heads, proj = _in_proj(xt, row(norm_mix[l]), w_in, l)
        mem_k, mem_v = _mem_kv(mem, row(norm_mem[l]), bf(w_mem_kv[l]), row(mem_k_norm[l]))
        lam_vecs = jnp.stack([da_lam_q1[l], da_lam_k1[l], da_lam_q2[l], da_lam_k2[l]])
        o_da = _diff_attn(heads.reshape(-1, b, s, LANES), lam_vecs, row(da_q_norm[l]), row(da_k_norm[l]),
                          da_out_norm[l].reshape(-1, 1), lam0)
        xt = _mix(xt, proj, o_da.reshape(b * s, -1), mem_k, mem_v, row(norm_mix[l]), bf(w_gate[l]),
                  row(b_gate[l]), bf(pool_w[l]), row(pool_scale[l]), row(mem_q_norm[l]),
                  bf(w_br_da[l]), bf(w_br_pool[l]), bf(w_br_mem[l]), bf(w_out[l]), s)
        j = l // 2
        if l % 2 == 0:
            xt = _ffn(xt, row(norm_ffn[l]), bf(ffn_w1[j]), bf(ffn_w3[j]), bf(ffn_w2[j]))
        else:
            xt = _moe(xt, row(norm_ffn[l]), moe_router[j], moe_w1[j], moe_w3[j], moe_w2[j])
    return xt.reshape(b, s, d)
```
